```python
import math
import jax, jax.numpy as jnp
from jax import lax
import numpy as np

D_MODEL = 1024
BATCH = 4
SEQ = 8192
DEPTH = 4

N_BRANCH = 4
BRANCH_W = 256
GM_GROUPS = 4
GM_CHUNK = 128
GM_GD = BRANCH_W // GM_GROUPS
POOL_WINDOWS = (2, 4, 8, 16)
POOL_GD = BRANCH_W // len(POOL_WINDOWS)
ATT_HEADS = 4
ATT_HD = BRANCH_W // ATT_HEADS
IDX_HEADS = 4
IDX_HD = 32
TOPK_MAX = 256
Q_BLOCK = 128
REL_BUCKETS = 32
REL_MAX_DIST = 128
CONV_K = 31
D_FF = 2816
EPS = 1e-6

IN_SIZES = (2 * BRANCH_W, BRANCH_W, BRANCH_W, BRANCH_W, BRANCH_W, IDX_HEADS * IDX_HD, IDX_HD, IDX_HEADS, 2 * BRANCH_W, N_BRANCH * D_MODEL)
IN_COLS = 2 * BRANCH_W + 4 * BRANCH_W + IDX_HEADS * IDX_HD + IDX_HD + IDX_HEADS + 2 * BRANCH_W + N_BRANCH * D_MODEL

kernel_name = 'hybrid_gated_parallel_mixers'


def rmsnorm(x, g):
    xf = x.astype(jnp.float32)
    y = xf * lax.rsqrt(jnp.mean(xf * xf, axis=-1, keepdims=True) + EPS)
    return (y * g.astype(jnp.float32)).astype(x.dtype)


def layernorm(x, g, b):
    xf = x.astype(jnp.float32)
    mu = jnp.mean(xf, axis=-1, keepdims=True)
    xc = xf - mu
    y = xc * lax.rsqrt(jnp.mean(xc * xc, axis=-1, keepdims=True) + EPS)
    return (y * g.astype(jnp.float32) + b.astype(jnp.float32)).astype(x.dtype)


def swiglu_ffn(h, w_gu, w_down):
    a, b = jnp.split(h @ w_gu, 2, axis=-1)
    return (jax.nn.silu(a) * b) @ w_down


def gmlp_mixer(uv, v_g, ws, wb):
    B, S, _ = uv.shape
    u, v = jnp.split(jax.nn.gelu(uv), 2, axis=-1)
    v = rmsnorm(v, v_g)
    vc = v.reshape(B, S // GM_CHUNK, GM_CHUNK, GM_GROUPS, GM_GD)
    causal = jnp.tril(jnp.ones((GM_CHUNK, GM_CHUNK), dtype=bool))
    wsm = jnp.where(causal[None], ws, 0.0).astype(v.dtype)
    mixed = jnp.einsum('gts,bcsgd->bctgd', wsm, vc) + wb.T[None, None, :, :, None]
    return u * mixed.reshape(B, S, BRANCH_W)


def pool_mixer(p, pw, scale):
    B, S, _ = p.shape
    wmax = POOL_WINDOWS[-1]
    pf = p.astype(jnp.float32)
    csp = jnp.pad(jnp.cumsum(pf, axis=1), ((0, 0), (wmax, 0), (0, 0)))
    count = jnp.arange(1, S + 1, dtype=jnp.float32)[:, None]
    outs = []
    for g, w in enumerate(POOL_WINDOWS):
        sl = slice(g * POOL_GD, (g + 1) * POOL_GD)
        win = csp[:, wmax:, sl] - csp[:, wmax - w:wmax - w + S, sl]
        outs.append(win / jnp.minimum(count, float(w)) - pf[:, :, sl])
    d = jnp.stack(outs, axis=2).astype(p.dtype)
    y = jnp.einsum('bsgc,gcd->bsgd', d, pw).reshape(B, S, BRANCH_W)
    return y * scale


def conv_mixer(ab, dw, db, ln_g, ln_b):
    a, gt = jnp.split(ab, 2, axis=-1)
    h = a * jax.nn.sigmoid(gt)
    h = lax.conv_general_dilated(h, dw[:, None, :].astype(h.dtype), window_strides=(1,),
                                 padding=[(CONV_K - 1, 0)], dimension_numbers=('NWC', 'WIO', 'NWC'),
                                 feature_group_count=BRANCH_W) + db
    return jax.nn.silu(layernorm(h, ln_g, ln_b))


def t5_bucket(n):
    n = jnp.maximum(n, 0)
    max_exact = REL_BUCKETS // 2
    large = max_exact + (jnp.log(jnp.maximum(n, 1).astype(jnp.float32) / max_exact)
                         / math.log(REL_MAX_DIST / max_exact) * (REL_BUCKETS - max_exact)).astype(jnp.int32)
    return jnp.where(n < max_exact, n, jnp.minimum(large, REL_BUCKETS - 1))


def sparse_attention(q, k, v, q_idx, k_idx, w_idx, rel_bias):
    B, S, _ = q.shape
    topk = min(TOPK_MAX, S // 4)
    q = q.reshape(B, S, ATT_HEADS, ATT_HD)
    k = k.reshape(B, S, ATT_HEADS, ATT_HD)
    v = v.reshape(B, S, ATT_HEADS, ATT_HD)
    q_idx = q_idx.reshape(B, S, IDX_HEADS, IDX_HD)
    w_idx = w_idx.astype(jnp.float32) * (IDX_HEADS ** -0.5)
    s_pos = jnp.arange(S)
    gather = jax.vmap(lambda a, i: a[i])

    def block(i):
        t0 = i * Q_BLOCK
        t_pos = t0 + jnp.arange(Q_BLOCK)
        qb = lax.dynamic_slice_in_dim(q, t0, Q_BLOCK, axis=1)
        qib = lax.dynamic_slice_in_dim(q_idx, t0, Q_BLOCK, axis=1)
        wib = lax.dynamic_slice_in_dim(w_idx, t0, Q_BLOCK, axis=1)
        rel = jax.nn.relu(jnp.einsum('bqhd,bsd->bqhs', qib, k_idx).astype(jnp.float32) * (IDX_HD ** -0.5))
        score = jnp.einsum('bqh,bqhs->bqs', wib, rel)
        score = jnp.where(s_pos[None, None, :] <= t_pos[None, :, None], score, -jnp.inf)
        _, idx = lax.top_k(score, topk)
        valid = idx <= t_pos[None, :, None]
        ks = gather(k, idx)
        vs = gather(v, idx)
        bias = rel_bias[t5_bucket(t_pos[None, :, None] - idx)]
        logits = (jnp.einsum('bqhd,bqkhd->bqhk', qb, ks).astype(jnp.float32) * (ATT_HD ** -0.5)
                  + bias.astype(jnp.float32).transpose(0, 1, 3, 2))
        logits = jnp.where(valid[:, :, None, :], logits, -jnp.inf)
        p = jax.nn.softmax(logits, axis=-1).astype(v.dtype)
        o = jnp.einsum('bqhk,bqkhd->bqhd', p, vs)
        return o.reshape(B, Q_BLOCK, BRANCH_W)

    out = lax.map(block, jnp.arange(S // Q_BLOCK))
    return out.transpose(1, 0, 2, 3).reshape(B, S, BRANCH_W)


def mixing_sublayer(h, w_in, gm_v_g, gm_ws, gm_b, pool_w, pool_scale, conv_dw, conv_b,
                    conv_ln_g, conv_ln_b, w_branch, w_out, rel_bias):
    B, S, _ = h.shape
    z = h @ w_in
    uv, pz, qz, kz, vz, qi, ki, wi, cz, gz = jnp.split(z, list(np.cumsum(IN_SIZES)[:-1]), axis=-1)
    branches = (
        gmlp_mixer(uv, gm_v_g, gm_ws, gm_b),
        pool_mixer(pz, pool_w, pool_scale),
        sparse_attention(qz, kz, vz, qi, ki, wi, rel_bias),
        conv_mixer(cz, conv_dw, conv_b, conv_ln_g, conv_ln_b),
    )
    gates = jax.nn.sigmoid(gz.reshape(B, S, N_BRANCH, D_MODEL))
    y = gates[:, :, 0] * (branches[0] @ w_branch[0])
    for n in range(1, N_BRANCH):
        y = y + gates[:, :, n] * (branches[n] @ w_branch[n])
    return y @ w_out


def setup_inputs(seed: int = 0) -> dict:
    key = jax.random.key(seed)
    ks = jax.random.split(key, 24)
    f32 = jnp.float32

    def nrm(k, shape, scale):
        return jax.random.normal(k, shape, f32) * scale

    def gain(k, shape):
        return 1.0 + 0.05 * jax.random.normal(k, shape, f32)

    L, D, C, F = DEPTH, D_MODEL, BRANCH_W, D_FF
    return {
        'x': nrm(ks[0], (BATCH, SEQ, D), 1.0),
        'ffn1_pre_g': gain(ks[1], (L, D)),
        'ffn1_post_g': gain(ks[2], (L, D)),
        'ffn1_w_gu': nrm(ks[3], (L, D, 2 * F), D ** -0.5),
        'ffn1_w_down': nrm(ks[4], (L, F, D), F ** -0.5),
        'mix_pre_g': gain(ks[5], (L, D)),
        'mix_post_g': gain(ks[6], (L, D)),
        'w_in': nrm(ks[7], (L, D, IN_COLS), D ** -0.5),
        'gm_v_g': gain(ks[8], (L, C)),
        'gm_ws': nrm(ks[9], (L, GM_GROUPS, GM_CHUNK, GM_CHUNK), GM_CHUNK ** -0.5),
        'gm_b': 1.0 + 0.1 * jax.random.normal(ks[10], (L, GM_GROUPS, GM_CHUNK), f32),
        'pool_w': nrm(ks[11], (L, len(POOL_WINDOWS), POOL_GD, POOL_GD), POOL_GD ** -0.5),
        'pool_scale': gain(ks[12], (L, C)),
        'conv_dw': nrm(ks[13], (L, CONV_K, C), CONV_K ** -0.5),
        'conv_b': nrm(ks[14], (L, C), 0.02),
        'conv_ln_g': gain(ks[15], (L, C)),
        'conv_ln_b': nrm(ks[16], (L, C), 0.02),
        'w_branch': nrm(ks[17], (L, N_BRANCH, C, D), C ** -0.5),
        'w_out': nrm(ks[18], (L, D, D), D ** -0.5),
        'ffn2_pre_g': gain(ks[19], (L, D)),
        'ffn2_post_g': gain(ks[20], (L, D)),
        'ffn2_w_gu': nrm(ks[21], (L, D, 2 * F), D ** -0.5),
        'ffn2_w_down': nrm(ks[22], (L, F, D), F ** -0.5),
        'rel_bias': nrm(ks[23], (REL_BUCKETS, ATT_HEADS), 0.5),
    }


def reference(x, ffn1_pre_g, ffn1_post_g, ffn1_w_gu, ffn1_w_down, mix_pre_g, mix_post_g, w_in,
              gm_v_g, gm_ws, gm_b, pool_w, pool_scale, conv_dw, conv_b, conv_ln_g, conv_ln_b,
              w_branch, w_out, ffn2_pre_g, ffn2_post_g, ffn2_w_gu, ffn2_w_down, rel_bias):
    for l in range(DEPTH):
        h = swiglu_ffn(rmsnorm(x, ffn1_pre_g[l]), ffn1_w_gu[l], ffn1_w_down[l])
        x = x + 0.5 * rmsnorm(h, ffn1_post_g[l])
        h = mixing_sublayer(rmsnorm(x, mix_pre_g[l]), w_in[l], gm_v_g[l], gm_ws[l], gm_b[l],
                            pool_w[l], pool_scale[l], conv_dw[l], conv_b[l], conv_ln_g[l],
                            conv_ln_b[l], w_branch[l], w_out[l], rel_bias)
        x = x + rmsnorm(h, mix_post_g[l])
        h = swiglu_ffn(rmsnorm(x, ffn2_pre_g[l]), ffn2_w_gu[l], ffn2_w_down[l])
        x = x + 0.5 * rmsnorm(h, ffn2_post_g[l])
    return x
```

```python
import functools
import math

import jax
import jax.numpy as jnp
from jax import lax
from jax.experimental import pallas as pl
from jax.experimental.pallas import tpu as pltpu

F32 = jnp.float32
BF16 = jnp.bfloat16

EPS = 1e-6
BRANCH_W = 256
N_BRANCH = 4
GM_GROUPS = 4
GM_CHUNK = 128
POOL_WINDOWS = (2, 4, 8, 16)
ATT_HEADS = 4
ATT_HD = 64
IDX_HEADS = 4
IDX_HD = 32
TOPK_MAX = 256
REL_BUCKETS = 32
REL_MAX_DIST = 128
CONV_K = 31

VMEM_LIMIT_BYTES = 56 * 1024 * 1024
LANES = 128

FFN_CHUNK = 256
CONV_HALO = 32
POOL_HALO = 16
CONV_ROWS = 64
ATT_BLOCK = 256
BISECT_STEPS = 18
NEG = -1e30


def _rmsnorm(x, g):
    return x * lax.rsqrt(jnp.mean(x * x, axis=-1, keepdims=True) + EPS) * g


def _sigmoid(x):
    return 1.0 / (1.0 + jnp.exp(-x))


def _dot(a, b):
    return jnp.dot(a, b, preferred_element_type=F32)


def _ffn_body(x_ref, pre_ref, post_ref, wg_ref, wu_ref, wd_ref, o_ref, xn_ref, acc_ref):
    n_chunks = wg_ref.shape[0]
    x = x_ref[...]
    xn_ref[...] = _rmsnorm(x, pre_ref[...]).astype(BF16)
    acc_ref[...] = jnp.zeros_like(acc_ref)

    def chunk(c, carry):
        xn = xn_ref[...]
        a = _dot(xn, wg_ref[c])
        b = _dot(xn, wu_ref[c])
        hm = (a * _sigmoid(a) * b).astype(BF16)
        acc_ref[...] += _dot(hm, wd_ref[c])
        return carry

    lax.fori_loop(0, n_chunks, chunk, 0)
    o_ref[...] = x + 0.5 * _rmsnorm(acc_ref[...], post_ref[...])


def _ffn(x, pre_g, post_g, wg, wu, wd, tm):
    n, d = x.shape
    nc, _, fc = wg.shape
    full = lambda shape: pl.BlockSpec(shape, lambda i: (0,) * len(shape))
    return pl.pallas_call(
        _ffn_body,
        grid=(n // tm,),
        in_specs=[
            pl.BlockSpec((tm, d), lambda i: (i, 0)),
            full((1, d)), full((1, d)),
            full((nc, d, fc)), full((nc, d, fc)), full((nc, fc, d)),
        ],
        out_specs=pl.BlockSpec((tm, d), lambda i: (i, 0)),
        out_shape=jax.ShapeDtypeStruct((n, d), F32),
        scratch_shapes=[pltpu.VMEM((tm, d), BF16), pltpu.VMEM((tm, d), F32)],
        compiler_params=pltpu.CompilerParams(
            dimension_semantics=("arbitrary",), vmem_limit_bytes=VMEM_LIMIT_BYTES),
        name="ffn",
    )(x, pre_g, post_g, wg, wu, wd)


_UV0, _PZ0, _Q0, _V0, _CZ0, _QI0, _WI0, _W1_COLS = 0, 512, 768, 1024, 1280, 1792, 1920, 2048


def _front_body(blocks_per_seq,
                x_ref, pre_ref, w1_ref, w2t_ref, gmvg_ref, ws_ref, gmb_ref, poolw_ref,
                pscale_ref, dw_ref, cb_ref, lng_ref, lnb_ref,
                bg_ref, bp_ref, bc_ref, q_ref, v_ref, qi_ref, wi_ref, kt_ref, kit_ref,
                hbuf, pbuf):
    tm = x_ref.shape[0]
    w = BRANCH_W
    j = pl.program_id(0) % blocks_per_seq

    @pl.when(j == 0)
    def _():
        hbuf[0:CONV_HALO, :] = jnp.zeros((CONV_HALO, w), F32)
        pbuf[0:POOL_HALO, :] = jnp.zeros((POOL_HALO, w), F32)

    xn = _rmsnorm(x_ref[...], pre_ref[...]).astype(BF16)

    q_ref[...] = (_dot(xn, w1_ref[:, _Q0:_Q0 + w]) * (ATT_HD ** -0.5)).astype(BF16)
    v_ref[...] = _dot(xn, w1_ref[:, _V0:_V0 + w]).astype(BF16)
    qi_ref[...] = _dot(xn, w1_ref[:, _QI0:_QI0 + LANES]).astype(BF16)
    wi_ref[...] = _dot(xn, w1_ref[:, _WI0:_WI0 + LANES])
    kt = lax.dot_general(w2t_ref[...], xn, (((1,), (1,)), ((), ())),
                         preferred_element_type=F32).astype(BF16)
    for cc in range(tm // ATT_BLOCK):
        sl = slice(cc * ATT_BLOCK, (cc + 1) * ATT_BLOCK)
        kt_ref[0, cc] = kt[0:w, sl]
        kit_ref[0, cc] = kt[w:w + LANES, sl]

    uv = _dot(xn, w1_ref[:, _UV0:_UV0 + 2 * w])
    guv = uv * (0.5 * (1.0 + jnp.tanh(math.sqrt(2.0 / math.pi) * (uv + 0.044715 * (uv ** 3)))))
    u = guv[:, 0:w]
    vv = _rmsnorm(guv[:, w:2 * w], gmvg_ref[...])
    row = lax.broadcasted_iota(jnp.int32, (GM_CHUNK, GM_GROUPS * GM_CHUNK), 0)
    col = lax.broadcasted_iota(jnp.int32, (GM_CHUNK, GM_GROUPS * GM_CHUNK), 1)
    wsm = jnp.where((col % GM_CHUNK) <= row, ws_ref[...], 0.0).astype(BF16)
    lane_group = lax.broadcasted_iota(jnp.int32, (GM_CHUNK, w), 1) // (w // GM_GROUPS)
    for c in range(tm // GM_CHUNK):
        rs = slice(c * GM_CHUNK, (c + 1) * GM_CHUNK)
        vc = vv[rs, :]
        stacked = jnp.concatenate(
            [jnp.where(lane_group == g, vc, 0.0) for g in range(GM_GROUPS)], axis=0).astype(BF16)
        mixed = _dot(wsm, stacked) + gmb_ref[...]
        bg_ref[rs, :] = (u[rs, :] * mixed).astype(BF16)

    p = _dot(xn, w1_ref[:, _PZ0:_PZ0 + w])
    pbuf[POOL_HALO:POOL_HALO + tm, :] = p
    lane_win = lax.broadcasted_iota(jnp.int32, (tm, w), 1) // (w // len(POOL_WINDOWS))
    pos1 = (j * tm + lax.broadcasted_iota(jnp.int32, (tm, w), 0) + 1).astype(F32)
    run = p
    win = jnp.zeros((tm, w), F32)
    cnt = jnp.zeros((tm, w), F32)
    shift = 1
    for g, wlen in enumerate(POOL_WINDOWS):
        while shift < wlen:
            run = run + pbuf[POOL_HALO - shift:POOL_HALO - shift + tm, :]
            shift += 1
        win = jnp.where(lane_win == g, run, win)
        cnt = jnp.where(lane_win == g, jnp.minimum(pos1, float(wlen)), cnt)
    dpool = (win / cnt - p).astype(BF16)
    bp_ref[...] = (_dot(dpool, poolw_ref[...]) * pscale_ref[...]).astype(BF16)
    pbuf[0:POOL_HALO, :] = pbuf[tm:tm + POOL_HALO, :]

    cz = _dot(xn, w1_ref[:, _CZ0:_CZ0 + 2 * w])
    hbuf[CONV_HALO:CONV_HALO + tm, :] = cz[:, 0:w] * _sigmoid(cz[:, w:2 * w])
    for r in range(tm // CONV_ROWS):
        base = r * CONV_ROWS + CONV_HALO - (CONV_K - 1)
        acc = jnp.zeros((CONV_ROWS, w), F32)
        for t in range(CONV_K):
            acc = acc + hbuf[base + t:base + t + CONV_ROWS, :] * dw_ref[t:t + 1, :]
        hc = acc + cb_ref[...]
        mu = jnp.mean(hc, axis=-1, keepdims=True)
        xc = hc - mu
        yn = xc * lax.rsqrt(jnp.mean(xc * xc, axis=-1, keepdims=True) + EPS)
        yn = yn * lng_ref[...] + lnb_ref[...]
        bc_ref[r * CONV_ROWS:(r + 1) * CONV_ROWS, :] = (yn * _sigmoid(yn)).astype(BF16)
    hbuf[0:CONV_HALO, :] = hbuf[tm:tm + CONV_HALO, :]


def _front(x, seq, pre_g, w1, w2t, gmvg, ws_cat, gmb2d, poolw, pscale, dw, cb, lng, lnb, tm):
    n, d = x.shape
    batch = n // seq
    bps = seq // tm
    w = BRANCH_W
    nch = seq // ATT_BLOCK
    cpb = tm // ATT_BLOCK
    full = lambda a: pl.BlockSpec(a.shape, lambda i: (0,) * a.ndim)
    tok = lambda width: pl.BlockSpec((tm, width), lambda i: (i, 0))
    params = (pre_g, w1, w2t, gmvg, ws_cat, gmb2d, poolw, pscale, dw, cb, lng, lnb)
    out_shape = (
        jax.ShapeDtypeStruct((n, w), BF16),
        jax.ShapeDtypeStruct((n, w), BF16),
        jax.ShapeDtypeStruct((n, w), BF16),
        jax.ShapeDtypeStruct((n, w), BF16),
        jax.ShapeDtypeStruct((n, w), BF16),
        jax.ShapeDtypeStruct((n, LANES), BF16),
        jax.ShapeDtypeStruct((n, LANES), F32),
        jax.ShapeDtypeStruct((batch, nch, w, ATT_BLOCK), BF16),
        jax.ShapeDtypeStruct((batch, nch, LANES, ATT_BLOCK), BF16),
    )
    out_specs = (
        tok(w), tok(w), tok(w), tok(w), tok(w), tok(LANES), tok(LANES),
        pl.BlockSpec((1, cpb, w, ATT_BLOCK), lambda i: (i // bps, i % bps, 0, 0)),
        pl.BlockSpec((1, cpb, LANES, ATT_BLOCK), lambda i: (i // bps, i % bps, 0, 0)),
    )
    return pl.pallas_call(
        functools.partial(_front_body, bps),
        grid=(n // tm,),
        in_specs=[tok(d)] + [full(a) for a in params],
        out_specs=out_specs,
        out_shape=out_shape,
        scratch_shapes=[pltpu.VMEM((tm + CONV_HALO, w), F32), pltpu.VMEM((tm + POOL_HALO, w), F32)],
        compiler_params=pltpu.CompilerParams(
            dimension_semantics=("arbitrary",), vmem_limit_bytes=VMEM_LIMIT_BYTES),
        name="mixer_front",
    )(x, *params)


def _attn_body(topk,
               tab_ref, q_ref, qi_ref, wi_ref, kt_ref, kit_ref, v_ref, o_ref,
               s_ref, bd_ref, bp_ref, tri_ref, qm_ref, qim_ref, acc_ref):
    qb = ATT_BLOCK
    b = pl.program_id(0)
    i = pl.program_id(1)
    row = lax.broadcasted_iota(jnp.int32, (qb, qb), 0)
    col = lax.broadcasted_iota(jnp.int32, (qb, qb), 1)

    @pl.when((b == 0) & (i == 0))
    def _():
        tri_ref[...] = jnp.where(row <= col, 1.0, 0.0).astype(BF16)
        max_exact = REL_BUCKETS // 2
        for ref, off in ((bd_ref, 0), (bp_ref, qb)):
            n = jnp.maximum(row - col + off, 0)
            large = max_exact + (
                jnp.log(jnp.maximum(n, 1).astype(F32) / max_exact)
                / math.log(REL_MAX_DIST / max_exact) * (REL_BUCKETS - max_exact)).astype(jnp.int32)
            bucket = jnp.where(n < max_exact, n, jnp.minimum(large, REL_BUCKETS - 1))
            for h in range(ATT_HEADS):
                bias = jnp.zeros((qb, qb), F32)
                for k in range(REL_BUCKETS):
                    bias = jnp.where(bucket == k, tab_ref[k * ATT_HEADS + h], bias)
                ref[h] = bias

    q = q_ref[0]
    qi = qi_ref[0]
    q_head = lax.broadcasted_iota(jnp.int32, q.shape, 1) // ATT_HD
    qi_head = lax.broadcasted_iota(jnp.int32, qi.shape, 1) // IDX_HD
    for h in range(ATT_HEADS):
        qm_ref[h] = jnp.where(q_head == h, q, jnp.zeros_like(q))
    for h in range(IDX_HEADS):
        qim_ref[h] = jnp.where(qi_head == h, qi, jnp.zeros_like(qi))
    wv = wi_ref[0] * ((IDX_HEADS ** -0.5) * (IDX_HD ** -0.5))

    t_pos = i * qb + row
    n_chunks = i + 1

    def score_chunk(c, carry):
        rmax, rmin = carry
        kic = kit_ref[0, c]
        sc = jnp.zeros((qb, qb), F32)
        for h in range(IDX_HEADS):
            sc = sc + wv[:, h:h + 1] * jnp.maximum(_dot(qim_ref[h], kic), 0.0)
        valid = (c * qb + col) <= t_pos
        s_ref[c] = jnp.where(valid, sc, -jnp.inf)
        rmax = jnp.maximum(rmax, jnp.max(jnp.where(valid, sc, -jnp.inf), axis=1, keepdims=True))
        rmin = jnp.minimum(rmin, jnp.min(jnp.where(valid, sc, jnp.inf), axis=1, keepdims=True))
        return rmax, rmin

    rmax, rmin = lax.fori_loop(
        0, n_chunks, score_chunk,
        (jnp.full((qb, 1), -jnp.inf, F32), jnp.full((qb, 1), jnp.inf, F32)))

    n_valid = (i * qb + lax.broadcasted_iota(jnp.int32, (qb, 1), 0) + 1).astype(F32)
    kp = jnp.minimum(float(topk), n_valid)

    def count_ge(thr):
        def body(c, acc):
            ge = jnp.where(s_ref[c] >= thr, 1.0, 0.0)
            return acc + ge[:, 0:LANES] + ge[:, LANES:2 * LANES]
        acc = lax.fori_loop(0, n_chunks, body, jnp.zeros((qb, LANES), F32))
        return jnp.sum(acc, axis=1, keepdims=True)

    def bisect(_, carry):
        lo, hi, chi = carry
        mid = 0.5 * lo + 0.5 * jnp.minimum(hi, rmax)
        cnt = count_ge(mid)
        feas = cnt >= kp
        return (jnp.where(feas, mid, lo), jnp.where(feas, hi, mid), jnp.where(feas, chi, cnt))

    lo, hi, chi = lax.fori_loop(
        0, BISECT_STEPS, bisect,
        (rmin, jnp.full((qb, 1), jnp.inf, F32), jnp.zeros((qb, 1), F32)))

    def max_below(thr):
        def body(c, acc):
            x = s_ref[c]
            x = jnp.where(x < thr, x, -jnp.inf)
            return jnp.maximum(acc, jnp.maximum(x[:, 0:LANES], x[:, LANES:2 * LANES]))
        acc = lax.fori_loop(0, n_chunks, body, jnp.full((qb, LANES), -jnp.inf, F32))
        return jnp.max(acc, axis=1, keepdims=True)

    def finish_cond(state):
        return state[4] > 0.0

    def finish_body(state):
        hi, chi, tau, done, _ = state
        m = max_below(hi)
        cnt = count_ge(m)
        feas = cnt >= kp
        active = done < 0.5
        tau = jnp.where(active & feas, m, tau)
        hi = jnp.where(active & (~feas), m, hi)
        chi = jnp.where(active & (~feas), cnt, chi)
        done = jnp.where(active & feas, 1.0, done)
        return hi, chi, tau, done, jnp.max(1.0 - done)

    _, chi, tau, _, _ = lax.while_loop(
        finish_cond, finish_body,
        (hi, chi, lo, jnp.zeros((qb, 1), F32), jnp.float32(1.0)))
    need = kp - chi

    lane_head = lax.broadcasted_iota(jnp.int32, (qb, BRANCH_W), 1) // ATT_HD
    acc_ref[...] = jnp.zeros_like(acc_ref)

    def attend(c, bias, live, carry):
        ms, ls, runeq = carry
        x = s_ref[c]
        eq = jnp.where(x == tau, 1.0, 0.0)
        if live is not None:
            eq = jnp.where(live, eq, 0.0)
        prefix = _dot(eq.astype(BF16), tri_ref[...]) + runeq
        sel = jnp.where(x > tau, 1.0, jnp.where(prefix <= need, eq, 0.0))
        if live is not None:
            sel = jnp.where(live, sel, 0.0)
        selb = sel > 0.5
        runeq = runeq + jnp.sum(eq, axis=1, keepdims=True)
        kc = kt_ref[0, c]
        vc = v_ref[0, pl.ds(pl.multiple_of(c * qb, qb), qb), :]
        pv = jnp.zeros((qb, BRANCH_W), F32)
        alpha_full = jnp.ones((qb, BRANCH_W), F32)
        new_ms, new_ls = [], []
        for h in range(ATT_HEADS):
            lg = _dot(qm_ref[h], kc) + bias[h]
            lg = jnp.where(selb, lg, NEG)
            mn = jnp.maximum(ms[h], jnp.max(lg, axis=1, keepdims=True))
            pr = jnp.where(selb, jnp.exp(lg - mn), 0.0)
            alpha = jnp.exp(ms[h] - mn)
            new_ms.append(mn)
            new_ls.append(alpha * ls[h] + jnp.sum(pr, axis=1, keepdims=True))
            vh = jnp.where(lane_head == h, vc, jnp.zeros_like(vc))
            pv = pv + _dot(pr.astype(BF16), vh)
            alpha_full = jnp.where(lane_head == h, alpha, alpha_full)
        acc_ref[...] = acc_ref[...] * alpha_full + pv
        return tuple(new_ms), tuple(new_ls), runeq

    far_bias = [tab_ref[(REL_BUCKETS - 1) * ATT_HEADS + h] for h in range(ATT_HEADS)]
    carry = (tuple(jnp.full((qb, 1), NEG, F32) for _ in range(ATT_HEADS)),
             tuple(jnp.zeros((qb, 1), F32) for _ in range(ATT_HEADS)),
             jnp.zeros((qb, 1), F32))
    carry = lax.fori_loop(0, jnp.maximum(i - 1, 0),
                          lambda c, cr: attend(c, far_bias, None, cr), carry)
    carry = attend(jnp.maximum(i - 1, 0), [bp_ref[h] for h in range(ATT_HEADS)], i >= 1, carry)
    _, ls, _ = attend(i, [bd_ref[h] for h in range(ATT_HEADS)], None, carry)

    l_full = jnp.ones((qb, BRANCH_W), F32)
    for h in range(ATT_HEADS):
        l_full = jnp.where(lane_head == h, ls[h], l_full)
    o_ref[0] = (acc_ref[...] / l_full).astype(BF16)


def _attention(tab, q, qi, wi, kt, kit, v):
    batch, seq, w = q.shape
    qb = ATT_BLOCK
    nch = seq // qb
    topk = min(TOPK_MAX, seq // 4)
    qspec = lambda width: pl.BlockSpec((1, qb, width), lambda b, i: (b, i, 0))
    return pl.pallas_call(
        functools.partial(_attn_body, topk),
        grid=(batch, nch),
        in_specs=[
            pl.BlockSpec(memory_space=pltpu.SMEM),
            qspec(w), qspec(LANES), qspec(LANES),
            pl.BlockSpec((1, nch, w, qb), lambda b, i: (b, 0, 0, 0)),
            pl.BlockSpec((1, nch, LANES, qb), lambda b, i: (b, 0, 0, 0)),
            pl.BlockSpec((1, seq, w), lambda b, i: (b, 0, 0)),
        ],
        out_specs=qspec(w),
        out_shape=jax.ShapeDtypeStruct((batch, seq, w), BF16),
        scratch_shapes=[
            pltpu.VMEM((nch, qb, qb), F32),
            pltpu.VMEM((ATT_HEADS, qb, qb), F32),
            pltpu.VMEM((ATT_HEADS, qb, qb), F32),
            pltpu.VMEM((qb, qb), BF16),
            pltpu.VMEM((ATT_HEADS, qb, w), BF16),
            pltpu.VMEM((IDX_HEADS, qb, LANES), BF16),
            pltpu.VMEM((qb, w), F32),
        ],
        compiler_params=pltpu.CompilerParams(
            dimension_semantics=("arbitrary", "arbitrary"), vmem_limit_bytes=VMEM_LIMIT_BYTES),
        name="sparse_attention",
    )(tab, q, qi, wi, kt, kit, v)


def _back_body(x_ref, bg_ref, bp_ref, ba_ref, bc_ref, pre_ref, post_ref, wgate_ref, wbr_ref,
               wout_ref, o_ref):
    x = x_ref[...]
    xn = _rmsnorm(x, pre_ref[...]).astype(BF16)
    y = None
    for n, br in enumerate((bg_ref, bp_ref, ba_ref, bc_ref)):
        term = _sigmoid(_dot(xn, wgate_ref[n])) * _dot(br[...], wbr_ref[n])
        y = term if y is None else y + term
    h = _dot(y.astype(BF16), wout_ref[...])
    o_ref[...] = x + _rmsnorm(h, post_ref[...])


def _back(x, bg, bp, ba, bc, pre_g, post_g, wgate, wbr, wout, tm):
    n, d = x.shape
    w = BRANCH_W
    full = lambda a: pl.BlockSpec(a.shape, lambda i: (0,) * a.ndim)
    tok = lambda width: pl.BlockSpec((tm, width), lambda i: (i, 0))
    params = (pre_g, post_g, wgate, wbr, wout)
    return pl.pallas_call(
        _back_body,
        grid=(n // tm,),
        in_specs=[tok(d), tok(w), tok(w), tok(w), tok(w)] + [full(a) for a in params],
        out_specs=tok(d),
        out_shape=jax.ShapeDtypeStruct((n, d), F32),
        compiler_params=pltpu.CompilerParams(
            dimension_semantics=("arbitrary",), vmem_limit_bytes=VMEM_LIMIT_BYTES),
        name="mixer_back",
    )(x, bg, bp, ba, bc, *params)


def _split_ffn_weights(w_gu, w_down):
    d, f2 = w_gu.shape
    f = f2 // 2
    nc = f // FFN_CHUNK
    wg = w_gu[:, :f].reshape(d, nc, FFN_CHUNK).transpose(1, 0, 2).astype(BF16)
    wu = w_gu[:, f:].reshape(d, nc, FFN_CHUNK).transpose(1, 0, 2).astype(BF16)
    wd = w_down.reshape(nc, FFN_CHUNK, d).astype(BF16)
    return wg, wu, wd


def _pack_front_weights(w_in):
    d = w_in.shape[0]
    w = BRANCH_W
    o = 0
    uv = w_in[:, o:o + 2 * w]; o += 2 * w
    pz = w_in[:, o:o + w]; o += w
    qz = w_in[:, o:o + w]; o += w
    kz = w_in[:, o:o + w]; o += w
    vz = w_in[:, o:o + w]; o += w
    qi = w_in[:, o:o + IDX_HEADS * IDX_HD]; o += IDX_HEADS * IDX_HD
    ki = w_in[:, o:o + IDX_HD]; o += IDX_HD
    wi = w_in[:, o:o + IDX_HEADS]; o += IDX_HEADS
    cz = w_in[:, o:o + 2 * w]; o += 2 * w
    gz = w_in[:, o:]
    wi_pad = jnp.pad(wi, ((0, 0), (0, LANES - IDX_HEADS)))
    w1 = jnp.concatenate([uv, pz, qz, vz, cz, qi, wi_pad], axis=1).astype(BF16)
    w2t = jnp.concatenate([kz] + [ki] * (LANES // IDX_HD), axis=1).T.astype(BF16)
    wgate = gz.reshape(d, N_BRANCH, d).transpose(1, 0, 2).astype(BF16)
    return w1, w2t, wgate


def _block_diag(pw):
    g, c, _ = pw.shape
    out = jnp.zeros((g * c, g * c), pw.dtype)
    for k in range(g):
        out = out.at[k * c:(k + 1) * c, k * c:(k + 1) * c].set(pw[k])
    return out


def kernel(x, ffn1_pre_g, ffn1_post_g, ffn1_w_gu, ffn1_w_down, mix_pre_g, mix_post_g, w_in,
           gm_v_g, gm_ws, gm_b, pool_w, pool_scale, conv_dw, conv_b, conv_ln_g, conv_ln_b,
           w_branch, w_out, ffn2_pre_g, ffn2_post_g, ffn2_w_gu, ffn2_w_down, rel_bias):
    batch, seq, d = x.shape
    depth = w_in.shape[0]
    n = batch * seq
    tm = min(512, seq)
    w = BRANCH_W
    row = lambda a: a.reshape(1, -1)
    tab = rel_bias.reshape(-1)

    xf = x.reshape(n, d)
    for l in range(depth):
        xf = _ffn(xf, row(ffn1_pre_g[l]), row(ffn1_post_g[l]),
                  *_split_ffn_weights(ffn1_w_gu[l], ffn1_w_down[l]), tm)

        w1, w2t, wgate = _pack_front_weights(w_in[l])
        ws_cat = gm_ws[l].transpose(1, 0, 2).reshape(GM_CHUNK, GM_GROUPS * GM_CHUNK)
        gmb2d = jnp.repeat(gm_b[l].T, w // GM_GROUPS, axis=1)
        dw = jnp.pad(conv_dw[l], ((0, 1), (0, 0)))
        bg, bp, bc, q, v, qi, wi, kt, kit = _front(
            xf, seq, row(mix_pre_g[l]), w1, w2t, row(gm_v_g[l]), ws_cat, gmb2d,
            _block_diag(pool_w[l]).astype(BF16), row(pool_scale[l]), dw, row(conv_b[l]),
            row(conv_ln_g[l]), row(conv_ln_b[l]), tm)
        ba = _attention(tab, q.reshape(batch, seq, w), qi.reshape(batch, seq, LANES),
                        wi.reshape(batch, seq, LANES), kt, kit, v.reshape(batch, seq, w))
        xf = _back(xf, bg, bp, ba.reshape(n, w), bc, row(mix_pre_g[l]), row(mix_post_g[l]),
                   wgate, w_branch[l].astype(BF16), w_out[l].astype(BF16), tm)

        xf = _ffn(xf, row(ffn2_pre_g[l]), row(ffn2_post_g[l]),
                  *_split_ffn_weights(ffn2_w_gu[l], ffn2_w_down[l]), tm)
    return xf.reshape(batch, seq, d)
```

```python
import functools
import math

import jax
import jax.numpy as jnp
from jax import lax
from jax.experimental import pallas as pl
from jax.experimental.pallas import tpu as pltpu

F32 = jnp.float32
BF16 = jnp.bfloat16

EPS = 1e-6
BRANCH_W = 256
N_BRANCH = 4
GM_GROUPS = 4
GM_CHUNK = 128
POOL_WINDOWS = (2, 4, 8, 16)
ATT_HEADS = 4
ATT_HD = 64
IDX_HEADS = 4
IDX_HD = 32
TOPK_MAX = 256
REL_BUCKETS = 32
REL_MAX_DIST = 128
CONV_K = 31

VMEM_LIMIT_BYTES = 56 * 1024 * 1024
LANES = 128
SUBLANES = 8

FFN_CHUNK = 256
CONV_HALO = 32
POOL_HALO = 16
CONV_ROWS = 64
ATT_BLOCK = 256
BISECT_STEPS = 18
LOG2E = math.log2(math.e)
Q_SCALE = ATT_HD ** -0.5 * LOG2E
NEG = -1e30
M_INIT = -1e29


def _rmsnorm(x, g):
    return x * lax.rsqrt(jnp.mean(x * x, axis=-1, keepdims=True) + EPS) * g


def _sigmoid(x):
    return 1.0 / (1.0 + jnp.exp(-x))


def _dot(a, b):
    return jnp.dot(a, b, preferred_element_type=F32)


def _fold8(x, op):
    parts = [x[j * SUBLANES:(j + 1) * SUBLANES] for j in range(x.shape[0] // SUBLANES)]
    while len(parts) > 1:
        nxt = [op(parts[j], parts[j + 1]) for j in range(0, len(parts) - 1, 2)]
        if len(parts) % 2:
            nxt.append(parts[-1])
        parts = nxt
    return parts[0]


def _ffn_body(x_ref, pre_ref, post_ref, wg_ref, wu_ref, wd_ref, o_ref, xn_ref, acc_ref):
    n_chunks = wg_ref.shape[0]
    x = x_ref[...]
    xn_ref[...] = _rmsnorm(x, pre_ref[...]).astype(BF16)
    acc_ref[...] = jnp.zeros_like(acc_ref)

    def chunk(c, carry):
        xn = xn_ref[...]
        a = _dot(xn, wg_ref[c])
        b = _dot(xn, wu_ref[c])
        hm = (a * _sigmoid(a) * b).astype(BF16)
        acc_ref[...] += _dot(hm, wd_ref[c])
        return carry

    lax.fori_loop(0, n_chunks, chunk, 0)
    o_ref[...] = x + 0.5 * _rmsnorm(acc_ref[...], post_ref[...])


def _ffn(x, pre_g, post_g, wg, wu, wd, tm):
    n, d = x.shape
    nc, _, fc = wg.shape
    full = lambda shape: pl.BlockSpec(shape, lambda i: (0,) * len(shape))
    return pl.pallas_call(
        _ffn_body,
        grid=(n // tm,),
        in_specs=[
            pl.BlockSpec((tm, d), lambda i: (i, 0)),
            full((1, d)), full((1, d)),
            full((nc, d, fc)), full((nc, d, fc)), full((nc, fc, d)),
        ],
        out_specs=pl.BlockSpec((tm, d), lambda i: (i, 0)),
        out_shape=jax.ShapeDtypeStruct((n, d), F32),
        scratch_shapes=[pltpu.VMEM((tm, d), BF16), pltpu.VMEM((tm, d), F32)],
        compiler_params=pltpu.CompilerParams(
            dimension_semantics=("arbitrary",), vmem_limit_bytes=VMEM_LIMIT_BYTES),
        name="ffn",
    )(x, pre_g, post_g, wg, wu, wd)


_UV0, _PZ0, _K0, _CZ0, _KI0, _W1_COLS = 0, 512, 768, 1024, 1536, 1664
_QT0, _VT0, _QIT0, _WIT0, _W2_ROWS = 0, 256, 512, 640, 656


def _front_body(blocks_per_seq,
                x_ref, pre_ref, w1_ref, w2t_ref, gmvg_ref, ws_ref, gmb_ref, poolw_ref,
                pscale_ref, dw_ref, cb_ref, lng_ref, lnb_ref,
                bg_ref, bp_ref, bc_ref, k_ref, ki_ref, qt_ref, vt_ref, qit_ref, wit_ref,
                hbuf, pbuf):
    tm = x_ref.shape[0]
    w = BRANCH_W
    j = pl.program_id(0) % blocks_per_seq

    @pl.when(j == 0)
    def _():
        hbuf[0:CONV_HALO, :] = jnp.zeros((CONV_HALO, w), F32)
        pbuf[0:POOL_HALO, :] = jnp.zeros((POOL_HALO, w), F32)

    xn = _rmsnorm(x_ref[...], pre_ref[...]).astype(BF16)

    k_ref[...] = _dot(xn, w1_ref[:, _K0:_K0 + w]).astype(BF16)
    ki_ref[...] = _dot(xn, w1_ref[:, _KI0:_KI0 + LANES]).astype(BF16)
    zt = lax.dot_general(w2t_ref[...], xn, (((1,), (1,)), ((), ())),
                         preferred_element_type=F32)
    for cc in range(tm // ATT_BLOCK):
        sl = slice(cc * ATT_BLOCK, (cc + 1) * ATT_BLOCK)
        qt_ref[0, cc] = (zt[_QT0:_QT0 + w, sl] * Q_SCALE).astype(BF16)
        vt_ref[0, cc] = zt[_VT0:_VT0 + w, sl].astype(BF16)
        qit_ref[0, cc] = zt[_QIT0:_QIT0 + LANES, sl].astype(BF16)
        wit_ref[0, cc] = zt[_WIT0:_WIT0 + SUBLANES, sl]

    uv = _dot(xn, w1_ref[:, _UV0:_UV0 + 2 * w])
    guv = uv * (0.5 * (1.0 + jnp.tanh(math.sqrt(2.0 / math.pi) * (uv + 0.044715 * (uv ** 3)))))
    u = guv[:, 0:w]
    vv = _rmsnorm(guv[:, w:2 * w], gmvg_ref[...])
    row = lax.broadcasted_iota(jnp.int32, (GM_CHUNK, GM_GROUPS * GM_CHUNK), 0)
    col = lax.broadcasted_iota(jnp.int32, (GM_CHUNK, GM_GROUPS * GM_CHUNK), 1)
    wsm = jnp.where((col % GM_CHUNK) <= row, ws_ref[...], 0.0).astype(BF16)
    lane_group = lax.broadcasted_iota(jnp.int32, (GM_CHUNK, w), 1) // (w // GM_GROUPS)
    for c in range(tm // GM_CHUNK):
        rs = slice(c * GM_CHUNK, (c + 1) * GM_CHUNK)
        vc = vv[rs, :]
        stacked = jnp.concatenate(
            [jnp.where(lane_group == g, vc, 0.0) for g in range(GM_GROUPS)], axis=0).astype(BF16)
        mixed = _dot(wsm, stacked) + gmb_ref[...]
        bg_ref[rs, :] = (u[rs, :] * mixed).astype(BF16)

    p = _dot(xn, w1_ref[:, _PZ0:_PZ0 + w])
    pbuf[POOL_HALO:POOL_HALO + tm, :] = p
    lane_win = lax.broadcasted_iota(jnp.int32, (tm, w), 1) // (w // len(POOL_WINDOWS))
    pos1 = (j * tm + lax.broadcasted_iota(jnp.int32, (tm, w), 0) + 1).astype(F32)
    run = p
    win = jnp.zeros((tm, w), F32)
    cnt = jnp.zeros((tm, w), F32)
    shift = 1
    for g, wlen in enumerate(POOL_WINDOWS):
        while shift < wlen:
            run = run + pbuf[POOL_HALO - shift:POOL_HALO - shift + tm, :]
            shift += 1
        win = jnp.where(lane_win == g, run, win)
        cnt = jnp.where(lane_win == g, jnp.minimum(pos1, float(wlen)), cnt)
    dpool = (win / cnt - p).astype(BF16)
    bp_ref[...] = (_dot(dpool, poolw_ref[...]) * pscale_ref[...]).astype(BF16)
    pbuf[0:POOL_HALO, :] = pbuf[tm:tm + POOL_HALO, :]

    cz = _dot(xn, w1_ref[:, _CZ0:_CZ0 + 2 * w])
    hbuf[CONV_HALO:CONV_HALO + tm, :] = cz[:, 0:w] * _sigmoid(cz[:, w:2 * w])
    for r in range(tm // CONV_ROWS):
        base = r * CONV_ROWS + CONV_HALO - (CONV_K - 1)
        acc = jnp.zeros((CONV_ROWS, w), F32)
        for t in range(CONV_K):
            acc = acc + hbuf[base + t:base + t + CONV_ROWS, :] * dw_ref[t:t + 1, :]
        hc = acc + cb_ref[...]
        mu = jnp.mean(hc, axis=-1, keepdims=True)
        xc = hc - mu
        yn = xc * lax.rsqrt(jnp.mean(xc * xc, axis=-1, keepdims=True) + EPS)
        yn = yn * lng_ref[...] + lnb_ref[...]
        bc_ref[r * CONV_ROWS:(r + 1) * CONV_ROWS, :] = (yn * _sigmoid(yn)).astype(BF16)
    hbuf[0:CONV_HALO, :] = hbuf[tm:tm + CONV_HALO, :]


def _front(x, seq, pre_g, w1, w2t, gmvg, ws_cat, gmb2d, poolw, pscale, dw, cb, lng, lnb, tm):
    n, d = x.shape
    batch = n // seq
    bps = seq // tm
    w = BRANCH_W
    nch = seq // ATT_BLOCK
    cpb = tm // ATT_BLOCK
    full = lambda a: pl.BlockSpec(a.shape, lambda i: (0,) * a.ndim)
    tok = lambda width: pl.BlockSpec((tm, width), lambda i: (i, 0))
    chunked = lambda rows: pl.BlockSpec((1, cpb, rows, ATT_BLOCK),
                                        lambda i: (i // bps, i % bps, 0, 0))
    params = (pre_g, w1, w2t, gmvg, ws_cat, gmb2d, poolw, pscale, dw, cb, lng, lnb)
    out_shape = (
        jax.ShapeDtypeStruct((n, w), BF16),
        jax.ShapeDtypeStruct((n, w), BF16),
        jax.ShapeDtypeStruct((n, w), BF16),
        jax.ShapeDtypeStruct((n, w), BF16),
        jax.ShapeDtypeStruct((n, LANES), BF16),
        jax.ShapeDtypeStruct((batch, nch, w, ATT_BLOCK), BF16),
        jax.ShapeDtypeStruct((batch, nch, w, ATT_BLOCK), BF16),
        jax.ShapeDtypeStruct((batch, nch, LANES, ATT_BLOCK), BF16),
        jax.ShapeDtypeStruct((batch, nch, SUBLANES, ATT_BLOCK), F32),
    )
    out_specs = (tok(w), tok(w), tok(w), tok(w), tok(LANES),
                 chunked(w), chunked(w), chunked(LANES), chunked(SUBLANES))
    return pl.pallas_call(
        functools.partial(_front_body, bps),
        grid=(n // tm,),
        in_specs=[tok(d)] + [full(a) for a in params],
        out_specs=out_specs,
        out_shape=out_shape,
        scratch_shapes=[pltpu.VMEM((tm + CONV_HALO, w), F32), pltpu.VMEM((tm + POOL_HALO, w), F32)],
        compiler_params=pltpu.CompilerParams(
            dimension_semantics=("arbitrary",), vmem_limit_bytes=VMEM_LIMIT_BYTES),
        name="mixer_front",
    )(x, *params)


def _attn_body(topk,
               tab_ref, qt_ref, qit_ref, wit_ref, k_ref, ki_ref, vt_ref, o_ref,
               s_ref, bd_ref, bp_ref, tri_ref, qm_ref, qim_ref, acc_ref, lg0_ref, lg1_ref):
    qb = ATT_BLOCK
    nch = vt_ref.shape[1]
    b = pl.program_id(0)
    i = pl.program_id(1)
    key = lax.broadcasted_iota(jnp.int32, (qb, qb), 0)
    qry = lax.broadcasted_iota(jnp.int32, (qb, qb), 1)

    @pl.when((b == 0) & (i == 0))
    def _():
        tri_ref[...] = jnp.where(qry <= key, 1.0, 0.0).astype(BF16)
        max_exact = REL_BUCKETS // 2
        for ref, off in ((bd_ref, 0), (bp_ref, qb)):
            n = jnp.maximum(qry - key + off, 0)
            large = max_exact + (
                jnp.log(jnp.maximum(n, 1).astype(F32) / max_exact)
                / math.log(REL_MAX_DIST / max_exact) * (REL_BUCKETS - max_exact)).astype(jnp.int32)
            bucket = jnp.where(n < max_exact, n, jnp.minimum(large, REL_BUCKETS - 1))
            for h in range(ATT_HEADS):
                bias = jnp.zeros((qb, qb), F32)
                for k in range(REL_BUCKETS):
                    bias = jnp.where(bucket == k, tab_ref[k * ATT_HEADS + h], bias)
                ref[h] = (bias - tab_ref[(REL_BUCKETS - 1) * ATT_HEADS + h]) * LOG2E

    qt = qt_ref[0, 0]
    qit = qit_ref[0, 0]
    q_head = lax.broadcasted_iota(jnp.int32, qt.shape, 0) // ATT_HD
    qi_head = lax.broadcasted_iota(jnp.int32, qit.shape, 0) // IDX_HD
    for h in range(ATT_HEADS):
        qm_ref[h] = jnp.where(q_head == h, qt, jnp.zeros_like(qt))
    for h in range(IDX_HEADS):
        qim_ref[h] = jnp.where(qi_head == h, qit, jnp.zeros_like(qit))
    wv = wit_ref[0, 0] * ((IDX_HEADS ** -0.5) * (IDX_HD ** -0.5))

    t_pos = i * qb + qry
    n_pairs = (i + 2) // 2

    def keys_of(c):
        return pl.ds(pl.multiple_of(c * qb, qb), qb)

    def score_chunk(c):
        kic = ki_ref[0, keys_of(jnp.minimum(c, nch - 1)), :]
        sc = jnp.zeros((qb, qb), F32)
        for h in range(IDX_HEADS):
            sc = sc + wv[h:h + 1, :] * jnp.maximum(_dot(kic, qim_ref[h]), 0.0)
        valid = (c * qb + key) <= t_pos
        s_ref[c] = jnp.where(valid, sc, -jnp.inf)
        return (_fold8(jnp.where(valid, sc, -jnp.inf), jnp.maximum),
                _fold8(jnp.where(valid, sc, jnp.inf), jnp.minimum))

    def score_pair(p, carry):
        rmax8, rmin8 = carry
        hi0, lo0 = score_chunk(2 * p)
        hi1, lo1 = score_chunk(2 * p + 1)
        return (jnp.maximum(rmax8, jnp.maximum(hi0, hi1)), jnp.minimum(rmin8, jnp.minimum(lo0, lo1)))

    rmax8, rmin8 = lax.fori_loop(
        0, n_pairs, score_pair,
        (jnp.full((SUBLANES, qb), -jnp.inf, F32), jnp.full((SUBLANES, qb), jnp.inf, F32)))
    rmax = jnp.max(rmax8, axis=0, keepdims=True)
    rmin = jnp.min(rmin8, axis=0, keepdims=True)

    n_valid = (i * qb + lax.broadcasted_iota(jnp.int32, (1, qb), 1) + 1).astype(F32)
    kp = jnp.minimum(float(topk), n_valid)

    def count_ge(thr):
        def body(p, acc):
            g0 = _fold8(jnp.where(s_ref[2 * p] >= thr, 1.0, 0.0), jnp.add)
            g1 = _fold8(jnp.where(s_ref[2 * p + 1] >= thr, 1.0, 0.0), jnp.add)
            return acc + (g0 + g1)
        acc = lax.fori_loop(0, n_pairs, body, jnp.zeros((SUBLANES, qb), F32))
        return jnp.sum(acc, axis=0, keepdims=True)

    def bisect(_, carry):
        lo, hi, chi = carry
        mid = 0.5 * lo + 0.5 * jnp.minimum(hi, rmax)
        cnt = count_ge(mid)
        feas = cnt >= kp
        return (jnp.where(feas, mid, lo), jnp.where(feas, hi, mid), jnp.where(feas, chi, cnt))

    lo, hi, chi = lax.fori_loop(
        0, BISECT_STEPS, bisect,
        (rmin, jnp.full((1, qb), jnp.inf, F32), jnp.zeros((1, qb), F32)))

    def max_below(thr):
        def body(p, acc):
            x0 = s_ref[2 * p]
            x1 = s_ref[2 * p + 1]
            m0 = _fold8(jnp.where(x0 < thr, x0, -jnp.inf), jnp.maximum)
            m1 = _fold8(jnp.where(x1 < thr, x1, -jnp.inf), jnp.maximum)
            return jnp.maximum(acc, jnp.maximum(m0, m1))
        acc = lax.fori_loop(0, n_pairs, body, jnp.full((SUBLANES, qb), -jnp.inf, F32))
        return jnp.max(acc, axis=0, keepdims=True)

    def finish_cond(state):
        return state[4] > 0.0

    def finish_body(state):
        hi, chi, tau, done, _ = state
        m = max_below(hi)
        cnt = count_ge(m)
        feas = cnt >= kp
        active = done < 0.5
        tau = jnp.where(active & feas, m, tau)
        hi = jnp.where(active & (~feas), m, hi)
        chi = jnp.where(active & (~feas), cnt, chi)
        done = jnp.where(active & feas, 1.0, done)
        return hi, chi, tau, done, jnp.max(1.0 - done)

    hi, chi, tau, _, _ = lax.while_loop(
        finish_cond, finish_body,
        (hi, chi, lo, jnp.zeros((1, qb), F32), jnp.float32(1.0)))
    need = kp - chi

    acc_ref[...] = jnp.zeros_like(acc_ref)
    lg1_ref[...] = jnp.full(lg1_ref.shape, NEG, F32)
    heads = range(ATT_HEADS)

    def stage_a(c, bias, live, buf, need_left):
        x = s_ref[c]
        eq = jnp.where(x == tau, 1.0, 0.0)
        if live is not None:
            eq = jnp.where(live, eq, 0.0)
        prefix = _dot(tri_ref[...], eq.astype(BF16))
        maskadd = jnp.where(x >= jnp.where(prefix <= need_left, tau, hi), 0.0, NEG)
        if live is not None:
            maskadd = jnp.where(live, maskadd, NEG)
        kc = k_ref[0, keys_of(c), :]
        mxs = []
        for h in heads:
            lg = _dot(kc, qm_ref[h]) + maskadd
            if bias is not None:
                lg = lg + bias[h]
            buf[h] = lg
            mxs.append(jnp.max(_fold8(lg, jnp.maximum), axis=0, keepdims=True))
        return tuple(mxs), need_left - prefix[qb - 1:qb, :]

    ones_rows = jnp.ones((2 * SUBLANES, qb), BF16)

    def stage_b(c, buf, mxs, ms, ls):
        vtc = vt_ref[0, c]
        new_ms = [jnp.maximum(ms[h], mxs[h]) for h in heads]
        alphas = [jnp.exp2(ms[h] - new_ms[h]) for h in heads]
        pvs = [_dot(jnp.concatenate([vtc[h * ATT_HD:(h + 1) * ATT_HD, :], ones_rows], axis=0),
                    jnp.exp2(buf[h] - new_ms[h]).astype(BF16)) for h in heads]
        new_ls = [alphas[h] * ls[h] + pvs[h][ATT_HD:ATT_HD + 1, :] for h in heads]
        pv = jnp.concatenate([pvs[h][0:ATT_HD, :] for h in heads], axis=0)
        alpha_rows = jnp.concatenate(
            [jnp.broadcast_to(alphas[h], (ATT_HD, qb)) for h in heads], axis=0)
        acc_ref[...] = acc_ref[...] * alpha_rows + pv
        return tuple(new_ms), tuple(new_ls)

    def far_pair(p, carry):
        ms, ls, need_left, pending = carry
        mx0, need_left = stage_a(2 * p, None, None, lg0_ref, need_left)
        ms, ls = stage_b(jnp.maximum(2 * p - 1, 0), lg1_ref, pending, ms, ls)
        mx1, need_left = stage_a(2 * p + 1, None, None, lg1_ref, need_left)
        ms, ls = stage_b(2 * p, lg0_ref, mx0, ms, ls)
        return ms, ls, need_left, mx1

    far_pairs = jnp.maximum(i - 1, 0) // 2
    m_init = tuple(jnp.full((1, qb), M_INIT, F32) for _ in heads)
    ms, ls, need_left, pending = lax.fori_loop(
        0, far_pairs, far_pair,
        (m_init, tuple(jnp.zeros((1, qb), F32) for _ in heads), need, m_init))
    c_far = jnp.maximum(i - 2, 0)
    c_prev = jnp.maximum(i - 1, 0)
    mx_far, need_left = stage_a(c_far, None, (i >= 2) & ((i - 1) % 2 == 1), lg0_ref, need_left)
    ms, ls = stage_b(jnp.maximum(2 * far_pairs - 1, 0), lg1_ref, pending, ms, ls)
    mx_prev, need_left = stage_a(c_prev, bp_ref, i >= 1, lg1_ref, need_left)
    ms, ls = stage_b(c_far, lg0_ref, mx_far, ms, ls)
    mx_diag, _ = stage_a(i, bd_ref, None, lg0_ref, need_left)
    ms, ls = stage_b(c_prev, lg1_ref, mx_prev, ms, ls)
    _, ls = stage_b(i, lg0_ref, mx_diag, ms, ls)

    for h in range(ATT_HEADS):
        rows = slice(h * ATT_HD, (h + 1) * ATT_HD)
        acc_ref[rows, :] = acc_ref[rows, :] / ls[h]
    o_ref[0] = acc_ref[...].T.astype(BF16)


def _attention(tab, qt, qit, wit, k, ki, vt):
    batch, nch, w, qb = qt.shape
    seq = nch * qb
    topk = min(TOPK_MAX, seq // 4)
    per_block = lambda rows: pl.BlockSpec((1, 1, rows, qb), lambda b, i: (b, i, 0, 0))
    return pl.pallas_call(
        functools.partial(_attn_body, topk),
        grid=(batch, nch),
        in_specs=[
            pl.BlockSpec(memory_space=pltpu.SMEM),
            per_block(w), per_block(LANES), per_block(SUBLANES),
            pl.BlockSpec((1, seq, w), lambda b, i: (b, 0, 0)),
            pl.BlockSpec((1, seq, LANES), lambda b, i: (b, 0, 0)),
            pl.BlockSpec((1, nch, w, qb), lambda b, i: (b, 0, 0, 0)),
        ],
        out_specs=pl.BlockSpec((1, qb, w), lambda b, i: (b, i, 0)),
        out_shape=jax.ShapeDtypeStruct((batch, seq, w), BF16),
        scratch_shapes=[
            pltpu.VMEM((nch + 1, qb, qb), F32),
            pltpu.VMEM((ATT_HEADS, qb, qb), F32),
            pltpu.VMEM((ATT_HEADS, qb, qb), F32),
            pltpu.VMEM((qb, qb), BF16),
            pltpu.VMEM((ATT_HEADS, w, qb), BF16),
            pltpu.VMEM((IDX_HEADS, LANES, qb), BF16),
            pltpu.VMEM((w, qb), F32),
            pltpu.VMEM((ATT_HEADS, qb, qb), F32),
            pltpu.VMEM((ATT_HEADS, qb, qb), F32),
        ],
        compiler_params=pltpu.CompilerParams(
            dimension_semantics=("arbitrary", "arbitrary"), vmem_limit_bytes=VMEM_LIMIT_BYTES),
        name="sparse_attention",
    )(tab, qt, qit, wit, k, ki, vt)


def _back_body(x_ref, bg_ref, bp_ref, ba_ref, bc_ref, pre_ref, post_ref, wgate_ref, wbr_ref,
               wout_ref, o_ref):
    x = x_ref[...]
    xn = _rmsnorm(x, pre_ref[...]).astype(BF16)
    y = None
    for n, br in enumerate((bg_ref, bp_ref, ba_ref, bc_ref)):
        term = _sigmoid(_dot(xn, wgate_ref[n])) * _dot(br[...], wbr_ref[n])
        y = term if y is None else y + term
    h = _dot(y.astype(BF16), wout_ref[...])
    o_ref[...] = x + _rmsnorm(h, post_ref[...])


def _back(x, bg, bp, ba, bc, pre_g, post_g, wgate, wbr, wout, tm):
    n, d = x.shape
    w = BRANCH_W
    full = lambda a: pl.BlockSpec(a.shape, lambda i: (0,) * a.ndim)
    tok = lambda width: pl.BlockSpec((tm, width), lambda i: (i, 0))
    params = (pre_g, post_g, wgate, wbr, wout)
    return pl.pallas_call(
        _back_body,
        grid=(n // tm,),
        in_specs=[tok(d), tok(w), tok(w), tok(w), tok(w)] + [full(a) for a in params],
        out_specs=tok(d),
        out_shape=jax.ShapeDtypeStruct((n, d), F32),
        compiler_params=pltpu.CompilerParams(
            dimension_semantics=("arbitrary",), vmem_limit_bytes=VMEM_LIMIT_BYTES),
        name="mixer_back",
    )(x, bg, bp, ba, bc, *params)


def _split_ffn_weights(w_gu, w_down):
    d, f2 = w_gu.shape
    f = f2 // 2
    nc = f // FFN_CHUNK
    wg = w_gu[:, :f].reshape(d, nc, FFN_CHUNK).transpose(1, 0, 2).astype(BF16)
    wu = w_gu[:, f:].reshape(d, nc, FFN_CHUNK).transpose(1, 0, 2).astype(BF16)
    wd = w_down.reshape(nc, FFN_CHUNK, d).astype(BF16)
    return wg, wu, wd


def _pack_front_weights(w_in):
    d = w_in.shape[0]
    w = BRANCH_W
    o = 0
    uv = w_in[:, o:o + 2 * w]; o += 2 * w
    pz = w_in[:, o:o + w]; o += w
    qz = w_in[:, o:o + w]; o += w
    kz = w_in[:, o:o + w]; o += w
    vz = w_in[:, o:o + w]; o += w
    qi = w_in[:, o:o + IDX_HEADS * IDX_HD]; o += IDX_HEADS * IDX_HD
    ki = w_in[:, o:o + IDX_HD]; o += IDX_HD
    wi = w_in[:, o:o + IDX_HEADS]; o += IDX_HEADS
    cz = w_in[:, o:o + 2 * w]; o += 2 * w
    gz = w_in[:, o:]
    w1 = jnp.concatenate([uv, pz, kz, cz] + [ki] * (LANES // IDX_HD), axis=1).astype(BF16)
    wi_pad = jnp.pad(wi, ((0, 0), (0, _W2_ROWS - _WIT0 - IDX_HEADS)))
    w2t = jnp.concatenate([qz, vz, qi, wi_pad], axis=1).T.astype(BF16)
    wgate = gz.reshape(d, N_BRANCH, d).transpose(1, 0, 2).astype(BF16)
    return w1, w2t, wgate


def _block_diag(pw):
    g, c, _ = pw.shape
    out = jnp.zeros((g * c, g * c), pw.dtype)
    for k in range(g):
        out = out.at[k * c:(k + 1) * c, k * c:(k + 1) * c].set(pw[k])
    return out


def kernel(x, ffn1_pre_g, ffn1_post_g, ffn1_w_gu, ffn1_w_down, mix_pre_g, mix_post_g, w_in,
           gm_v_g, gm_ws, gm_b, pool_w, pool_scale, conv_dw, conv_b, conv_ln_g, conv_ln_b,
           w_branch, w_out, ffn2_pre_g, ffn2_post_g, ffn2_w_gu, ffn2_w_down, rel_bias):
    batch, seq, d = x.shape
    depth = w_in.shape[0]
    n = batch * seq
    tm = min(512, seq)
    w = BRANCH_W
    row = lambda a: a.reshape(1, -1)
    tab = rel_bias.reshape(-1)

    xf = x.reshape(n, d)
    for l in range(depth):
        xf = _ffn(xf, row(ffn1_pre_g[l]), row(ffn1_post_g[l]),
                  *_split_ffn_weights(ffn1_w_gu[l], ffn1_w_down[l]), tm)

        w1, w2t, wgate = _pack_front_weights(w_in[l])
        ws_cat = gm_ws[l].transpose(1, 0, 2).reshape(GM_CHUNK, GM_GROUPS * GM_CHUNK)
        gmb2d = jnp.repeat(gm_b[l].T, w // GM_GROUPS, axis=1)
        dw = jnp.pad(conv_dw[l], ((0, 1), (0, 0)))
        bg, bp, bc, k, ki, qt, vt, qit, wit = _front(
            xf, seq, row(mix_pre_g[l]), w1, w2t, row(gm_v_g[l]), ws_cat, gmb2d,
            _block_diag(pool_w[l]).astype(BF16), row(pool_scale[l]), dw, row(conv_b[l]),
            row(conv_ln_g[l]), row(conv_ln_b[l]), tm)
        ba = _attention(tab, qt, qit, wit, k.reshape(batch, seq, w),
                        ki.reshape(batch, seq, LANES), vt)
        xf = _back(xf, bg, bp, ba.reshape(n, w), bc, row(mix_pre_g[l]), row(mix_post_g[l]),
                   wgate, w_branch[l].astype(BF16), w_out[l].astype(BF16), tm)

        xf = _ffn(xf, row(ffn2_pre_g[l]), row(ffn2_post_g[l]),
                  *_split_ffn_weights(ffn2_w_gu[l], ffn2_w_down[l]), tm)
    return xf.reshape(batch, seq, d)
```

```python
import functools
import math

import jax
import jax.numpy as jnp
from jax import lax
from jax.experimental import pallas as pl
from jax.experimental.pallas import tpu as pltpu

F32 = jnp.float32
BF16 = jnp.bfloat16

EPS = 1e-6
BRANCH_W = 256
N_BRANCH = 4
GM_GROUPS = 4
GM_CHUNK = 128
POOL_WINDOWS = (2, 4, 8, 16)
ATT_HEADS = 4
ATT_HD = 64
IDX_HEADS = 4
IDX_HD = 32
TOPK_MAX = 256
REL_BUCKETS = 32
REL_MAX_DIST = 128
CONV_K = 31

VMEM_LIMIT_BYTES = 56 * 1024 * 1024
LANES = 128
SUBLANES = 8

FFN_CHUNK = 256
CONV_HALO = 32
POOL_HALO = 16
CONV_ROWS = 64
ATT_BLOCK = 256
COARSE_STEPS = 10
BISECT_STEPS = 8
BF16_ROWS = 16
BF16_INTERVAL = 2.0 ** -6
TINY = 1e-30
LOG2E = math.log2(math.e)
Q_SCALE = ATT_HD ** -0.5 * LOG2E
NEG = -1e30
M_INIT = -1e29


def _rmsnorm(x, g):
    return x * lax.rsqrt(jnp.mean(x * x, axis=-1, keepdims=True) + EPS) * g


def _sigmoid(x):
    return 1.0 / (1.0 + jnp.exp(-x))


def _dot(a, b):
    return jnp.dot(a, b, preferred_element_type=F32)


def _fold8(x, op):
    return _fold(x, op, SUBLANES)


def _fold(x, op, rows):
    parts = [x[j * rows:(j + 1) * rows] for j in range(x.shape[0] // rows)]
    while len(parts) > 1:
        nxt = [op(parts[j], parts[j + 1]) for j in range(0, len(parts) - 1, 2)]
        if len(parts) % 2:
            nxt.append(parts[-1])
        parts = nxt
    return parts[0]


def _ffn_body(x_ref, pre_ref, post_ref, wg_ref, wu_ref, wd_ref, o_ref, xn_ref, acc_ref):
    n_chunks = wg_ref.shape[0]
    x = x_ref[...]
    xn_ref[...] = _rmsnorm(x, pre_ref[...]).astype(BF16)
    acc_ref[...] = jnp.zeros_like(acc_ref)

    for c in range(n_chunks):
        xn = xn_ref[...]
        a = _dot(xn, wg_ref[c])
        b = _dot(xn, wu_ref[c])
        hm = (a * _sigmoid(a) * b).astype(BF16)
        acc_ref[...] += _dot(hm, wd_ref[c])
    o_ref[...] = x + 0.5 * _rmsnorm(acc_ref[...], post_ref[...])


def _ffn(x, pre_g, post_g, wg, wu, wd, tm):
    n, d = x.shape
    nc, _, fc = wg.shape
    full = lambda shape: pl.BlockSpec(shape, lambda i: (0,) * len(shape))
    return pl.pallas_call(
        _ffn_body,
        grid=(n // tm,),
        in_specs=[
            pl.BlockSpec((tm, d), lambda i: (i, 0)),
            full((1, d)), full((1, d)),
            full((nc, d, fc)), full((nc, d, fc)), full((nc, fc, d)),
        ],
        out_specs=pl.BlockSpec((tm, d), lambda i: (i, 0)),
        out_shape=jax.ShapeDtypeStruct((n, d), F32),
        scratch_shapes=[pltpu.VMEM((tm, d), BF16), pltpu.VMEM((tm, d), F32)],
        compiler_params=pltpu.CompilerParams(
            dimension_semantics=("arbitrary",), vmem_limit_bytes=VMEM_LIMIT_BYTES),
        name="ffn",
    )(x, pre_g, post_g, wg, wu, wd)


_UV0, _PZ0, _K0, _CZ0, _KI0, _W1_COLS = 0, 512, 768, 1024, 1536, 1664
_QT0, _VT0, _QIT0, _WIT0, _W2_ROWS = 0, 256, 512, 640, 656


def _front_body(blocks_per_seq,
                x_ref, pre_ref, w1_ref, w2t_ref, gmvg_ref, ws_ref, gmb_ref, poolw_ref,
                pscale_ref, dw_ref, cb_ref, lng_ref, lnb_ref,
                bg_ref, bp_ref, bc_ref, k_ref, ki_ref, qt_ref, vt_ref, qit_ref, wit_ref,
                hbuf, pbuf):
    tm = x_ref.shape[0]
    w = BRANCH_W
    j = pl.program_id(0) % blocks_per_seq

    @pl.when(j == 0)
    def _():
        hbuf[0:CONV_HALO, :] = jnp.zeros((CONV_HALO, w), F32)
        pbuf[0:POOL_HALO, :] = jnp.zeros((POOL_HALO, w), F32)

    xn = _rmsnorm(x_ref[...], pre_ref[...]).astype(BF16)

    k_ref[...] = _dot(xn, w1_ref[:, _K0:_K0 + w]).astype(BF16)
    ki_ref[...] = _dot(xn, w1_ref[:, _KI0:_KI0 + LANES]).astype(BF16)
    zt = lax.dot_general(w2t_ref[...], xn, (((1,), (1,)), ((), ())),
                         preferred_element_type=F32)
    for cc in range(tm // ATT_BLOCK):
        sl = slice(cc * ATT_BLOCK, (cc + 1) * ATT_BLOCK)
        qt_ref[0, cc] = (zt[_QT0:_QT0 + w, sl] * Q_SCALE).astype(BF16)
        vt_ref[0, cc] = zt[_VT0:_VT0 + w, sl].astype(BF16)
        qit_ref[0, cc] = zt[_QIT0:_QIT0 + LANES, sl].astype(BF16)
        wit_ref[0, cc] = zt[_WIT0:_WIT0 + SUBLANES, sl]

    uv = _dot(xn, w1_ref[:, _UV0:_UV0 + 2 * w])
    guv = uv * (0.5 * (1.0 + jnp.tanh(math.sqrt(2.0 / math.pi) * (uv + 0.044715 * (uv ** 3)))))
    u = guv[:, 0:w]
    vv = _rmsnorm(guv[:, w:2 * w], gmvg_ref[...])
    row = lax.broadcasted_iota(jnp.int32, (GM_CHUNK, GM_GROUPS * GM_CHUNK), 0)
    col = lax.broadcasted_iota(jnp.int32, (GM_CHUNK, GM_GROUPS * GM_CHUNK), 1)
    wsm = jnp.where((col % GM_CHUNK) <= row, ws_ref[...], 0.0).astype(BF16)
    lane_group = lax.broadcasted_iota(jnp.int32, (GM_CHUNK, w), 1) // (w // GM_GROUPS)
    for c in range(tm // GM_CHUNK):
        rs = slice(c * GM_CHUNK, (c + 1) * GM_CHUNK)
        vc = vv[rs, :]
        stacked = jnp.concatenate(
            [jnp.where(lane_group == g, vc, 0.0) for g in range(GM_GROUPS)], axis=0).astype(BF16)
        mixed = _dot(wsm, stacked) + gmb_ref[...]
        bg_ref[rs, :] = (u[rs, :] * mixed).astype(BF16)

    p = _dot(xn, w1_ref[:, _PZ0:_PZ0 + w])
    pbuf[POOL_HALO:POOL_HALO + tm, :] = p
    lane_win = lax.broadcasted_iota(jnp.int32, (tm, w), 1) // (w // len(POOL_WINDOWS))
    pos1 = (j * tm + lax.broadcasted_iota(jnp.int32, (tm, w), 0) + 1).astype(F32)
    run = p
    win = jnp.zeros((tm, w), F32)
    cnt = jnp.zeros((tm, w), F32)
    shift = 1
    for g, wlen in enumerate(POOL_WINDOWS):
        while shift < wlen:
            run = run + pbuf[POOL_HALO - shift:POOL_HALO - shift + tm, :]
            shift += 1
        win = jnp.where(lane_win == g, run, win)
        cnt = jnp.where(lane_win == g, jnp.minimum(pos1, float(wlen)), cnt)
    dpool = (win / cnt - p).astype(BF16)
    bp_ref[...] = (_dot(dpool, poolw_ref[...]) * pscale_ref[...]).astype(BF16)
    pbuf[0:POOL_HALO, :] = pbuf[tm:tm + POOL_HALO, :]

    cz = _dot(xn, w1_ref[:, _CZ0:_CZ0 + 2 * w])
    hbuf[CONV_HALO:CONV_HALO + tm, :] = cz[:, 0:w] * _sigmoid(cz[:, w:2 * w])
    for r in range(tm // CONV_ROWS):
        base = r * CONV_ROWS + CONV_HALO - (CONV_K - 1)
        acc = jnp.zeros((CONV_ROWS, w), F32)
        for t in range(CONV_K):
            acc = acc + hbuf[base + t:base + t + CONV_ROWS, :] * dw_ref[t:t + 1, :]
        hc = acc + cb_ref[...]
        mu = jnp.mean(hc, axis=-1, keepdims=True)
        xc = hc - mu
        yn = xc * lax.rsqrt(jnp.mean(xc * xc, axis=-1, keepdims=True) + EPS)
        yn = yn * lng_ref[...] + lnb_ref[...]
        bc_ref[r * CONV_ROWS:(r + 1) * CONV_ROWS, :] = (yn * _sigmoid(yn)).astype(BF16)
    hbuf[0:CONV_HALO, :] = hbuf[tm:tm + CONV_HALO, :]


def _front(x, seq, pre_g, w1, w2t, gmvg, ws_cat, gmb2d, poolw, pscale, dw, cb, lng, lnb, tm):
    n, d = x.shape
    batch = n // seq
    bps = seq // tm
    w = BRANCH_W
    nch = seq // ATT_BLOCK
    cpb = tm // ATT_BLOCK
    full = lambda a: pl.BlockSpec(a.shape, lambda i: (0,) * a.ndim)
    tok = lambda width: pl.BlockSpec((tm, width), lambda i: (i, 0))
    chunked = lambda rows: pl.BlockSpec((1, cpb, rows, ATT_BLOCK),
                                        lambda i: (i // bps, i % bps, 0, 0))
    params = (pre_g, w1, w2t, gmvg, ws_cat, gmb2d, poolw, pscale, dw, cb, lng, lnb)
    out_shape = (
        jax.ShapeDtypeStruct((n, w), BF16),
        jax.ShapeDtypeStruct((n, w), BF16),
        jax.ShapeDtypeStruct((n, w), BF16),
        jax.ShapeDtypeStruct((n, w), BF16),
        jax.ShapeDtypeStruct((n, LANES), BF16),
        jax.ShapeDtypeStruct((batch, nch, w, ATT_BLOCK), BF16),
        jax.ShapeDtypeStruct((batch, nch, w, ATT_BLOCK), BF16),
        jax.ShapeDtypeStruct((batch, nch, LANES, ATT_BLOCK), BF16),
        jax.ShapeDtypeStruct((batch, nch, SUBLANES, ATT_BLOCK), F32),
    )
    out_specs = (tok(w), tok(w), tok(w), tok(w), tok(LANES),
                 chunked(w), chunked(w), chunked(LANES), chunked(SUBLANES))
    return pl.pallas_call(
        functools.partial(_front_body, bps),
        grid=(n // tm,),
        in_specs=[tok(d)] + [full(a) for a in params],
        out_specs=out_specs,
        out_shape=out_shape,
        scratch_shapes=[pltpu.VMEM((tm + CONV_HALO, w), F32), pltpu.VMEM((tm + POOL_HALO, w), F32)],
        compiler_params=pltpu.CompilerParams(
            dimension_semantics=("arbitrary",), vmem_limit_bytes=VMEM_LIMIT_BYTES),
        name="mixer_front",
    )(x, *params)


def _attn_body(topk,
               tab_ref, qt_ref, qit_ref, wit_ref, k_ref, ki_ref, vt_ref, o_ref,
               s_ref, s16_ref, bd_ref, bp_ref, tri_ref, qm_ref, qim_ref, acc_ref, lg0_ref, lg1_ref):
    qb = ATT_BLOCK
    nch = vt_ref.shape[1]
    b = pl.program_id(0)
    i = pl.program_id(1)
    key = lax.broadcasted_iota(jnp.int32, (qb, qb), 0)
    qry = lax.broadcasted_iota(jnp.int32, (qb, qb), 1)

    @pl.when((b == 0) & (i == 0))
    def _():
        tri_ref[...] = jnp.where(qry <= key, 1.0, 0.0).astype(BF16)
        max_exact = REL_BUCKETS // 2
        for ref, off in ((bd_ref, 0), (bp_ref, qb)):
            n = jnp.maximum(qry - key + off, 0)
            large = max_exact + (
                jnp.log(jnp.maximum(n, 1).astype(F32) / max_exact)
                / math.log(REL_MAX_DIST / max_exact) * (REL_BUCKETS - max_exact)).astype(jnp.int32)
            bucket = jnp.where(n < max_exact, n, jnp.minimum(large, REL_BUCKETS - 1))
            for h in range(ATT_HEADS):
                bias = jnp.zeros((qb, qb), F32)
                for k in range(REL_BUCKETS):
                    bias = jnp.where(bucket == k, tab_ref[k * ATT_HEADS + h], bias)
                ref[h] = (bias - tab_ref[(REL_BUCKETS - 1) * ATT_HEADS + h]) * LOG2E

    qt = qt_ref[0, 0]
    qit = qit_ref[0, 0]
    q_head = lax.broadcasted_iota(jnp.int32, qt.shape, 0) // ATT_HD
    qi_head = lax.broadcasted_iota(jnp.int32, qit.shape, 0) // IDX_HD
    for h in range(ATT_HEADS):
        qm_ref[h] = jnp.where(q_head == h, qt, jnp.zeros_like(qt))
    for h in range(IDX_HEADS):
        qim_ref[h] = jnp.where(qi_head == h, qit, jnp.zeros_like(qit))
    wv = wit_ref[0, 0] * ((IDX_HEADS ** -0.5) * (IDX_HD ** -0.5))

    t_pos = i * qb + qry
    n_pairs = (i + 2) // 2

    def keys_of(c):
        return pl.ds(pl.multiple_of(c * qb, qb), qb)

    def score_chunk(c):
        kic = ki_ref[0, keys_of(jnp.minimum(c, nch - 1)), :]
        sc = jnp.zeros((qb, qb), F32)
        for h in range(IDX_HEADS):
            sc = sc + wv[h:h + 1, :] * jnp.maximum(_dot(kic, qim_ref[h]), 0.0)
        valid = (c * qb + key) <= t_pos
        s_ref[c] = jnp.where(valid, sc, -jnp.inf)
        s16_ref[c] = jnp.where(valid, sc, -jnp.inf).astype(BF16)
        return (_fold8(jnp.where(valid, sc, -jnp.inf), jnp.maximum),
                _fold8(jnp.where(valid, sc, jnp.inf), jnp.minimum))

    def score_pair(p, carry):
        rmax8, rmin8 = carry
        hi0, lo0 = score_chunk(2 * p)
        hi1, lo1 = score_chunk(2 * p + 1)
        return (jnp.maximum(rmax8, jnp.maximum(hi0, hi1)), jnp.minimum(rmin8, jnp.minimum(lo0, lo1)))

    rmax8, rmin8 = lax.fori_loop(
        0, n_pairs, score_pair,
        (jnp.full((SUBLANES, qb), -jnp.inf, F32), jnp.full((SUBLANES, qb), jnp.inf, F32)))
    rmax = jnp.max(rmax8, axis=0, keepdims=True)
    rmin = jnp.min(rmin8, axis=0, keepdims=True)

    n_valid = (i * qb + lax.broadcasted_iota(jnp.int32, (1, qb), 1) + 1).astype(F32)
    kp = jnp.minimum(float(topk), n_valid)

    def count_ge(thr):
        def body(p, acc):
            g0 = _fold8(jnp.where(s_ref[2 * p] >= thr, 1.0, 0.0), jnp.add)
            g1 = _fold8(jnp.where(s_ref[2 * p + 1] >= thr, 1.0, 0.0), jnp.add)
            return acc + (g0 + g1)
        acc = lax.fori_loop(0, n_pairs, body, jnp.zeros((SUBLANES, qb), F32))
        return jnp.sum(acc, axis=0, keepdims=True)

    one16 = jnp.ones((), BF16)
    zero16 = jnp.zeros((), BF16)

    def count_ge16(thr16):
        def body(p, acc):
            g0 = _fold(jnp.where(s16_ref[2 * p] >= thr16, one16, zero16), jnp.add, BF16_ROWS)
            g1 = _fold(jnp.where(s16_ref[2 * p + 1] >= thr16, one16, zero16), jnp.add, BF16_ROWS)
            return acc + (g0 + g1).astype(F32)
        acc = lax.fori_loop(0, n_pairs, body, jnp.zeros((BF16_ROWS, qb), F32))
        return jnp.sum(acc, axis=0, keepdims=True)

    cap16 = rmax.astype(BF16).astype(F32)

    def coarse(_, carry):
        lo, hi = carry
        mid16 = (0.5 * lo + 0.5 * jnp.minimum(hi, cap16)).astype(BF16)
        feas = count_ge16(mid16) >= kp
        mid = mid16.astype(F32)
        return jnp.where(feas, mid, lo), jnp.where(feas, hi, mid)

    lo16, hi = lax.fori_loop(0, COARSE_STEPS, coarse,
                             (rmin.astype(BF16).astype(F32), jnp.full((1, qb), jnp.inf, F32)))
    lo = lo16 - (jnp.abs(lo16) * BF16_INTERVAL + TINY)
    chi = count_ge(hi)

    def bisect(_, carry):
        lo, hi, chi = carry
        mid = 0.5 * lo + 0.5 * jnp.minimum(hi, rmax)
        cnt = count_ge(mid)
        feas = cnt >= kp
        return (jnp.where(feas, mid, lo), jnp.where(feas, hi, mid), jnp.where(feas, chi, cnt))

    lo, hi, chi = lax.fori_loop(0, BISECT_STEPS, bisect, (lo, hi, chi))

    def max_below(thr):
        def body(p, acc):
            x0 = s_ref[2 * p]
            x1 = s_ref[2 * p + 1]
            m0 = _fold8(jnp.where(x0 < thr, x0, -jnp.inf), jnp.maximum)
            m1 = _fold8(jnp.where(x1 < thr, x1, -jnp.inf), jnp.maximum)
            return jnp.maximum(acc, jnp.maximum(m0, m1))
        acc = lax.fori_loop(0, n_pairs, body, jnp.full((SUBLANES, qb), -jnp.inf, F32))
        return jnp.max(acc, axis=0, keepdims=True)

    def finish_cond(state):
        return state[4] > 0.0

    def finish_body(state):
        hi, chi, tau, done, _ = state
        m = max_below(hi)
        cnt = count_ge(m)
        feas = cnt >= kp
        active = done < 0.5
        tau = jnp.where(active & feas, m, tau)
        hi = jnp.where(active & (~feas), m, hi)
        chi = jnp.where(active & (~feas), cnt, chi)
        done = jnp.where(active & feas, 1.0, done)
        return hi, chi, tau, done, jnp.max(1.0 - done)

    hi, chi, tau, _, _ = lax.while_loop(
        finish_cond, finish_body,
        (hi, chi, lo, jnp.zeros((1, qb), F32), jnp.float32(1.0)))
    need = kp - chi

    acc_ref[...] = jnp.zeros_like(acc_ref)
    lg1_ref[...] = jnp.full(lg1_ref.shape, NEG, F32)
    heads = range(ATT_HEADS)

    def stage_a(c, bias, live, buf, need_left):
        x = s_ref[c]
        eq = jnp.where(x == tau, 1.0, 0.0)
        if live is not None:
            eq = jnp.where(live, eq, 0.0)
        prefix = _dot(tri_ref[...], eq.astype(BF16))
        maskadd = jnp.where(x >= jnp.where(prefix <= need_left, tau, hi), 0.0, NEG)
        if live is not None:
            maskadd = jnp.where(live, maskadd, NEG)
        kc = k_ref[0, keys_of(c), :]
        mxs = []
        for h in heads:
            lg = _dot(kc, qm_ref[h]) + maskadd
            if bias is not None:
                lg = lg + bias[h]
            buf[h] = lg
            mxs.append(jnp.max(_fold8(lg, jnp.maximum), axis=0, keepdims=True))
        return tuple(mxs), need_left - prefix[qb - 1:qb, :]

    ones_rows = jnp.ones((2 * SUBLANES, qb), BF16)

    def stage_b(c, buf, mxs, ms, ls):
        vtc = vt_ref[0, c]
        new_ms = [jnp.maximum(ms[h], mxs[h]) for h in heads]
        alphas = [jnp.exp2(ms[h] - new_ms[h]) for h in heads]
        pvs = [_dot(jnp.concatenate([vtc[h * ATT_HD:(h + 1) * ATT_HD, :], ones_rows], axis=0),
                    jnp.exp2(buf[h] - new_ms[h]).astype(BF16)) for h in heads]
        new_ls = [alphas[h] * ls[h] + pvs[h][ATT_HD:ATT_HD + 1, :] for h in heads]
        pv = jnp.concatenate([pvs[h][0:ATT_HD, :] for h in heads], axis=0)
        alpha_rows = jnp.concatenate(
            [jnp.broadcast_to(alphas[h], (ATT_HD, qb)) for h in heads], axis=0)
        acc_ref[...] = acc_ref[...] * alpha_rows + pv
        return tuple(new_ms), tuple(new_ls)

    def far_pair(p, carry):
        ms, ls, need_left, pending = carry
        mx0, need_left = stage_a(2 * p, None, None, lg0_ref, need_left)
        ms, ls = stage_b(jnp.maximum(2 * p - 1, 0), lg1_ref, pending, ms, ls)
        mx1, need_left = stage_a(2 * p + 1, None, None, lg1_ref, need_left)
        ms, ls = stage_b(2 * p, lg0_ref, mx0, ms, ls)
        return ms, ls, need_left, mx1

    far_pairs = jnp.maximum(i - 1, 0) // 2
    m_init = tuple(jnp.full((1, qb), M_INIT, F32) for _ in heads)
    ms, ls, need_left, pending = lax.fori_loop(
        0, far_pairs, far_pair,
        (m_init, tuple(jnp.zeros((1, qb), F32) for _ in heads), need, m_init))
    c_far = jnp.maximum(i - 2, 0)
    c_prev = jnp.maximum(i - 1, 0)
    mx_far, need_left = stage_a(c_far, None, (i >= 2) & ((i - 1) % 2 == 1), lg0_ref, need_left)
    ms, ls = stage_b(jnp.maximum(2 * far_pairs - 1, 0), lg1_ref, pending, ms, ls)
    mx_prev, need_left = stage_a(c_prev, bp_ref, i >= 1, lg1_ref, need_left)
    ms, ls = stage_b(c_far, lg0_ref, mx_far, ms, ls)
    mx_diag, _ = stage_a(i, bd_ref, None, lg0_ref, need_left)
    ms, ls = stage_b(c_prev, lg1_ref, mx_prev, ms, ls)
    _, ls = stage_b(i, lg0_ref, mx_diag, ms, ls)

    for h in range(ATT_HEADS):
        rows = slice(h * ATT_HD, (h + 1) * ATT_HD)
        acc_ref[rows, :] = acc_ref[rows, :] / ls[h]
    o_ref[0] = acc_ref[...].T.astype(BF16)


def _attention(tab, qt, qit, wit, k, ki, vt):
    batch, nch, w, qb = qt.shape
    seq = nch * qb
    topk = min(TOPK_MAX, seq // 4)
    per_block = lambda rows: pl.BlockSpec((1, 1, rows, qb), lambda b, i: (b, i, 0, 0))
    return pl.pallas_call(
        functools.partial(_attn_body, topk),
        grid=(batch, nch),
        in_specs=[
            pl.BlockSpec(memory_space=pltpu.SMEM),
            per_block(w), per_block(LANES), per_block(SUBLANES),
            pl.BlockSpec((1, seq, w), lambda b, i: (b, 0, 0)),
            pl.BlockSpec((1, seq, LANES), lambda b, i: (b, 0, 0)),
            pl.BlockSpec((1, nch, w, qb), lambda b, i: (b, 0, 0, 0)),
        ],
        out_specs=pl.BlockSpec((1, qb, w), lambda b, i: (b, i, 0)),
        out_shape=jax.ShapeDtypeStruct((batch, seq, w), BF16),
        scratch_shapes=[
            pltpu.VMEM((nch + 1, qb, qb), F32),
            pltpu.VMEM((nch + 1, qb, qb), BF16),
            pltpu.VMEM((ATT_HEADS, qb, qb), F32),
            pltpu.VMEM((ATT_HEADS, qb, qb), F32),
            pltpu.VMEM((qb, qb), BF16),
            pltpu.VMEM((ATT_HEADS, w, qb), BF16),
            pltpu.VMEM((IDX_HEADS, LANES, qb), BF16),
            pltpu.VMEM((w, qb), F32),
            pltpu.VMEM((ATT_HEADS, qb, qb), F32),
            pltpu.VMEM((ATT_HEADS, qb, qb), F32),
        ],
        compiler_params=pltpu.CompilerParams(
            dimension_semantics=("arbitrary", "arbitrary"), vmem_limit_bytes=VMEM_LIMIT_BYTES),
        name="sparse_attention",
    )(tab, qt, qit, wit, k, ki, vt)


def _back_body(x_ref, bg_ref, bp_ref, ba_ref, bc_ref, pre_ref, post_ref, wgate_ref, wbr_ref,
               wout_ref, o_ref):
    x = x_ref[...]
    xn = _rmsnorm(x, pre_ref[...]).astype(BF16)
    y = None
    for n, br in enumerate((bg_ref, bp_ref, ba_ref, bc_ref)):
        term = _sigmoid(_dot(xn, wgate_ref[n])) * _dot(br[...], wbr_ref[n])
        y = term if y is None else y + term
    h = _dot(y.astype(BF16), wout_ref[...])
    o_ref[...] = x + _rmsnorm(h, post_ref[...])


def _back(x, bg, bp, ba, bc, pre_g, post_g, wgate, wbr, wout, tm):
    n, d = x.shape
    w = BRANCH_W
    full = lambda a: pl.BlockSpec(a.shape, lambda i: (0,) * a.ndim)
    tok = lambda width: pl.BlockSpec((tm, width), lambda i: (i, 0))
    params = (pre_g, post_g, wgate, wbr, wout)
    return pl.pallas_call(
        _back_body,
        grid=(n // tm,),
        in_specs=[tok(d), tok(w), tok(w), tok(w), tok(w)] + [full(a) for a in params],
        out_specs=tok(d),
        out_shape=jax.ShapeDtypeStruct((n, d), F32),
        compiler_params=pltpu.CompilerParams(
            dimension_semantics=("arbitrary",), vmem_limit_bytes=VMEM_LIMIT_BYTES),
        name="mixer_back",
    )(x, bg, bp, ba, bc, *params)


def _split_ffn_weights(w_gu, w_down):
    d, f2 = w_gu.shape
    f = f2 // 2
    nc = f // FFN_CHUNK
    wg = w_gu[:, :f].reshape(d, nc, FFN_CHUNK).transpose(1, 0, 2).astype(BF16)
    wu = w_gu[:, f:].reshape(d, nc, FFN_CHUNK).transpose(1, 0, 2).astype(BF16)
    wd = w_down.reshape(nc, FFN_CHUNK, d).astype(BF16)
    return wg, wu, wd


def _pack_front_weights(w_in):
    d = w_in.shape[0]
    w = BRANCH_W
    o = 0
    uv = w_in[:, o:o + 2 * w]; o += 2 * w
    pz = w_in[:, o:o + w]; o += w
    qz = w_in[:, o:o + w]; o += w
    kz = w_in[:, o:o + w]; o += w
    vz = w_in[:, o:o + w]; o += w
    qi = w_in[:, o:o + IDX_HEADS * IDX_HD]; o += IDX_HEADS * IDX_HD
    ki = w_in[:, o:o + IDX_HD]; o += IDX_HD
    wi = w_in[:, o:o + IDX_HEADS]; o += IDX_HEADS
    cz = w_in[:, o:o + 2 * w]; o += 2 * w
    gz = w_in[:, o:]
    w1 = jnp.concatenate([uv, pz, kz, cz] + [ki] * (LANES // IDX_HD), axis=1).astype(BF16)
    wi_pad = jnp.pad(wi, ((0, 0), (0, _W2_ROWS - _WIT0 - IDX_HEADS)))
    w2t = jnp.concatenate([qz, vz, qi, wi_pad], axis=1).T.astype(BF16)
    wgate = gz.reshape(d, N_BRANCH, d).transpose(1, 0, 2).astype(BF16)
    return w1, w2t, wgate


def _block_diag(pw):
    g, c, _ = pw.shape
    out = jnp.zeros((g * c, g * c), pw.dtype)
    for k in range(g):
        out = out.at[k * c:(k + 1) * c, k * c:(k + 1) * c].set(pw[k])
    return out


def kernel(x, ffn1_pre_g, ffn1_post_g, ffn1_w_gu, ffn1_w_down, mix_pre_g, mix_post_g, w_in,
           gm_v_g, gm_ws, gm_b, pool_w, pool_scale, conv_dw, conv_b, conv_ln_g, conv_ln_b,
           w_branch, w_out, ffn2_pre_g, ffn2_post_g, ffn2_w_gu, ffn2_w_down, rel_bias):
    batch, seq, d = x.shape
    depth = w_in.shape[0]
    n = batch * seq
    tm = min(512, seq)
    w = BRANCH_W
    row = lambda a: a.reshape(1, -1)
    tab = rel_bias.reshape(-1)

    xf = x.reshape(n, d)
    for l in range(depth):
        xf = _ffn(xf, row(ffn1_pre_g[l]), row(ffn1_post_g[l]),
                  *_split_ffn_weights(ffn1_w_gu[l], ffn1_w_down[l]), tm)

        w1, w2t, wgate = _pack_front_weights(w_in[l])
        ws_cat = gm_ws[l].transpose(1, 0, 2).reshape(GM_CHUNK, GM_GROUPS * GM_CHUNK)
        gmb2d = jnp.repeat(gm_b[l].T, w // GM_GROUPS, axis=1)
        dw = jnp.pad(conv_dw[l], ((0, 1), (0, 0)))
        bg, bp, bc, k, ki, qt, vt, qit, wit = _front(
            xf, seq, row(mix_pre_g[l]), w1, w2t, row(gm_v_g[l]), ws_cat, gmb2d,
            _block_diag(pool_w[l]).astype(BF16), row(pool_scale[l]), dw, row(conv_b[l]),
            row(conv_ln_g[l]), row(conv_ln_b[l]), tm)
        ba = _attention(tab, qt, qit, wit, k.reshape(batch, seq, w),
                        ki.reshape(batch, seq, LANES), vt)
        xf = _back(xf, bg, bp, ba.reshape(n, w), bc, row(mix_pre_g[l]), row(mix_post_g[l]),
                   wgate, w_branch[l].astype(BF16), w_out[l].astype(BF16), tm)

        xf = _ffn(xf, row(ffn2_pre_g[l]), row(ffn2_post_g[l]),
                  *_split_ffn_weights(ffn2_w_gu[l], ffn2_w_down[l]), tm)
    return xf.reshape(batch, seq, d)
```

```python
import functools
import math

import jax
import jax.numpy as jnp
from jax import lax
from jax.experimental import pallas as pl
from jax.experimental.pallas import tpu as pltpu

F32 = jnp.float32
BF16 = jnp.bfloat16

EPS = 1e-6
BRANCH_W = 256
N_BRANCH = 4
GM_GROUPS = 4
GM_CHUNK = 128
POOL_WINDOWS = (2, 4, 8, 16)
ATT_HEADS = 4
ATT_HD = 64
IDX_HEADS = 4
IDX_HD = 32
TOPK_MAX = 256
REL_BUCKETS = 32
REL_MAX_DIST = 128
CONV_K = 31

VMEM_LIMIT_BYTES = 56 * 1024 * 1024
LANES = 128
SUBLANES = 8

FFN_CHUNK = 256
CONV_HALO = 32
POOL_HALO = 16
CONV_ROWS = 64
ATT_BLOCK = 256
COARSE_STEPS = 10
BISECT_STEPS = 8
BF16_ROWS = 16
BF16_INTERVAL = 2.0 ** -6
TINY = 1e-30
LOG2E = math.log2(math.e)
Q_SCALE = ATT_HD ** -0.5 * LOG2E
NEG = -1e30
M_INIT = -1e29


def _rmsnorm(x, g):
    return x * lax.rsqrt(jnp.mean(x * x, axis=-1, keepdims=True) + EPS) * g


def _sigmoid(x):
    return 1.0 / (1.0 + jnp.exp(-x))


def _dot(a, b):
    return jnp.dot(a, b, preferred_element_type=F32)


def _fold8(x, op):
    return _fold(x, op, SUBLANES)


def _fold(x, op, rows):
    parts = [x[j * rows:(j + 1) * rows] for j in range(x.shape[0] // rows)]
    while len(parts) > 1:
        nxt = [op(parts[j], parts[j + 1]) for j in range(0, len(parts) - 1, 2)]
        if len(parts) % 2:
            nxt.append(parts[-1])
        parts = nxt
    return parts[0]


def _ffn_body(x_ref, pre_ref, post_ref, wgu_ref, wd_ref, o_ref, xn_ref, acc_ref):
    f = wd_ref.shape[0]
    x = x_ref[...]
    xn_ref[...] = _rmsnorm(x, pre_ref[...]).astype(BF16)
    acc_ref[...] = jnp.zeros_like(acc_ref)

    for c in range(0, f, FFN_CHUNK):
        xn = xn_ref[...]
        a = _dot(xn, wgu_ref[:, c:c + FFN_CHUNK])
        b = _dot(xn, wgu_ref[:, f + c:f + c + FFN_CHUNK])
        hm = (a * _sigmoid(a) * b).astype(BF16)
        acc_ref[...] += _dot(hm, wd_ref[c:c + FFN_CHUNK, :])
    o_ref[...] = x + 0.5 * _rmsnorm(acc_ref[...], post_ref[...])


def _ffn(x, pre_g, post_g, wgu, wd, tm):
    n, d = x.shape
    f = wd.shape[0]
    assert f % FFN_CHUNK == 0 and wgu.shape == (d, 2 * f)
    full = lambda shape: pl.BlockSpec(shape, lambda i: (0,) * len(shape))
    return pl.pallas_call(
        _ffn_body,
        grid=(n // tm,),
        in_specs=[
            pl.BlockSpec((tm, d), lambda i: (i, 0)),
            full((1, d)), full((1, d)), full((d, 2 * f)), full((f, d)),
        ],
        out_specs=pl.BlockSpec((tm, d), lambda i: (i, 0)),
        out_shape=jax.ShapeDtypeStruct((n, d), F32),
        scratch_shapes=[pltpu.VMEM((tm, d), BF16), pltpu.VMEM((tm, d), F32)],
        compiler_params=pltpu.CompilerParams(
            dimension_semantics=("arbitrary",), vmem_limit_bytes=VMEM_LIMIT_BYTES),
        name="ffn",
    )(x, pre_g, post_g, wgu, wd)


_UV0, _PZ0, _K0, _CZ0, _KI0, _W1_COLS = 0, 512, 768, 1024, 1536, 1664
_QT0, _VT0, _QIT0, _WIT0, _W2_ROWS = 0, 256, 512, 640, 656


def _front_body(blocks_per_seq,
                x_ref, pre_ref, w1_ref, w2t_ref, gmvg_ref, ws_ref, gmb_ref, poolw_ref,
                pscale_ref, dw_ref, cb_ref, lng_ref, lnb_ref,
                bg_ref, bp_ref, bc_ref, k_ref, ki_ref, qt_ref, vt_ref, qit_ref, wit_ref,
                hbuf, pbuf):
    tm = x_ref.shape[0]
    w = BRANCH_W
    j = pl.program_id(0) % blocks_per_seq

    @pl.when(j == 0)
    def _():
        hbuf[0:CONV_HALO, :] = jnp.zeros((CONV_HALO, w), F32)
        hbuf[CONV_HALO + tm:CONV_HALO + tm + SUBLANES, :] = jnp.zeros((SUBLANES, w), F32)
        pbuf[0:POOL_HALO, :] = jnp.zeros((POOL_HALO, w), F32)

    xn = _rmsnorm(x_ref[...], pre_ref[...]).astype(BF16)

    k_ref[...] = _dot(xn, w1_ref[:, _K0:_K0 + w]).astype(BF16)
    ki_ref[...] = _dot(xn, w1_ref[:, _KI0:_KI0 + LANES]).astype(BF16)
    zt = lax.dot_general(w2t_ref[...], xn, (((1,), (1,)), ((), ())),
                         preferred_element_type=F32)
    for cc in range(tm // ATT_BLOCK):
        sl = slice(cc * ATT_BLOCK, (cc + 1) * ATT_BLOCK)
        qt_ref[0, cc] = (zt[_QT0:_QT0 + w, sl] * Q_SCALE).astype(BF16)
        vt_ref[0, cc] = zt[_VT0:_VT0 + w, sl].astype(BF16)
        qit_ref[0, cc] = zt[_QIT0:_QIT0 + LANES, sl].astype(BF16)
        wit_ref[0, cc] = zt[_WIT0:_WIT0 + SUBLANES, sl]

    uv = _dot(xn, w1_ref[:, _UV0:_UV0 + 2 * w])
    guv = uv * (0.5 * (1.0 + jnp.tanh(math.sqrt(2.0 / math.pi) * (uv + 0.044715 * (uv ** 3)))))
    u = guv[:, 0:w]
    vv = _rmsnorm(guv[:, w:2 * w], gmvg_ref[...])
    row = lax.broadcasted_iota(jnp.int32, (GM_CHUNK, GM_GROUPS * GM_CHUNK), 0)
    col = lax.broadcasted_iota(jnp.int32, (GM_CHUNK, GM_GROUPS * GM_CHUNK), 1)
    wsm = jnp.where((col % GM_CHUNK) <= row, ws_ref[...], 0.0).astype(BF16)
    lane_group = lax.broadcasted_iota(jnp.int32, (GM_CHUNK, w), 1) // (w // GM_GROUPS)
    for c in range(tm // GM_CHUNK):
        rs = slice(c * GM_CHUNK, (c + 1) * GM_CHUNK)
        vc = vv[rs, :]
        stacked = jnp.concatenate(
            [jnp.where(lane_group == g, vc, 0.0) for g in range(GM_GROUPS)], axis=0).astype(BF16)
        mixed = _dot(wsm, stacked) + gmb_ref[...]
        bg_ref[rs, :] = (u[rs, :] * mixed).astype(BF16)

    p = _dot(xn, w1_ref[:, _PZ0:_PZ0 + w])
    pbuf[POOL_HALO:POOL_HALO + tm, :] = p
    lane_win = lax.broadcasted_iota(jnp.int32, (tm, w), 1) // (w // len(POOL_WINDOWS))
    pos1 = (j * tm + lax.broadcasted_iota(jnp.int32, (tm, w), 0) + 1).astype(F32)
    run = p
    win = jnp.zeros((tm, w), F32)
    cnt = jnp.zeros((tm, w), F32)
    shift = 1
    for g, wlen in enumerate(POOL_WINDOWS):
        while shift < wlen:
            run = run + pbuf[POOL_HALO - shift:POOL_HALO - shift + tm, :]
            shift += 1
        win = jnp.where(lane_win == g, run, win)
        cnt = jnp.where(lane_win == g, jnp.minimum(pos1, float(wlen)), cnt)
    dpool = (win / cnt - p).astype(BF16)
    bp_ref[...] = (_dot(dpool, poolw_ref[...]) * pscale_ref[...]).astype(BF16)
    pbuf[0:POOL_HALO, :] = pbuf[tm:tm + POOL_HALO, :]

    cz = _dot(xn, w1_ref[:, _CZ0:_CZ0 + 2 * w])
    hbuf[CONV_HALO:CONV_HALO + tm, :] = cz[:, 0:w] * _sigmoid(cz[:, w:2 * w])
    lead = CONV_HALO - (CONV_K - 1)
    for r in range(tm // CONV_ROWS):
        acc = None
        for shift in range(SUBLANES):
            part = None
            for t in range(CONV_K):
                if (lead + t) % SUBLANES == shift:
                    start = r * CONV_ROWS + lead + t - shift
                    term = hbuf[start:start + CONV_ROWS + SUBLANES, :] * dw_ref[t:t + 1, :]
                    part = term if part is None else part + term
            part = part[shift:shift + CONV_ROWS, :]
            acc = part if acc is None else acc + part
        hc = acc + cb_ref[...]
        mu = jnp.mean(hc, axis=-1, keepdims=True)
        xc = hc - mu
        yn = xc * lax.rsqrt(jnp.mean(xc * xc, axis=-1, keepdims=True) + EPS)
        yn = yn * lng_ref[...] + lnb_ref[...]
        bc_ref[r * CONV_ROWS:(r + 1) * CONV_ROWS, :] = (yn * _sigmoid(yn)).astype(BF16)
    hbuf[0:CONV_HALO, :] = hbuf[tm:tm + CONV_HALO, :]


def _front(x, seq, pre_g, w1, w2t, gmvg, ws_cat, gmb2d, poolw, pscale, dw, cb, lng, lnb, tm):
    n, d = x.shape
    batch = n // seq
    bps = seq // tm
    w = BRANCH_W
    nch = seq // ATT_BLOCK
    cpb = tm // ATT_BLOCK
    full = lambda a: pl.BlockSpec(a.shape, lambda i: (0,) * a.ndim)
    tok = lambda width: pl.BlockSpec((tm, width), lambda i: (i, 0))
    chunked = lambda rows: pl.BlockSpec((1, cpb, rows, ATT_BLOCK),
                                        lambda i: (i // bps, i % bps, 0, 0))
    params = (pre_g, w1, w2t, gmvg, ws_cat, gmb2d, poolw, pscale, dw, cb, lng, lnb)
    out_shape = (
        jax.ShapeDtypeStruct((n, w), BF16),
        jax.ShapeDtypeStruct((n, w), BF16),
        jax.ShapeDtypeStruct((n, w), BF16),
        jax.ShapeDtypeStruct((n, w), BF16),
        jax.ShapeDtypeStruct((n, LANES), BF16),
        jax.ShapeDtypeStruct((batch, nch, w, ATT_BLOCK), BF16),
        jax.ShapeDtypeStruct((batch, nch, w, ATT_BLOCK), BF16),
        jax.ShapeDtypeStruct((batch, nch, LANES, ATT_BLOCK), BF16),
        jax.ShapeDtypeStruct((batch, nch, SUBLANES, ATT_BLOCK), F32),
    )
    out_specs = (tok(w), tok(w), tok(w), tok(w), tok(LANES),
                 chunked(w), chunked(w), chunked(LANES), chunked(SUBLANES))
    return pl.pallas_call(
        functools.partial(_front_body, bps),
        grid=(n // tm,),
        in_specs=[tok(d)] + [full(a) for a in params],
        out_specs=out_specs,
        out_shape=out_shape,
        scratch_shapes=[pltpu.VMEM((tm + CONV_HALO + SUBLANES, w), F32),
                        pltpu.VMEM((tm + POOL_HALO, w), F32)],
        compiler_params=pltpu.CompilerParams(
            dimension_semantics=("arbitrary",), vmem_limit_bytes=VMEM_LIMIT_BYTES),
        name="mixer_front",
    )(x, *params)


def _attn_body(topk,
               tab_ref, qt_ref, qit_ref, wit_ref, k_ref, ki_ref, vt_ref, o_ref,
               s_ref, s16_ref, bd_ref, bp_ref, tri_ref, qm_ref, qim_ref, acc_ref, lg0_ref, lg1_ref):
    qb = ATT_BLOCK
    nch = vt_ref.shape[1]
    b = pl.program_id(0)
    i = pl.program_id(1)
    key = lax.broadcasted_iota(jnp.int32, (qb, qb), 0)
    qry = lax.broadcasted_iota(jnp.int32, (qb, qb), 1)

    @pl.when((b == 0) & (i == 0))
    def _():
        tri_ref[...] = jnp.where(qry <= key, 1.0, 0.0).astype(BF16)
        max_exact = REL_BUCKETS // 2
        for ref, off in ((bd_ref, 0), (bp_ref, qb)):
            n = jnp.maximum(qry - key + off, 0)
            large = max_exact + (
                jnp.log(jnp.maximum(n, 1).astype(F32) / max_exact)
                / math.log(REL_MAX_DIST / max_exact) * (REL_BUCKETS - max_exact)).astype(jnp.int32)
            bucket = jnp.where(n < max_exact, n, jnp.minimum(large, REL_BUCKETS - 1))
            for h in range(ATT_HEADS):
                bias = jnp.zeros((qb, qb), F32)
                for k in range(REL_BUCKETS):
                    bias = jnp.where(bucket == k, tab_ref[k * ATT_HEADS + h], bias)
                ref[h] = (bias - tab_ref[(REL_BUCKETS - 1) * ATT_HEADS + h]) * LOG2E

    qt = qt_ref[0, 0]
    qit = qit_ref[0, 0]
    q_head = lax.broadcasted_iota(jnp.int32, qt.shape, 0) // ATT_HD
    qi_head = lax.broadcasted_iota(jnp.int32, qit.shape, 0) // IDX_HD
    for h in range(ATT_HEADS):
        qm_ref[h] = jnp.where(q_head == h, qt, jnp.zeros_like(qt))
    for h in range(IDX_HEADS):
        qim_ref[h] = jnp.where(qi_head == h, qit, jnp.zeros_like(qit))
    wv = wit_ref[0, 0] * ((IDX_HEADS ** -0.5) * (IDX_HD ** -0.5))

    t_pos = i * qb + qry
    n_pairs = (i + 2) // 2

    def keys_of(c):
        return pl.ds(pl.multiple_of(c * qb, qb), qb)

    def score_chunk(c, masked):
        kic = ki_ref[0, keys_of(jnp.minimum(c, nch - 1)), :]
        sc = jnp.zeros((qb, qb), F32)
        for h in range(IDX_HEADS):
            sc = sc + wv[h:h + 1, :] * jnp.maximum(_dot(kic, qim_ref[h]), 0.0)
        if masked:
            valid = (c * qb + key) <= t_pos
            top = _fold(jnp.where(valid, sc, -jnp.inf), jnp.maximum, BF16_ROWS)
            bot = _fold(jnp.where(valid, sc, jnp.inf), jnp.minimum, BF16_ROWS)
            sc = jnp.where(valid, sc, -jnp.inf)
        s_ref[c] = sc
        sc16 = sc.astype(BF16)
        s16_ref[c] = sc16
        if not masked:
            top = _fold(sc16, jnp.maximum, BF16_ROWS).astype(F32)
            bot = _fold(sc16, jnp.minimum, BF16_ROWS).astype(F32)
        return top, bot

    def score_pair(p, carry, masked):
        top, bot = carry
        top0, bot0 = score_chunk(2 * p, masked)
        top1, bot1 = score_chunk(2 * p + 1, masked)
        return (jnp.maximum(top, jnp.maximum(top0, top1)), jnp.minimum(bot, jnp.minimum(bot0, bot1)))

    carry = lax.fori_loop(
        0, n_pairs - 1, lambda p, cr: score_pair(p, cr, False),
        (jnp.full((BF16_ROWS, qb), -jnp.inf, F32), jnp.full((BF16_ROWS, qb), jnp.inf, F32)))
    top, bot = score_pair(n_pairs - 1, carry, True)
    rmin = jnp.min(bot, axis=0, keepdims=True)
    cap16 = jnp.max(top, axis=0, keepdims=True).astype(BF16).astype(F32)
    rmax = cap16 + (jnp.abs(cap16) * BF16_INTERVAL + TINY)

    n_valid = (i * qb + lax.broadcasted_iota(jnp.int32, (1, qb), 1) + 1).astype(F32)
    kp = jnp.minimum(float(topk), n_valid)

    def count_ge(thr):
        def body(p, acc):
            g0 = _fold8(jnp.where(s_ref[2 * p] >= thr, 1.0, 0.0), jnp.add)
            g1 = _fold8(jnp.where(s_ref[2 * p + 1] >= thr, 1.0, 0.0), jnp.add)
            return acc + (g0 + g1)
        acc = lax.fori_loop(0, n_pairs, body, jnp.zeros((SUBLANES, qb), F32))
        return jnp.sum(acc, axis=0, keepdims=True)

    one16 = jnp.ones((), BF16)
    zero16 = jnp.zeros((), BF16)

    def count_ge16(thr16):
        def body(p, acc):
            g0 = _fold(jnp.where(s16_ref[2 * p] >= thr16, one16, zero16), jnp.add, BF16_ROWS)
            g1 = _fold(jnp.where(s16_ref[2 * p + 1] >= thr16, one16, zero16), jnp.add, BF16_ROWS)
            return acc + (g0 + g1).astype(F32)
        acc = lax.fori_loop(0, n_pairs, body, jnp.zeros((BF16_ROWS, qb), F32))
        return jnp.sum(acc, axis=0, keepdims=True)

    def coarse(_, carry):
        lo, hi = carry
        mid16 = (0.5 * lo + 0.5 * jnp.minimum(hi, cap16)).astype(BF16)
        feas = count_ge16(mid16) >= kp
        mid = mid16.astype(F32)
        return jnp.where(feas, mid, lo), jnp.where(feas, hi, mid)

    lo16, hi = lax.fori_loop(0, COARSE_STEPS, coarse,
                             (rmin.astype(BF16).astype(F32), jnp.full((1, qb), jnp.inf, F32)))
    lo = lo16 - (jnp.abs(lo16) * BF16_INTERVAL + TINY)
    chi = count_ge(hi)

    def bisect(_, carry):
        lo, hi, chi = carry
        mid = 0.5 * lo + 0.5 * jnp.minimum(hi, rmax)
        cnt = count_ge(mid)
        feas = cnt >= kp
        return (jnp.where(feas, mid, lo), jnp.where(feas, hi, mid), jnp.where(feas, chi, cnt))

    lo, hi, chi = lax.fori_loop(0, BISECT_STEPS, bisect, (lo, hi, chi))

    def max_below(thr):
        def body(p, acc):
            x0 = s_ref[2 * p]
            x1 = s_ref[2 * p + 1]
            m0 = _fold8(jnp.where(x0 < thr, x0, -jnp.inf), jnp.maximum)
            m1 = _fold8(jnp.where(x1 < thr, x1, -jnp.inf), jnp.maximum)
            return jnp.maximum(acc, jnp.maximum(m0, m1))
        acc = lax.fori_loop(0, n_pairs, body, jnp.full((SUBLANES, qb), -jnp.inf, F32))
        return jnp.max(acc, axis=0, keepdims=True)

    def finish_cond(state):
        return state[4] > 0.0

    def finish_body(state):
        hi, chi, tau, done, _ = state
        m = max_below(hi)
        cnt = count_ge(m)
        feas = cnt >= kp
        active = done < 0.5
        tau = jnp.where(active & feas, m, tau)
        hi = jnp.where(active & (~feas), m, hi)
        chi = jnp.where(active & (~feas), cnt, chi)
        done = jnp.where(active & feas, 1.0, done)
        return hi, chi, tau, done, jnp.max(1.0 - done)

    hi, chi, tau, _, _ = lax.while_loop(
        finish_cond, finish_body,
        (hi, chi, lo, jnp.zeros((1, qb), F32), jnp.float32(1.0)))
    need = kp - chi

    acc_ref[...] = jnp.zeros_like(acc_ref)
    lg1_ref[...] = jnp.full(lg1_ref.shape, NEG, F32)
    heads = range(ATT_HEADS)

    def stage_a(c, bias, live, buf, need_left):
        x = s_ref[c]
        eq = jnp.where(x == tau, 1.0, 0.0)
        if live is not None:
            eq = jnp.where(live, eq, 0.0)
        prefix = _dot(tri_ref[...], eq.astype(BF16))
        maskadd = jnp.where(x >= jnp.where(prefix <= need_left, tau, hi), 0.0, NEG)
        if live is not None:
            maskadd = jnp.where(live, maskadd, NEG)
        kc = k_ref[0, keys_of(c), :]
        mxs = []
        for h in heads:
            lg = _dot(kc, qm_ref[h]) + maskadd
            if bias is not None:
                lg = lg + bias[h]
            buf[h] = lg
            mxs.append(jnp.max(_fold8(lg, jnp.maximum), axis=0, keepdims=True))
        return tuple(mxs), need_left - prefix[qb - 1:qb, :]

    ones_rows = jnp.ones((2 * SUBLANES, qb), BF16)

    def stage_b(c, buf, mxs, ms, ls):
        vtc = vt_ref[0, c]
        new_ms = [jnp.maximum(ms[h], mxs[h]) for h in heads]
        alphas = [jnp.exp2(ms[h] - new_ms[h]) for h in heads]
        pvs = [_dot(jnp.concatenate([vtc[h * ATT_HD:(h + 1) * ATT_HD, :], ones_rows], axis=0),
                    jnp.exp2(buf[h] - new_ms[h]).astype(BF16)) for h in heads]
        new_ls = [alphas[h] * ls[h] + pvs[h][ATT_HD:ATT_HD + 1, :] for h in heads]
        pv = jnp.concatenate([pvs[h][0:ATT_HD, :] for h in heads], axis=0)
        alpha_rows = jnp.concatenate(
            [jnp.broadcast_to(alphas[h], (ATT_HD, qb)) for h in heads], axis=0)
        acc_ref[...] = acc_ref[...] * alpha_rows + pv
        return tuple(new_ms), tuple(new_ls)

    def far_pair(p, carry):
        ms, ls, need_left, pending = carry
        mx0, need_left = stage_a(2 * p, None, None, lg0_ref, need_left)
        ms, ls = stage_b(jnp.maximum(2 * p - 1, 0), lg1_ref, pending, ms, ls)
        mx1, need_left = stage_a(2 * p + 1, None, None, lg1_ref, need_left)
        ms, ls = stage_b(2 * p, lg0_ref, mx0, ms, ls)
        return ms, ls, need_left, mx1

    far_pairs = jnp.maximum(i - 1, 0) // 2
    m_init = tuple(jnp.full((1, qb), M_INIT, F32) for _ in heads)
    ms, ls, need_left, pending = lax.fori_loop(
        0, far_pairs, far_pair,
        (m_init, tuple(jnp.zeros((1, qb), F32) for _ in heads), need, m_init))
    c_far = jnp.maximum(i - 2, 0)
    c_prev = jnp.maximum(i - 1, 0)
    mx_far, need_left = stage_a(c_far, None, (i >= 2) & ((i - 1) % 2 == 1), lg0_ref, need_left)
    ms, ls = stage_b(jnp.maximum(2 * far_pairs - 1, 0), lg1_ref, pending, ms, ls)
    mx_prev, need_left = stage_a(c_prev, bp_ref, i >= 1, lg1_ref, need_left)
    ms, ls = stage_b(c_far, lg0_ref, mx_far, ms, ls)
    mx_diag, _ = stage_a(i, bd_ref, None, lg0_ref, need_left)
    ms, ls = stage_b(c_prev, lg1_ref, mx_prev, ms, ls)
    _, ls = stage_b(i, lg0_ref, mx_diag, ms, ls)

    for h in range(ATT_HEADS):
        rows = slice(h * ATT_HD, (h + 1) * ATT_HD)
        acc_ref[rows, :] = acc_ref[rows, :] / ls[h]
    o_ref[0] = acc_ref[...].T.astype(BF16)


def _attention(tab, qt, qit, wit, k, ki, vt):
    batch, nch, w, qb = qt.shape
    seq = nch * qb
    topk = min(TOPK_MAX, seq // 4)
    per_block = lambda rows: pl.BlockSpec((1, 1, rows, qb), lambda b, i: (b, i, 0, 0))
    return pl.pallas_call(
        functools.partial(_attn_body, topk),
        grid=(batch, nch),
        in_specs=[
            pl.BlockSpec(memory_space=pltpu.SMEM),
            per_block(w), per_block(LANES), per_block(SUBLANES),
            pl.BlockSpec((1, seq, w), lambda b, i: (b, 0, 0)),
            pl.BlockSpec((1, seq, LANES), lambda b, i: (b, 0, 0)),
            pl.BlockSpec((1, nch, w, qb), lambda b, i: (b, 0, 0, 0)),
        ],
        out_specs=pl.BlockSpec((1, qb, w), lambda b, i: (b, i, 0)),
        out_shape=jax.ShapeDtypeStruct((batch, seq, w), BF16),
        scratch_shapes=[
            pltpu.VMEM((nch + 1, qb, qb), F32),
            pltpu.VMEM((nch + 1, qb, qb), BF16),
            pltpu.VMEM((ATT_HEADS, qb, qb), F32),
            pltpu.VMEM((ATT_HEADS, qb, qb), F32),
            pltpu.VMEM((qb, qb), BF16),
            pltpu.VMEM((ATT_HEADS, w, qb), BF16),
            pltpu.VMEM((IDX_HEADS, LANES, qb), BF16),
            pltpu.VMEM((w, qb), F32),
            pltpu.VMEM((ATT_HEADS, qb, qb), F32),
            pltpu.VMEM((ATT_HEADS, qb, qb), F32),
        ],
        compiler_params=pltpu.CompilerParams(
            dimension_semantics=("arbitrary", "arbitrary"), vmem_limit_bytes=VMEM_LIMIT_BYTES),
        name="sparse_attention",
    )(tab, qt, qit, wit, k, ki, vt)


def _back_body(x_ref, bg_ref, bp_ref, ba_ref, bc_ref, pre_ref, post_ref, wgate_ref, wbr_ref,
               wout_ref, o_ref):
    x = x_ref[...]
    xn = _rmsnorm(x, pre_ref[...]).astype(BF16)
    y = None
    for n, br in enumerate((bg_ref, bp_ref, ba_ref, bc_ref)):
        term = _sigmoid(_dot(xn, wgate_ref[n])) * _dot(br[...], wbr_ref[n])
        y = term if y is None else y + term
    h = _dot(y.astype(BF16), wout_ref[...])
    o_ref[...] = x + _rmsnorm(h, post_ref[...])


def _back(x, bg, bp, ba, bc, pre_g, post_g, wgate, wbr, wout, tm):
    n, d = x.shape
    w = BRANCH_W
    full = lambda a: pl.BlockSpec(a.shape, lambda i: (0,) * a.ndim)
    tok = lambda width: pl.BlockSpec((tm, width), lambda i: (i, 0))
    params = (pre_g, post_g, wgate, wbr, wout)
    return pl.pallas_call(
        _back_body,
        grid=(n // tm,),
        in_specs=[tok(d), tok(w), tok(w), tok(w), tok(w)] + [full(a) for a in params],
        out_specs=tok(d),
        out_shape=jax.ShapeDtypeStruct((n, d), F32),
        compiler_params=pltpu.CompilerParams(
            dimension_semantics=("arbitrary",), vmem_limit_bytes=VMEM_LIMIT_BYTES),
        name="mixer_back",
    )(x, bg, bp, ba, bc, *params)


def _split_ffn_weights(w_gu, w_down):
    return w_gu.astype(BF16), w_down.astype(BF16)


def _pack_front_weights(w_in):
    d = w_in.shape[0]
    w = BRANCH_W
    o = 0
    uv = w_in[:, o:o + 2 * w]; o += 2 * w
    pz = w_in[:, o:o + w]; o += w
    qz = w_in[:, o:o + w]; o += w
    kz = w_in[:, o:o + w]; o += w
    vz = w_in[:, o:o + w]; o += w
    qi = w_in[:, o:o + IDX_HEADS * IDX_HD]; o += IDX_HEADS * IDX_HD
    ki = w_in[:, o:o + IDX_HD]; o += IDX_HD
    wi = w_in[:, o:o + IDX_HEADS]; o += IDX_HEADS
    cz = w_in[:, o:o + 2 * w]; o += 2 * w
    gz = w_in[:, o:]
    w1 = jnp.concatenate([uv, pz, kz, cz] + [ki] * (LANES // IDX_HD), axis=1).astype(BF16)
    wi_pad = jnp.pad(wi, ((0, 0), (0, _W2_ROWS - _WIT0 - IDX_HEADS)))
    w2t = jnp.concatenate([qz, vz, qi, wi_pad], axis=1).T.astype(BF16)
    wgate = gz.reshape(d, N_BRANCH, d).transpose(1, 0, 2).astype(BF16)
    return w1, w2t, wgate


def _block_diag(pw):
    g, c, _ = pw.shape
    out = jnp.zeros((g * c, g * c), pw.dtype)
    for k in range(g):
        out = out.at[k * c:(k + 1) * c, k * c:(k + 1) * c].set(pw[k])
    return out


def kernel(x, ffn1_pre_g, ffn1_post_g, ffn1_w_gu, ffn1_w_down, mix_pre_g, mix_post_g, w_in,
           gm_v_g, gm_ws, gm_b, pool_w, pool_scale, conv_dw, conv_b, conv_ln_g, conv_ln_b,
           w_branch, w_out, ffn2_pre_g, ffn2_post_g, ffn2_w_gu, ffn2_w_down, rel_bias):
    batch, seq, d = x.shape
    depth = w_in.shape[0]
    n = batch * seq
    tm = min(512, seq)
    w = BRANCH_W
    row = lambda a: a.reshape(1, -1)
    tab = rel_bias.reshape(-1)

    xf = x.reshape(n, d)
    for l in range(depth):
        xf = _ffn(xf, row(ffn1_pre_g[l]), row(ffn1_post_g[l]),
                  *_split_ffn_weights(ffn1_w_gu[l], ffn1_w_down[l]), tm)

        w1, w2t, wgate = _pack_front_weights(w_in[l])
        ws_cat = gm_ws[l].transpose(1, 0, 2).reshape(GM_CHUNK, GM_GROUPS * GM_CHUNK)
        gmb2d = jnp.repeat(gm_b[l].T, w // GM_GROUPS, axis=1)
        dw = jnp.pad(conv_dw[l], ((0, 1), (0, 0)))
        bg, bp, bc, k, ki, qt, vt, qit, wit = _front(
            xf, seq, row(mix_pre_g[l]), w1, w2t, row(gm_v_g[l]), ws_cat, gmb2d,
            _block_diag(pool_w[l]).astype(BF16), row(pool_scale[l]), dw, row(conv_b[l]),
            row(conv_ln_g[l]), row(conv_ln_b[l]), tm)
        ba = _attention(tab, qt, qit, wit, k.reshape(batch, seq, w),
                        ki.reshape(batch, seq, LANES), vt)
        xf = _back(xf, bg, bp, ba.reshape(n, w), bc, row(mix_pre_g[l]), row(mix_post_g[l]),
                   wgate, w_branch[l].astype(BF16), w_out[l].astype(BF16), tm)

        xf = _ffn(xf, row(ffn2_pre_g[l]), row(ffn2_post_g[l]),
                  *_split_ffn_weights(ffn2_w_gu[l], ffn2_w_down[l]), tm)
    return xf.reshape(batch, seq, d)
```

```python
import functools
import math

import jax
import jax.numpy as jnp
from jax import lax
from jax.experimental import pallas as pl
from jax.experimental.pallas import tpu as pltpu

F32 = jnp.float32
BF16 = jnp.bfloat16

EPS = 1e-6
BRANCH_W = 256
N_BRANCH = 4
GM_GROUPS = 4
GM_CHUNK = 128
POOL_WINDOWS = (2, 4, 8, 16)
ATT_HEADS = 4
ATT_HD = 64
IDX_HEADS = 4
IDX_HD = 32
TOPK_MAX = 256
REL_BUCKETS = 32
REL_MAX_DIST = 128
CONV_K = 31

VMEM_LIMIT_BYTES = 56 * 1024 * 1024
LANES = 128
SUBLANES = 8

FFN_CHUNK = 256
FFN_BLOCK = 1024
FFN_ROWS = 512
CONV_HALO = 32
POOL_HALO = 16
CONV_ROWS = 64
ATT_BLOCK = 256
COARSE_STEPS = 10
BISECT_STEPS = 8
BF16_ROWS = 16
BF16_INTERVAL = 2.0 ** -6
TINY = 1e-30
LOG2E = math.log2(math.e)
Q_SCALE = ATT_HD ** -0.5 * LOG2E
NEG = -1e30
M_INIT = -1e29


def _rmsnorm(x, g):
    return x * lax.rsqrt(jnp.mean(x * x, axis=-1, keepdims=True) + EPS) * g


def _sigmoid(x):
    return 1.0 / (1.0 + jnp.exp(-x))


def _dot(a, b):
    return jnp.dot(a, b, preferred_element_type=F32)


def _fold8(x, op):
    return _fold(x, op, SUBLANES)


def _fold(x, op, rows):
    parts = [x[j * rows:(j + 1) * rows] for j in range(x.shape[0] // rows)]
    while len(parts) > 1:
        nxt = [op(parts[j], parts[j + 1]) for j in range(0, len(parts) - 1, 2)]
        if len(parts) % 2:
            nxt.append(parts[-1])
        parts = nxt
    return parts[0]


def _ffn_body(x_ref, pre_ref, post_ref, wgu_ref, wd_ref, o_ref, xn_ref, acc_ref):
    f = wd_ref.shape[0]
    tm = x_ref.shape[0]
    halves = [slice(r, r + FFN_ROWS) for r in range(0, tm, FFN_ROWS)]
    for rows in halves:
        xn_ref[rows, :] = _rmsnorm(x_ref[rows, :], pre_ref[...]).astype(BF16)
    for c in range(0, f, FFN_CHUNK):
        for rows in halves:
            xn = xn_ref[rows, :]
            a = _dot(xn, wgu_ref[:, c:c + FFN_CHUNK])
            b = _dot(xn, wgu_ref[:, f + c:f + c + FFN_CHUNK])
            hm = (a * _sigmoid(a) * b).astype(BF16)
            down = _dot(hm, wd_ref[c:c + FFN_CHUNK, :])
            acc_ref[rows, :] = down if c == 0 else acc_ref[rows, :] + down
    for rows in halves:
        o_ref[rows, :] = x_ref[rows, :] + 0.5 * _rmsnorm(acc_ref[rows, :], post_ref[...])


def _ffn(x, pre_g, post_g, wgu, wd):
    n, d = x.shape
    f = wd.shape[0]
    tm = min(FFN_BLOCK, n)
    assert f % FFN_CHUNK == 0 and wgu.shape == (d, 2 * f) and tm % FFN_ROWS == 0
    full = lambda shape: pl.BlockSpec(shape, lambda i: (0,) * len(shape))
    resident = lambda shape: pl.BlockSpec(shape, lambda i: (0,) * len(shape),
                                          pipeline_mode=pl.Buffered(1))
    return pl.pallas_call(
        _ffn_body,
        grid=(n // tm,),
        in_specs=[
            pl.BlockSpec((tm, d), lambda i: (i, 0)),
            full((1, d)), full((1, d)), resident((d, 2 * f)), resident((f, d)),
        ],
        out_specs=pl.BlockSpec((tm, d), lambda i: (i, 0)),
        out_shape=jax.ShapeDtypeStruct((n, d), F32),
        scratch_shapes=[pltpu.VMEM((tm, d), BF16), pltpu.VMEM((tm, d), F32)],
        compiler_params=pltpu.CompilerParams(
            dimension_semantics=("arbitrary",), vmem_limit_bytes=VMEM_LIMIT_BYTES),
        name="ffn",
    )(x, pre_g, post_g, wgu, wd)


_UV0, _PZ0, _K0, _CZ0, _KI0, _W1_COLS = 0, 512, 768, 1024, 1536, 1664
_QT0, _VT0, _QIT0, _WIT0, _W2_ROWS = 0, 256, 512, 640, 656


def _front_body(blocks_per_seq,
                x_ref, pre_ref, w1_ref, w2t_ref, gmvg_ref, ws_ref, gmb_ref, poolw_ref,
                pscale_ref, dw_ref, cb_ref, lng_ref, lnb_ref,
                bg_ref, bp_ref, bc_ref, k_ref, ki_ref, qt_ref, vt_ref, qit_ref, wit_ref,
                hbuf, pbuf):
    tm = x_ref.shape[0]
    w = BRANCH_W
    j = pl.program_id(0) % blocks_per_seq

    @pl.when(j == 0)
    def _():
        hbuf[0:CONV_HALO, :] = jnp.zeros((CONV_HALO, w), F32)
        hbuf[CONV_HALO + tm:CONV_HALO + tm + SUBLANES, :] = jnp.zeros((SUBLANES, w), F32)
        pbuf[0:POOL_HALO, :] = jnp.zeros((POOL_HALO, w), F32)

    xn = _rmsnorm(x_ref[...], pre_ref[...]).astype(BF16)

    k_ref[...] = _dot(xn, w1_ref[:, _K0:_K0 + w]).astype(BF16)
    ki_ref[...] = _dot(xn, w1_ref[:, _KI0:_KI0 + LANES]).astype(BF16)
    zt = lax.dot_general(w2t_ref[...], xn, (((1,), (1,)), ((), ())),
                         preferred_element_type=F32)
    for cc in range(tm // ATT_BLOCK):
        sl = slice(cc * ATT_BLOCK, (cc + 1) * ATT_BLOCK)
        qt_ref[0, cc] = (zt[_QT0:_QT0 + w, sl] * Q_SCALE).astype(BF16)
        vt_ref[0, cc] = zt[_VT0:_VT0 + w, sl].astype(BF16)
        qit_ref[0, cc] = zt[_QIT0:_QIT0 + LANES, sl].astype(BF16)
        wit_ref[0, cc] = zt[_WIT0:_WIT0 + SUBLANES, sl]

    uv = _dot(xn, w1_ref[:, _UV0:_UV0 + 2 * w])
    guv = uv * (0.5 * (1.0 + jnp.tanh(math.sqrt(2.0 / math.pi) * (uv + 0.044715 * (uv ** 3)))))
    u = guv[:, 0:w]
    vv = _rmsnorm(guv[:, w:2 * w], gmvg_ref[...])
    row = lax.broadcasted_iota(jnp.int32, (GM_CHUNK, GM_GROUPS * GM_CHUNK), 0)
    col = lax.broadcasted_iota(jnp.int32, (GM_CHUNK, GM_GROUPS * GM_CHUNK), 1)
    wsm = jnp.where((col % GM_CHUNK) <= row, ws_ref[...], 0.0).astype(BF16)
    lane_group = lax.broadcasted_iota(jnp.int32, (GM_CHUNK, w), 1) // (w // GM_GROUPS)
    for c in range(tm // GM_CHUNK):
        rs = slice(c * GM_CHUNK, (c + 1) * GM_CHUNK)
        vc = vv[rs, :]
        stacked = jnp.concatenate(
            [jnp.where(lane_group == g, vc, 0.0) for g in range(GM_GROUPS)], axis=0).astype(BF16)
        mixed = _dot(wsm, stacked) + gmb_ref[...]
        bg_ref[rs, :] = (u[rs, :] * mixed).astype(BF16)

    p = _dot(xn, w1_ref[:, _PZ0:_PZ0 + w])
    pbuf[POOL_HALO:POOL_HALO + tm, :] = p
    lane_win = lax.broadcasted_iota(jnp.int32, (tm, w), 1) // (w // len(POOL_WINDOWS))
    pos1 = (j * tm + lax.broadcasted_iota(jnp.int32, (tm, w), 0) + 1).astype(F32)
    run = p
    win = jnp.zeros((tm, w), F32)
    cnt = jnp.zeros((tm, w), F32)
    shift = 1
    for g, wlen in enumerate(POOL_WINDOWS):
        while shift < wlen:
            run = run + pbuf[POOL_HALO - shift:POOL_HALO - shift + tm, :]
            shift += 1
        win = jnp.where(lane_win == g, run, win)
        cnt = jnp.where(lane_win == g, jnp.minimum(pos1, float(wlen)), cnt)
    dpool = (win / cnt - p).astype(BF16)
    bp_ref[...] = (_dot(dpool, poolw_ref[...]) * pscale_ref[...]).astype(BF16)
    pbuf[0:POOL_HALO, :] = pbuf[tm:tm + POOL_HALO, :]

    cz = _dot(xn, w1_ref[:, _CZ0:_CZ0 + 2 * w])
    hbuf[CONV_HALO:CONV_HALO + tm, :] = cz[:, 0:w] * _sigmoid(cz[:, w:2 * w])
    lead = CONV_HALO - (CONV_K - 1)
    for r in range(tm // CONV_ROWS):
        acc = None
        for shift in range(SUBLANES):
            part = None
            for t in range(CONV_K):
                if (lead + t) % SUBLANES == shift:
                    start = r * CONV_ROWS + lead + t - shift
                    term = hbuf[start:start + CONV_ROWS + SUBLANES, :] * dw_ref[t:t + 1, :]
                    part = term if part is None else part + term
            part = part[shift:shift + CONV_ROWS, :]
            acc = part if acc is None else acc + part
        hc = acc + cb_ref[...]
        mu = jnp.mean(hc, axis=-1, keepdims=True)
        xc = hc - mu
        yn = xc * lax.rsqrt(jnp.mean(xc * xc, axis=-1, keepdims=True) + EPS)
        yn = yn * lng_ref[...] + lnb_ref[...]
        bc_ref[r * CONV_ROWS:(r + 1) * CONV_ROWS, :] = (yn * _sigmoid(yn)).astype(BF16)
    hbuf[0:CONV_HALO, :] = hbuf[tm:tm + CONV_HALO, :]


def _front(x, seq, pre_g, w1, w2t, gmvg, ws_cat, gmb2d, poolw, pscale, dw, cb, lng, lnb, tm):
    n, d = x.shape
    batch = n // seq
    bps = seq // tm
    w = BRANCH_W
    nch = seq // ATT_BLOCK
    cpb = tm // ATT_BLOCK
    full = lambda a: pl.BlockSpec(a.shape, lambda i: (0,) * a.ndim)
    tok = lambda width: pl.BlockSpec((tm, width), lambda i: (i, 0))
    chunked = lambda rows: pl.BlockSpec((1, cpb, rows, ATT_BLOCK),
                                        lambda i: (i // bps, i % bps, 0, 0))
    params = (pre_g, w1, w2t, gmvg, ws_cat, gmb2d, poolw, pscale, dw, cb, lng, lnb)
    out_shape = (
        jax.ShapeDtypeStruct((n, w), BF16),
        jax.ShapeDtypeStruct((n, w), BF16),
        jax.ShapeDtypeStruct((n, w), BF16),
        jax.ShapeDtypeStruct((n, w), BF16),
        jax.ShapeDtypeStruct((n, LANES), BF16),
        jax.ShapeDtypeStruct((batch, nch, w, ATT_BLOCK), BF16),
        jax.ShapeDtypeStruct((batch, nch, w, ATT_BLOCK), BF16),
        jax.ShapeDtypeStruct((batch, nch, LANES, ATT_BLOCK), BF16),
        jax.ShapeDtypeStruct((batch, nch, SUBLANES, ATT_BLOCK), F32),
    )
    out_specs = (tok(w), tok(w), tok(w), tok(w), tok(LANES),
                 chunked(w), chunked(w), chunked(LANES), chunked(SUBLANES))
    return pl.pallas_call(
        functools.partial(_front_body, bps),
        grid=(n // tm,),
        in_specs=[tok(d)] + [full(a) for a in params],
        out_specs=out_specs,
        out_shape=out_shape,
        scratch_shapes=[pltpu.VMEM((tm + CONV_HALO + SUBLANES, w), F32),
                        pltpu.VMEM((tm + POOL_HALO, w), F32)],
        compiler_params=pltpu.CompilerParams(
            dimension_semantics=("arbitrary",), vmem_limit_bytes=VMEM_LIMIT_BYTES),
        name="mixer_front",
    )(x, *params)


def _attn_body(topk,
               tab_ref, qt_ref, qit_ref, wit_ref, k_ref, ki_ref, vt_ref, o_ref,
               s_ref, s16_ref, bd_ref, bp_ref, tri_ref, qm_ref, qim_ref, acc_ref, lg0_ref, lg1_ref):
    qb = ATT_BLOCK
    nch = vt_ref.shape[1]
    b = pl.program_id(0)
    i = pl.program_id(1)
    key = lax.broadcasted_iota(jnp.int32, (qb, qb), 0)
    qry = lax.broadcasted_iota(jnp.int32, (qb, qb), 1)

    @pl.when((b == 0) & (i == 0))
    def _():
        tri_ref[...] = jnp.where(qry <= key, 1.0, 0.0).astype(BF16)
        max_exact = REL_BUCKETS // 2
        for ref, off in ((bd_ref, 0), (bp_ref, qb)):
            n = jnp.maximum(qry - key + off, 0)
            large = max_exact + (
                jnp.log(jnp.maximum(n, 1).astype(F32) / max_exact)
                / math.log(REL_MAX_DIST / max_exact) * (REL_BUCKETS - max_exact)).astype(jnp.int32)
            bucket = jnp.where(n < max_exact, n, jnp.minimum(large, REL_BUCKETS - 1))
            for h in range(ATT_HEADS):
                bias = jnp.zeros((qb, qb), F32)
                for k in range(REL_BUCKETS):
                    bias = jnp.where(bucket == k, tab_ref[k * ATT_HEADS + h], bias)
                ref[h] = (bias - tab_ref[(REL_BUCKETS - 1) * ATT_HEADS + h]) * LOG2E

    qt = qt_ref[0, 0]
    qit = qit_ref[0, 0]
    q_head = lax.broadcasted_iota(jnp.int32, qt.shape, 0) // ATT_HD
    qi_head = lax.broadcasted_iota(jnp.int32, qit.shape, 0) // IDX_HD
    for h in range(ATT_HEADS):
        qm_ref[h] = jnp.where(q_head == h, qt, jnp.zeros_like(qt))
    for h in range(IDX_HEADS):
        qim_ref[h] = jnp.where(qi_head == h, qit, jnp.zeros_like(qit))
    wv = wit_ref[0, 0] * ((IDX_HEADS ** -0.5) * (IDX_HD ** -0.5))

    t_pos = i * qb + qry
    n_pairs = (i + 2) // 2

    def keys_of(c):
        return pl.ds(pl.multiple_of(c * qb, qb), qb)

    def score_chunk(c, masked):
        kic = ki_ref[0, keys_of(jnp.minimum(c, nch - 1)), :]
        sc = jnp.zeros((qb, qb), F32)
        for h in range(IDX_HEADS):
            sc = sc + wv[h:h + 1, :] * jnp.maximum(_dot(kic, qim_ref[h]), 0.0)
        if masked:
            valid = (c * qb + key) <= t_pos
            top = _fold(jnp.where(valid, sc, -jnp.inf), jnp.maximum, BF16_ROWS)
            bot = _fold(jnp.where(valid, sc, jnp.inf), jnp.minimum, BF16_ROWS)
            sc = jnp.where(valid, sc, -jnp.inf)
        s_ref[c] = sc
        sc16 = sc.astype(BF16)
        s16_ref[c] = sc16
        if not masked:
            top = _fold(sc16, jnp.maximum, BF16_ROWS).astype(F32)
            bot = _fold(sc16, jnp.minimum, BF16_ROWS).astype(F32)
        return top, bot

    def score_pair(p, carry, masked):
        top, bot = carry
        top0, bot0 = score_chunk(2 * p, masked)
        top1, bot1 = score_chunk(2 * p + 1, masked)
        return (jnp.maximum(top, jnp.maximum(top0, top1)), jnp.minimum(bot, jnp.minimum(bot0, bot1)))

    carry = lax.fori_loop(
        0, n_pairs - 1, lambda p, cr: score_pair(p, cr, False),
        (jnp.full((BF16_ROWS, qb), -jnp.inf, F32), jnp.full((BF16_ROWS, qb), jnp.inf, F32)))
    top, bot = score_pair(n_pairs - 1, carry, True)
    rmin = jnp.min(bot, axis=0, keepdims=True)
    cap16 = jnp.max(top, axis=0, keepdims=True).astype(BF16).astype(F32)
    rmax = cap16 + (jnp.abs(cap16) * BF16_INTERVAL + TINY)

    n_valid = (i * qb + lax.broadcasted_iota(jnp.int32, (1, qb), 1) + 1).astype(F32)
    kp = jnp.minimum(float(topk), n_valid)

    def count_ge(thr):
        def body(p, acc):
            g0 = _fold8(jnp.where(s_ref[2 * p] >= thr, 1.0, 0.0), jnp.add)
            g1 = _fold8(jnp.where(s_ref[2 * p + 1] >= thr, 1.0, 0.0), jnp.add)
            return acc + (g0 + g1)
        acc = lax.fori_loop(0, n_pairs, body, jnp.zeros((SUBLANES, qb), F32))
        return jnp.sum(acc, axis=0, keepdims=True)

    one16 = jnp.ones((), BF16)
    zero16 = jnp.zeros((), BF16)

    def count_ge16(thr16):
        def body(p, acc):
            g0 = _fold(jnp.where(s16_ref[2 * p] >= thr16, one16, zero16), jnp.add, BF16_ROWS)
            g1 = _fold(jnp.where(s16_ref[2 * p + 1] >= thr16, one16, zero16), jnp.add, BF16_ROWS)
            return acc + (g0 + g1).astype(F32)
        acc = lax.fori_loop(0, n_pairs, body, jnp.zeros((BF16_ROWS, qb), F32))
        return jnp.sum(acc, axis=0, keepdims=True)

    def coarse(_, carry):
        lo, hi = carry
        mid16 = (0.5 * lo + 0.5 * jnp.minimum(hi, cap16)).astype(BF16)
        feas = count_ge16(mid16) >= kp
        mid = mid16.astype(F32)
        return jnp.where(feas, mid, lo), jnp.where(feas, hi, mid)

    lo16, hi = lax.fori_loop(0, COARSE_STEPS, coarse,
                             (rmin.astype(BF16).astype(F32), jnp.full((1, qb), jnp.inf, F32)))
    lo = lo16 - (jnp.abs(lo16) * BF16_INTERVAL + TINY)
    chi = count_ge(hi)

    def bisect(_, carry):
        lo, hi, chi = carry
        mid = 0.5 * lo + 0.5 * jnp.minimum(hi, rmax)
        cnt = count_ge(mid)
        feas = cnt >= kp
        return (jnp.where(feas, mid, lo), jnp.where(feas, hi, mid), jnp.where(feas, chi, cnt))

    lo, hi, chi = lax.fori_loop(0, BISECT_STEPS, bisect, (lo, hi, chi))

    def max_below(thr):
        def body(p, acc):
            x0 = s_ref[2 * p]
            x1 = s_ref[2 * p + 1]
            m0 = _fold8(jnp.where(x0 < thr, x0, -jnp.inf), jnp.maximum)
            m1 = _fold8(jnp.where(x1 < thr, x1, -jnp.inf), jnp.maximum)
            return jnp.maximum(acc, jnp.maximum(m0, m1))
        acc = lax.fori_loop(0, n_pairs, body, jnp.full((SUBLANES, qb), -jnp.inf, F32))
        return jnp.max(acc, axis=0, keepdims=True)

    def finish_cond(state):
        return state[4] > 0.0

    def finish_body(state):
        hi, chi, tau, done, _ = state
        m = max_below(hi)
        cnt = count_ge(m)
        feas = cnt >= kp
        active = done < 0.5
        tau = jnp.where(active & feas, m, tau)
        hi = jnp.where(active & (~feas), m, hi)
        chi = jnp.where(active & (~feas), cnt, chi)
        done = jnp.where(active & feas, 1.0, done)
        return hi, chi, tau, done, jnp.max(1.0 - done)

    hi, chi, tau, _, _ = lax.while_loop(
        finish_cond, finish_body,
        (hi, chi, lo, jnp.zeros((1, qb), F32), jnp.float32(1.0)))
    need = kp - chi

    acc_ref[...] = jnp.zeros_like(acc_ref)
    lg1_ref[...] = jnp.full(lg1_ref.shape, NEG, F32)
    heads = range(ATT_HEADS)

    def stage_a(c, bias, live, buf, need_left):
        x = s_ref[c]
        eq = jnp.where(x == tau, 1.0, 0.0)
        if live is not None:
            eq = jnp.where(live, eq, 0.0)
        prefix = _dot(tri_ref[...], eq.astype(BF16))
        maskadd = jnp.where(x >= jnp.where(prefix <= need_left, tau, hi), 0.0, NEG)
        if live is not None:
            maskadd = jnp.where(live, maskadd, NEG)
        kc = k_ref[0, keys_of(c), :]
        mxs = []
        for h in heads:
            lg = _dot(kc, qm_ref[h]) + maskadd
            if bias is not None:
                lg = lg + bias[h]
            buf[h] = lg
            mxs.append(jnp.max(_fold8(lg, jnp.maximum), axis=0, keepdims=True))
        return tuple(mxs), need_left - prefix[qb - 1:qb, :]

    ones_rows = jnp.ones((2 * SUBLANES, qb), BF16)

    def stage_b(c, buf, mxs, ms, ls):
        vtc = vt_ref[0, c]
        new_ms = [jnp.maximum(ms[h], mxs[h]) for h in heads]
        alphas = [jnp.exp2(ms[h] - new_ms[h]) for h in heads]
        pvs = [_dot(jnp.concatenate([vtc[h * ATT_HD:(h + 1) * ATT_HD, :], ones_rows], axis=0),
                    jnp.exp2(buf[h] - new_ms[h]).astype(BF16)) for h in heads]
        new_ls = [alphas[h] * ls[h] + pvs[h][ATT_HD:ATT_HD + 1, :] for h in heads]
        pv = jnp.concatenate([pvs[h][0:ATT_HD, :] for h in heads], axis=0)
        alpha_rows = jnp.concatenate(
            [jnp.broadcast_to(alphas[h], (ATT_HD, qb)) for h in heads], axis=0)
        acc_ref[...] = acc_ref[...] * alpha_rows + pv
        return tuple(new_ms), tuple(new_ls)

    def far_pair(p, carry):
        ms, ls, need_left, pending = carry
        mx0, need_left = stage_a(2 * p, None, None, lg0_ref, need_left)
        ms, ls = stage_b(jnp.maximum(2 * p - 1, 0), lg1_ref, pending, ms, ls)
        mx1, need_left = stage_a(2 * p + 1, None, None, lg1_ref, need_left)
        ms, ls = stage_b(2 * p, lg0_ref, mx0, ms, ls)
        return ms, ls, need_left, mx1

    far_pairs = jnp.maximum(i - 1, 0) // 2
    m_init = tuple(jnp.full((1, qb), M_INIT, F32) for _ in heads)
    ms, ls, need_left, pending = lax.fori_loop(
        0, far_pairs, far_pair,
        (m_init, tuple(jnp.zeros((1, qb), F32) for _ in heads), need, m_init))
    c_far = jnp.maximum(i - 2, 0)
    c_prev = jnp.maximum(i - 1, 0)
    mx_far, need_left = stage_a(c_far, None, (i >= 2) & ((i - 1) % 2 == 1), lg0_ref, need_left)
    ms, ls = stage_b(jnp.maximum(2 * far_pairs - 1, 0), lg1_ref, pending, ms, ls)
    mx_prev, need_left = stage_a(c_prev, bp_ref, i >= 1, lg1_ref, need_left)
    ms, ls = stage_b(c_far, lg0_ref, mx_far, ms, ls)
    mx_diag, _ = stage_a(i, bd_ref, None, lg0_ref, need_left)
    ms, ls = stage_b(c_prev, lg1_ref, mx_prev, ms, ls)
    _, ls = stage_b(i, lg0_ref, mx_diag, ms, ls)

    for h in range(ATT_HEADS):
        rows = slice(h * ATT_HD, (h + 1) * ATT_HD)
        acc_ref[rows, :] = acc_ref[rows, :] / ls[h]
    o_ref[0] = acc_ref[...].T.astype(BF16)


def _attention(tab, qt, qit, wit, k, ki, vt):
    batch, nch, w, qb = qt.shape
    seq = nch * qb
    topk = min(TOPK_MAX, seq // 4)
    per_block = lambda rows: pl.BlockSpec((1, 1, rows, qb), lambda b, i: (b, i, 0, 0))
    return pl.pallas_call(
        functools.partial(_attn_body, topk),
        grid=(batch, nch),
        in_specs=[
            pl.BlockSpec(memory_space=pltpu.SMEM),
            per_block(w), per_block(LANES), per_block(SUBLANES),
            pl.BlockSpec((1, seq, w), lambda b, i: (b, 0, 0)),
            pl.BlockSpec((1, seq, LANES), lambda b, i: (b, 0, 0)),
            pl.BlockSpec((1, nch, w, qb), lambda b, i: (b, 0, 0, 0)),
        ],
        out_specs=pl.BlockSpec((1, qb, w), lambda b, i: (b, i, 0)),
        out_shape=jax.ShapeDtypeStruct((batch, seq, w), BF16),
        scratch_shapes=[
            pltpu.VMEM((nch + 1, qb, qb), F32),
            pltpu.VMEM((nch + 1, qb, qb), BF16),
            pltpu.VMEM((ATT_HEADS, qb, qb), F32),
            pltpu.VMEM((ATT_HEADS, qb, qb), F32),
            pltpu.VMEM((qb, qb), BF16),
            pltpu.VMEM((ATT_HEADS, w, qb), BF16),
            pltpu.VMEM((IDX_HEADS, LANES, qb), BF16),
            pltpu.VMEM((w, qb), F32),
            pltpu.VMEM((ATT_HEADS, qb, qb), F32),
            pltpu.VMEM((ATT_HEADS, qb, qb), F32),
        ],
        compiler_params=pltpu.CompilerParams(
            dimension_semantics=("arbitrary", "arbitrary"), vmem_limit_bytes=VMEM_LIMIT_BYTES),
        name="sparse_attention",
    )(tab, qt, qit, wit, k, ki, vt)


def _back_body(x_ref, bg_ref, bp_ref, ba_ref, bc_ref, pre_ref, post_ref, wgate_ref, wbr_ref,
               wout_ref, o_ref):
    for r in range(0, x_ref.shape[0], FFN_ROWS):
        rows = slice(r, r + FFN_ROWS)
        x = x_ref[rows, :]
        xn = _rmsnorm(x, pre_ref[...]).astype(BF16)
        y = None
        for n, br in enumerate((bg_ref, bp_ref, ba_ref, bc_ref)):
            term = _sigmoid(_dot(xn, wgate_ref[n])) * _dot(br[rows, :], wbr_ref[n])
            y = term if y is None else y + term
        h = _dot(y.astype(BF16), wout_ref[...])
        o_ref[rows, :] = x + _rmsnorm(h, post_ref[...])


def _back(x, bg, bp, ba, bc, pre_g, post_g, wgate, wbr, wout):
    n, d = x.shape
    w = BRANCH_W
    tm = min(FFN_BLOCK, n)
    assert tm % FFN_ROWS == 0
    full = lambda a: pl.BlockSpec(a.shape, lambda i: (0,) * a.ndim)
    resident = lambda a: pl.BlockSpec(a.shape, lambda i: (0,) * a.ndim,
                                      pipeline_mode=pl.Buffered(1))
    tok = lambda width: pl.BlockSpec((tm, width), lambda i: (i, 0))
    params = (pre_g, post_g, wgate, wbr, wout)
    return pl.pallas_call(
        _back_body,
        grid=(n // tm,),
        in_specs=([tok(d), tok(w), tok(w), tok(w), tok(w), full(pre_g), full(post_g)]
                  + [resident(a) for a in (wgate, wbr, wout)]),
        out_specs=tok(d),
        out_shape=jax.ShapeDtypeStruct((n, d), F32),
        compiler_params=pltpu.CompilerParams(
            dimension_semantics=("arbitrary",), vmem_limit_bytes=VMEM_LIMIT_BYTES),
        name="mixer_back",
    )(x, bg, bp, ba, bc, *params)


def _split_ffn_weights(w_gu, w_down):
    return w_gu.astype(BF16), w_down.astype(BF16)


def _pack_front_weights(w_in):
    d = w_in.shape[0]
    w = BRANCH_W
    o = 0
    uv = w_in[:, o:o + 2 * w]; o += 2 * w
    pz = w_in[:, o:o + w]; o += w
    qz = w_in[:, o:o + w]; o += w
    kz = w_in[:, o:o + w]; o += w
    vz = w_in[:, o:o + w]; o += w
    qi = w_in[:, o:o + IDX_HEADS * IDX_HD]; o += IDX_HEADS * IDX_HD
    ki = w_in[:, o:o + IDX_HD]; o += IDX_HD
    wi = w_in[:, o:o + IDX_HEADS]; o += IDX_HEADS
    cz = w_in[:, o:o + 2 * w]; o += 2 * w
    gz = w_in[:, o:]
    w1 = jnp.concatenate([uv, pz, kz, cz] + [ki] * (LANES // IDX_HD), axis=1).astype(BF16)
    wi_pad = jnp.pad(wi, ((0, 0), (0, _W2_ROWS - _WIT0 - IDX_HEADS)))
    w2t = jnp.concatenate([qz, vz, qi, wi_pad], axis=1).T.astype(BF16)
    wgate = gz.reshape(d, N_BRANCH, d).transpose(1, 0, 2).astype(BF16)
    return w1, w2t, wgate


def _block_diag(pw):
    g, c, _ = pw.shape
    out = jnp.zeros((g * c, g * c), pw.dtype)
    for k in range(g):
        out = out.at[k * c:(k + 1) * c, k * c:(k + 1) * c].set(pw[k])
    return out


def kernel(x, ffn1_pre_g, ffn1_post_g, ffn1_w_gu, ffn1_w_down, mix_pre_g, mix_post_g, w_in,
           gm_v_g, gm_ws, gm_b, pool_w, pool_scale, conv_dw, conv_b, conv_ln_g, conv_ln_b,
           w_branch, w_out, ffn2_pre_g, ffn2_post_g, ffn2_w_gu, ffn2_w_down, rel_bias):
    batch, seq, d = x.shape
    depth = w_in.shape[0]
    n = batch * seq
    tm = min(512, seq)
    w = BRANCH_W
    row = lambda a: a.reshape(1, -1)
    tab = rel_bias.reshape(-1)

    xf = x.reshape(n, d)
    for l in range(depth):
        xf = _ffn(xf, row(ffn1_pre_g[l]), row(ffn1_post_g[l]),
                  *_split_ffn_weights(ffn1_w_gu[l], ffn1_w_down[l]))

        w1, w2t, wgate = _pack_front_weights(w_in[l])
        ws_cat = gm_ws[l].transpose(1, 0, 2).reshape(GM_CHUNK, GM_GROUPS * GM_CHUNK)
        gmb2d = jnp.repeat(gm_b[l].T, w // GM_GROUPS, axis=1)
        dw = jnp.pad(conv_dw[l], ((0, 1), (0, 0)))
        bg, bp, bc, k, ki, qt, vt, qit, wit = _front(
            xf, seq, row(mix_pre_g[l]), w1, w2t, row(gm_v_g[l]), ws_cat, gmb2d,
            _block_diag(pool_w[l]).astype(BF16), row(pool_scale[l]), dw, row(conv_b[l]),
            row(conv_ln_g[l]), row(conv_ln_b[l]), tm)
        ba = _attention(tab, qt, qit, wit, k.reshape(batch, seq, w),
                        ki.reshape(batch, seq, LANES), vt)
        xf = _back(xf, bg, bp, ba.reshape(n, w), bc, row(mix_pre_g[l]), row(mix_post_g[l]),
                   wgate, w_branch[l].astype(BF16), w_out[l].astype(BF16))

        xf = _ffn(xf, row(ffn2_pre_g[l]), row(ffn2_post_g[l]),
                  *_split_ffn_weights(ffn2_w_gu[l], ffn2_w_down[l]))
    return xf.reshape(batch, seq, d)
```

```python
import functools
import math

import jax
import jax.numpy as jnp
from jax import lax
from jax.experimental import pallas as pl
from jax.experimental.pallas import tpu as pltpu

F32 = jnp.float32
BF16 = jnp.bfloat16

EPS = 1e-6
BRANCH_W = 256
N_BRANCH = 4
GM_GROUPS = 4
GM_CHUNK = 128
POOL_WINDOWS = (2, 4, 8, 16)
ATT_HEADS = 4
ATT_HD = 64
IDX_HEADS = 4
IDX_HD = 32
TOPK_MAX = 256
REL_BUCKETS = 32
REL_MAX_DIST = 128
CONV_K = 31

VMEM_LIMIT_BYTES = 56 * 1024 * 1024
LANES = 128
SUBLANES = 8

FFN_CHUNK = 256
FFN_BLOCK = 1024
FFN_ROWS = 512
CONV_HALO = 32
POOL_HALO = 16
CONV_ROWS = 64
ATT_BLOCK = 256
ATTEND_UNROLL = 4
COARSE_STEPS = 10
BISECT_STEPS = 8
BF16_ROWS = 16
BF16_INTERVAL = 2.0 ** -6
TINY = 1e-30
LOG2E = math.log2(math.e)
Q_SCALE = ATT_HD ** -0.5 * LOG2E
NEG = -1e30
M_INIT = -1e29


def _rmsnorm(x, g):
    return x * lax.rsqrt(jnp.mean(x * x, axis=-1, keepdims=True) + EPS) * g


def _sigmoid(x):
    return 1.0 / (1.0 + jnp.exp(-x))


def _dot(a, b):
    return jnp.dot(a, b, preferred_element_type=F32)


def _fold8(x, op):
    return _fold(x, op, SUBLANES)


def _fold(x, op, rows):
    parts = [x[j * rows:(j + 1) * rows] for j in range(x.shape[0] // rows)]
    while len(parts) > 1:
        nxt = [op(parts[j], parts[j + 1]) for j in range(0, len(parts) - 1, 2)]
        if len(parts) % 2:
            nxt.append(parts[-1])
        parts = nxt
    return parts[0]


def _ffn_body(x_ref, pre_ref, post_ref, wgu_ref, wd_ref, o_ref, xn_ref, acc_ref):
    f = wd_ref.shape[0]
    tm = x_ref.shape[0]
    halves = [slice(r, r + FFN_ROWS) for r in range(0, tm, FFN_ROWS)]
    for rows in halves:
        xn_ref[rows, :] = _rmsnorm(x_ref[rows, :], pre_ref[...]).astype(BF16)
    for c in range(0, f, FFN_CHUNK):
        for rows in halves:
            xn = xn_ref[rows, :]
            a = _dot(xn, wgu_ref[:, c:c + FFN_CHUNK])
            b = _dot(xn, wgu_ref[:, f + c:f + c + FFN_CHUNK])
            hm = (a * _sigmoid(a) * b).astype(BF16)
            down = _dot(hm, wd_ref[c:c + FFN_CHUNK, :])
            acc_ref[rows, :] = down if c == 0 else acc_ref[rows, :] + down
    for rows in halves:
        o_ref[rows, :] = x_ref[rows, :] + 0.5 * _rmsnorm(acc_ref[rows, :], post_ref[...])


def _ffn(x, pre_g, post_g, wgu, wd):
    n, d = x.shape
    f = wd.shape[0]
    tm = min(FFN_BLOCK, n)
    assert f % FFN_CHUNK == 0 and wgu.shape == (d, 2 * f) and tm % FFN_ROWS == 0
    full = lambda shape: pl.BlockSpec(shape, lambda i: (0,) * len(shape))
    resident = lambda shape: pl.BlockSpec(shape, lambda i: (0,) * len(shape),
                                          pipeline_mode=pl.Buffered(1))
    return pl.pallas_call(
        _ffn_body,
        grid=(n // tm,),
        in_specs=[
            pl.BlockSpec((tm, d), lambda i: (i, 0)),
            full((1, d)), full((1, d)), resident((d, 2 * f)), resident((f, d)),
        ],
        out_specs=pl.BlockSpec((tm, d), lambda i: (i, 0)),
        out_shape=jax.ShapeDtypeStruct((n, d), F32),
        scratch_shapes=[pltpu.VMEM((tm, d), BF16), pltpu.VMEM((tm, d), F32)],
        compiler_params=pltpu.CompilerParams(
            dimension_semantics=("arbitrary",), vmem_limit_bytes=VMEM_LIMIT_BYTES),
        name="ffn",
    )(x, pre_g, post_g, wgu, wd)


_UV0, _PZ0, _K0, _CZ0, _KI0, _W1_COLS = 0, 512, 768, 1024, 1536, 1664
_QT0, _VT0, _QIT0, _WIT0, _W2_ROWS = 0, 256, 512, 640, 656


def _front_body(blocks_per_seq,
                x_ref, pre_ref, w1_ref, w2t_ref, gmvg_ref, ws_ref, gmb_ref, poolw_ref,
                pscale_ref, dw_ref, cb_ref, lng_ref, lnb_ref,
                bg_ref, bp_ref, bc_ref, k_ref, ki_ref, qt_ref, vt_ref, qit_ref, wit_ref,
                hbuf, pbuf):
    tm = x_ref.shape[0]
    w = BRANCH_W
    j = pl.program_id(0) % blocks_per_seq

    @pl.when(j == 0)
    def _():
        hbuf[0:CONV_HALO, :] = jnp.zeros((CONV_HALO, w), F32)
        hbuf[CONV_HALO + tm:CONV_HALO + tm + SUBLANES, :] = jnp.zeros((SUBLANES, w), F32)
        pbuf[0:POOL_HALO, :] = jnp.zeros((POOL_HALO, w), F32)

    xn = _rmsnorm(x_ref[...], pre_ref[...]).astype(BF16)

    k_ref[...] = _dot(xn, w1_ref[:, _K0:_K0 + w]).astype(BF16)
    ki_ref[...] = _dot(xn, w1_ref[:, _KI0:_KI0 + LANES]).astype(BF16)
    zt = lax.dot_general(w2t_ref[...], xn, (((1,), (1,)), ((), ())),
                         preferred_element_type=F32)
    for cc in range(tm // ATT_BLOCK):
        sl = slice(cc * ATT_BLOCK, (cc + 1) * ATT_BLOCK)
        qt_ref[0, cc] = (zt[_QT0:_QT0 + w, sl] * Q_SCALE).astype(BF16)
        vt_ref[0, cc] = zt[_VT0:_VT0 + w, sl].astype(BF16)
        qit_ref[0, cc] = zt[_QIT0:_QIT0 + LANES, sl].astype(BF16)
        wit_ref[0, cc] = zt[_WIT0:_WIT0 + SUBLANES, sl]

    uv = _dot(xn, w1_ref[:, _UV0:_UV0 + 2 * w])
    guv = uv * (0.5 * (1.0 + jnp.tanh(math.sqrt(2.0 / math.pi) * (uv + 0.044715 * (uv ** 3)))))
    u = guv[:, 0:w]
    vv = _rmsnorm(guv[:, w:2 * w], gmvg_ref[...])
    row = lax.broadcasted_iota(jnp.int32, (GM_CHUNK, GM_GROUPS * GM_CHUNK), 0)
    col = lax.broadcasted_iota(jnp.int32, (GM_CHUNK, GM_GROUPS * GM_CHUNK), 1)
    wsm = jnp.where((col % GM_CHUNK) <= row, ws_ref[...], 0.0).astype(BF16)
    lane_group = lax.broadcasted_iota(jnp.int32, (GM_CHUNK, w), 1) // (w // GM_GROUPS)
    for c in range(tm // GM_CHUNK):
        rs = slice(c * GM_CHUNK, (c + 1) * GM_CHUNK)
        vc = vv[rs, :]
        stacked = jnp.concatenate(
            [jnp.where(lane_group == g, vc, 0.0) for g in range(GM_GROUPS)], axis=0).astype(BF16)
        mixed = _dot(wsm, stacked) + gmb_ref[...]
        bg_ref[rs, :] = (u[rs, :] * mixed).astype(BF16)

    p = _dot(xn, w1_ref[:, _PZ0:_PZ0 + w])
    pbuf[POOL_HALO:POOL_HALO + tm, :] = p
    lane_win = lax.broadcasted_iota(jnp.int32, (tm, w), 1) // (w // len(POOL_WINDOWS))
    pos1 = (j * tm + lax.broadcasted_iota(jnp.int32, (tm, w), 0) + 1).astype(F32)
    run = p
    win = jnp.zeros((tm, w), F32)
    cnt = jnp.zeros((tm, w), F32)
    shift = 1
    for g, wlen in enumerate(POOL_WINDOWS):
        while shift < wlen:
            run = run + pbuf[POOL_HALO - shift:POOL_HALO - shift + tm, :]
            shift += 1
        win = jnp.where(lane_win == g, run, win)
        cnt = jnp.where(lane_win == g, jnp.minimum(pos1, float(wlen)), cnt)
    dpool = (win / cnt - p).astype(BF16)
    bp_ref[...] = (_dot(dpool, poolw_ref[...]) * pscale_ref[...]).astype(BF16)
    pbuf[0:POOL_HALO, :] = pbuf[tm:tm + POOL_HALO, :]

    cz = _dot(xn, w1_ref[:, _CZ0:_CZ0 + 2 * w])
    hbuf[CONV_HALO:CONV_HALO + tm, :] = cz[:, 0:w] * _sigmoid(cz[:, w:2 * w])
    lead = CONV_HALO - (CONV_K - 1)
    for r in range(tm // CONV_ROWS):
        acc = None
        for shift in range(SUBLANES):
            part = None
            for t in range(CONV_K):
                if (lead + t) % SUBLANES == shift:
                    start = r * CONV_ROWS + lead + t - shift
                    term = hbuf[start:start + CONV_ROWS + SUBLANES, :] * dw_ref[t:t + 1, :]
                    part = term if part is None else part + term
            part = part[shift:shift + CONV_ROWS, :]
            acc = part if acc is None else acc + part
        hc = acc + cb_ref[...]
        mu = jnp.mean(hc, axis=-1, keepdims=True)
        xc = hc - mu
        yn = xc * lax.rsqrt(jnp.mean(xc * xc, axis=-1, keepdims=True) + EPS)
        yn = yn * lng_ref[...] + lnb_ref[...]
        bc_ref[r * CONV_ROWS:(r + 1) * CONV_ROWS, :] = (yn * _sigmoid(yn)).astype(BF16)
    hbuf[0:CONV_HALO, :] = hbuf[tm:tm + CONV_HALO, :]


def _front(x, seq, pre_g, w1, w2t, gmvg, ws_cat, gmb2d, poolw, pscale, dw, cb, lng, lnb, tm):
    n, d = x.shape
    batch = n // seq
    bps = seq // tm
    w = BRANCH_W
    nch = seq // ATT_BLOCK
    cpb = tm // ATT_BLOCK
    full = lambda a: pl.BlockSpec(a.shape, lambda i: (0,) * a.ndim)
    tok = lambda width: pl.BlockSpec((tm, width), lambda i: (i, 0))
    chunked = lambda rows: pl.BlockSpec((1, cpb, rows, ATT_BLOCK),
                                        lambda i: (i // bps, i % bps, 0, 0))
    params = (pre_g, w1, w2t, gmvg, ws_cat, gmb2d, poolw, pscale, dw, cb, lng, lnb)
    out_shape = (
        jax.ShapeDtypeStruct((n, w), BF16),
        jax.ShapeDtypeStruct((n, w), BF16),
        jax.ShapeDtypeStruct((n, w), BF16),
        jax.ShapeDtypeStruct((n, w), BF16),
        jax.ShapeDtypeStruct((n, LANES), BF16),
        jax.ShapeDtypeStruct((batch, nch, w, ATT_BLOCK), BF16),
        jax.ShapeDtypeStruct((batch, nch, w, ATT_BLOCK), BF16),
        jax.ShapeDtypeStruct((batch, nch, LANES, ATT_BLOCK), BF16),
        jax.ShapeDtypeStruct((batch, nch, SUBLANES, ATT_BLOCK), F32),
    )
    out_specs = (tok(w), tok(w), tok(w), tok(w), tok(LANES),
                 chunked(w), chunked(w), chunked(LANES), chunked(SUBLANES))
    return pl.pallas_call(
        functools.partial(_front_body, bps),
        grid=(n // tm,),
        in_specs=[tok(d)] + [full(a) for a in params],
        out_specs=out_specs,
        out_shape=out_shape,
        scratch_shapes=[pltpu.VMEM((tm + CONV_HALO + SUBLANES, w), F32),
                        pltpu.VMEM((tm + POOL_HALO, w), F32)],
        compiler_params=pltpu.CompilerParams(
            dimension_semantics=("arbitrary",), vmem_limit_bytes=VMEM_LIMIT_BYTES),
        name="mixer_front",
    )(x, *params)


def _attn_body(topk,
               tab_ref, qt_ref, qit_ref, wit_ref, k_ref, ki_ref, vt_ref, o_ref,
               s_ref, s16_ref, bd_ref, bp_ref, tri_ref, qm_ref, qim_ref, acc_ref, lg0_ref, lg1_ref):
    qb = ATT_BLOCK
    nch = vt_ref.shape[1]
    b = pl.program_id(0)
    i = pl.program_id(1)
    key = lax.broadcasted_iota(jnp.int32, (qb, qb), 0)
    qry = lax.broadcasted_iota(jnp.int32, (qb, qb), 1)

    @pl.when((b == 0) & (i == 0))
    def _():
        tri_ref[...] = jnp.where(qry <= key, 1.0, 0.0).astype(BF16)
        max_exact = REL_BUCKETS // 2
        for ref, off in ((bd_ref, 0), (bp_ref, qb)):
            n = jnp.maximum(qry - key + off, 0)
            large = max_exact + (
                jnp.log(jnp.maximum(n, 1).astype(F32) / max_exact)
                / math.log(REL_MAX_DIST / max_exact) * (REL_BUCKETS - max_exact)).astype(jnp.int32)
            bucket = jnp.where(n < max_exact, n, jnp.minimum(large, REL_BUCKETS - 1))
            for h in range(ATT_HEADS):
                bias = jnp.zeros((qb, qb), F32)
                for k in range(REL_BUCKETS):
                    bias = jnp.where(bucket == k, tab_ref[k * ATT_HEADS + h], bias)
                ref[h] = (bias - tab_ref[(REL_BUCKETS - 1) * ATT_HEADS + h]) * LOG2E

    qt = qt_ref[0, 0]
    qit = qit_ref[0, 0]
    q_head = lax.broadcasted_iota(jnp.int32, qt.shape, 0) // ATT_HD
    qi_head = lax.broadcasted_iota(jnp.int32, qit.shape, 0) // IDX_HD
    for h in range(ATT_HEADS):
        qm_ref[h] = jnp.where(q_head == h, qt, jnp.zeros_like(qt))
    for h in range(IDX_HEADS):
        qim_ref[h] = jnp.where(qi_head == h, qit, jnp.zeros_like(qit))
    wv = wit_ref[0, 0] * ((IDX_HEADS ** -0.5) * (IDX_HD ** -0.5))

    t_pos = i * qb + qry
    n_pairs = (i + 2) // 2

    def keys_of(c):
        return pl.ds(pl.multiple_of(c * qb, qb), qb)

    def score_chunk(c, masked):
        kic = ki_ref[0, keys_of(jnp.minimum(c, nch - 1)), :]
        sc = jnp.zeros((qb, qb), F32)
        for h in range(IDX_HEADS):
            sc = sc + wv[h:h + 1, :] * jnp.maximum(_dot(kic, qim_ref[h]), 0.0)
        if masked:
            valid = (c * qb + key) <= t_pos
            top = _fold(jnp.where(valid, sc, -jnp.inf), jnp.maximum, BF16_ROWS)
            bot = _fold(jnp.where(valid, sc, jnp.inf), jnp.minimum, BF16_ROWS)
            sc = jnp.where(valid, sc, -jnp.inf)
        s_ref[c] = sc
        sc16 = sc.astype(BF16)
        s16_ref[c] = sc16
        if not masked:
            top = _fold(sc16, jnp.maximum, BF16_ROWS).astype(F32)
            bot = _fold(sc16, jnp.minimum, BF16_ROWS).astype(F32)
        return top, bot

    def score_chunks(first, count, carry, masked):
        top, bot = carry
        for j in range(count):
            top_j, bot_j = score_chunk(first + j, masked)
            top, bot = jnp.maximum(top, top_j), jnp.minimum(bot, bot_j)
        return top, bot

    quads = (n_pairs - 1) // 2
    carry = lax.fori_loop(
        0, quads, lambda g, cr: score_chunks(4 * g, 4, cr, False),
        (jnp.full((BF16_ROWS, qb), -jnp.inf, F32), jnp.full((BF16_ROWS, qb), jnp.inf, F32)))
    carry = lax.fori_loop(
        2 * quads, n_pairs - 1, lambda p, cr: score_chunks(2 * p, 2, cr, False), carry)
    top, bot = score_chunks(2 * (n_pairs - 1), 2, carry, True)
    rmin = jnp.min(bot, axis=0, keepdims=True)
    cap16 = jnp.max(top, axis=0, keepdims=True).astype(BF16).astype(F32)
    rmax = cap16 + (jnp.abs(cap16) * BF16_INTERVAL + TINY)

    n_valid = (i * qb + lax.broadcasted_iota(jnp.int32, (1, qb), 1) + 1).astype(F32)
    kp = jnp.minimum(float(topk), n_valid)

    def count_ge(thr):
        def body(p, acc):
            g0 = _fold8(jnp.where(s_ref[2 * p] >= thr, 1.0, 0.0), jnp.add)
            g1 = _fold8(jnp.where(s_ref[2 * p + 1] >= thr, 1.0, 0.0), jnp.add)
            return acc + (g0 + g1)
        acc = lax.fori_loop(0, n_pairs, body, jnp.zeros((SUBLANES, qb), F32))
        return jnp.sum(acc, axis=0, keepdims=True)

    one16 = jnp.ones((), BF16)
    zero16 = jnp.zeros((), BF16)

    def count_ge16(thr16):
        def body(p, acc):
            g0 = _fold(jnp.where(s16_ref[2 * p] >= thr16, one16, zero16), jnp.add, BF16_ROWS)
            g1 = _fold(jnp.where(s16_ref[2 * p + 1] >= thr16, one16, zero16), jnp.add, BF16_ROWS)
            return acc + (g0 + g1).astype(F32)
        acc = lax.fori_loop(0, n_pairs, body, jnp.zeros((BF16_ROWS, qb), F32))
        return jnp.sum(acc, axis=0, keepdims=True)

    def coarse(_, carry):
        lo, hi = carry
        mid16 = (0.5 * lo + 0.5 * jnp.minimum(hi, cap16)).astype(BF16)
        feas = count_ge16(mid16) >= kp
        mid = mid16.astype(F32)
        return jnp.where(feas, mid, lo), jnp.where(feas, hi, mid)

    lo16, hi = lax.fori_loop(0, COARSE_STEPS, coarse,
                             (rmin.astype(BF16).astype(F32), jnp.full((1, qb), jnp.inf, F32)))
    lo = lo16 - (jnp.abs(lo16) * BF16_INTERVAL + TINY)
    chi = count_ge(hi)

    def bisect(_, carry):
        lo, hi, chi = carry
        mid = 0.5 * lo + 0.5 * jnp.minimum(hi, rmax)
        cnt = count_ge(mid)
        feas = cnt >= kp
        return (jnp.where(feas, mid, lo), jnp.where(feas, hi, mid), jnp.where(feas, chi, cnt))

    lo, hi, chi = lax.fori_loop(0, BISECT_STEPS, bisect, (lo, hi, chi))

    def max_below(thr):
        def body(p, acc):
            x0 = s_ref[2 * p]
            x1 = s_ref[2 * p + 1]
            m0 = _fold8(jnp.where(x0 < thr, x0, -jnp.inf), jnp.maximum)
            m1 = _fold8(jnp.where(x1 < thr, x1, -jnp.inf), jnp.maximum)
            return jnp.maximum(acc, jnp.maximum(m0, m1))
        acc = lax.fori_loop(0, n_pairs, body, jnp.full((SUBLANES, qb), -jnp.inf, F32))
        return jnp.max(acc, axis=0, keepdims=True)

    def finish_cond(state):
        return state[4] > 0.0

    def finish_body(state):
        hi, chi, tau, done, _ = state
        m = max_below(hi)
        cnt = count_ge(m)
        feas = cnt >= kp
        active = done < 0.5
        tau = jnp.where(active & feas, m, tau)
        hi = jnp.where(active & (~feas), m, hi)
        chi = jnp.where(active & (~feas), cnt, chi)
        done = jnp.where(active & feas, 1.0, done)
        return hi, chi, tau, done, jnp.max(1.0 - done)

    hi, chi, tau, _, _ = lax.while_loop(
        finish_cond, finish_body,
        (hi, chi, lo, jnp.zeros((1, qb), F32), jnp.float32(1.0)))
    need = kp - chi

    acc_ref[...] = jnp.zeros_like(acc_ref)
    lg1_ref[...] = jnp.full(lg1_ref.shape, NEG, F32)
    heads = range(ATT_HEADS)

    def stage_a(c, bias, live, buf, need_left):
        x = s_ref[c]
        eq = jnp.where(x == tau, 1.0, 0.0)
        if live is not None:
            eq = jnp.where(live, eq, 0.0)
        prefix = _dot(tri_ref[...], eq.astype(BF16))
        maskadd = jnp.where(x >= jnp.where(prefix <= need_left, tau, hi), 0.0, NEG)
        if live is not None:
            maskadd = jnp.where(live, maskadd, NEG)
        kc = k_ref[0, keys_of(c), :]
        mxs = []
        for h in heads:
            lg = _dot(kc, qm_ref[h]) + maskadd
            if bias is not None:
                lg = lg + bias[h]
            buf[h] = lg
            mxs.append(jnp.max(_fold8(lg, jnp.maximum), axis=0, keepdims=True))
        return tuple(mxs), need_left - prefix[qb - 1:qb, :]

    ones_rows = jnp.ones((2 * SUBLANES, qb), BF16)

    def stage_b(c, buf, mxs, ms, ls):
        vtc = vt_ref[0, c]
        new_ms = [jnp.maximum(ms[h], mxs[h]) for h in heads]
        alphas = [jnp.exp2(ms[h] - new_ms[h]) for h in heads]
        pvs = [_dot(jnp.concatenate([vtc[h * ATT_HD:(h + 1) * ATT_HD, :], ones_rows], axis=0),
                    jnp.exp2(buf[h] - new_ms[h]).astype(BF16)) for h in heads]
        new_ls = [alphas[h] * ls[h] + pvs[h][ATT_HD:ATT_HD + 1, :] for h in heads]
        pv = jnp.concatenate([pvs[h][0:ATT_HD, :] for h in heads], axis=0)
        alpha_rows = jnp.concatenate(
            [jnp.broadcast_to(alphas[h], (ATT_HD, qb)) for h in heads], axis=0)
        acc_ref[...] = acc_ref[...] * alpha_rows + pv
        return tuple(new_ms), tuple(new_ls)

    lg_refs = (lg0_ref, lg1_ref)

    def run_stages(stages, carry):
        ms, ls, need_left, pend_mx, pend_c = carry
        for k, (c, bias, live) in enumerate(stages):
            mx, need_left = stage_a(c, bias, live, lg_refs[k % 2], need_left)
            ms, ls = stage_b(pend_c, lg_refs[1 - k % 2], pend_mx, ms, ls)
            pend_mx, pend_c = mx, c
        return ms, ls, need_left, pend_mx, pend_c

    n_far = jnp.maximum(i - 1, 0)
    m_init = tuple(jnp.full((1, qb), M_INIT, F32) for _ in heads)
    carry = (m_init, tuple(jnp.zeros((1, qb), F32) for _ in heads), need, m_init, jnp.int32(0))
    carry = lax.fori_loop(
        0, n_far // ATTEND_UNROLL,
        lambda g, cr: run_stages([(ATTEND_UNROLL * g + j, None, None)
                                  for j in range(ATTEND_UNROLL)], cr),
        carry)
    far_done = (n_far // ATTEND_UNROLL) * ATTEND_UNROLL

    def tail(n_left):
        def run(cr):
            stages = [(far_done + j, None, None) for j in range(n_left)]
            stages += [(jnp.maximum(i - 1, 0), bp_ref, i >= 1), (i, bd_ref, None)]
            ms, ls, _, pend_mx, pend_c = run_stages(stages, cr)
            return stage_b(pend_c, lg_refs[1 - len(stages) % 2], pend_mx, ms, ls)[1]
        return run

    ls = lax.switch(n_far - far_done, [tail(r) for r in range(ATTEND_UNROLL)], carry)

    for h in range(ATT_HEADS):
        rows = slice(h * ATT_HD, (h + 1) * ATT_HD)
        acc_ref[rows, :] = acc_ref[rows, :] / ls[h]
    o_ref[0] = acc_ref[...].T.astype(BF16)


def _attention(tab, qt, qit, wit, k, ki, vt):
    batch, nch, w, qb = qt.shape
    seq = nch * qb
    topk = min(TOPK_MAX, seq // 4)
    per_block = lambda rows: pl.BlockSpec((1, 1, rows, qb), lambda b, i: (b, i, 0, 0))
    return pl.pallas_call(
        functools.partial(_attn_body, topk),
        grid=(batch, nch),
        in_specs=[
            pl.BlockSpec(memory_space=pltpu.SMEM),
            per_block(w), per_block(LANES), per_block(SUBLANES),
            pl.BlockSpec((1, seq, w), lambda b, i: (b, 0, 0)),
            pl.BlockSpec((1, seq, LANES), lambda b, i: (b, 0, 0)),
            pl.BlockSpec((1, nch, w, qb), lambda b, i: (b, 0, 0, 0)),
        ],
        out_specs=pl.BlockSpec((1, qb, w), lambda b, i: (b, i, 0)),
        out_shape=jax.ShapeDtypeStruct((batch, seq, w), BF16),
        scratch_shapes=[
            pltpu.VMEM((nch + 1, qb, qb), F32),
            pltpu.VMEM((nch + 1, qb, qb), BF16),
            pltpu.VMEM((ATT_HEADS, qb, qb), F32),
            pltpu.VMEM((ATT_HEADS, qb, qb), F32),
            pltpu.VMEM((qb, qb), BF16),
            pltpu.VMEM((ATT_HEADS, w, qb), BF16),
            pltpu.VMEM((IDX_HEADS, LANES, qb), BF16),
            pltpu.VMEM((w, qb), F32),
            pltpu.VMEM((ATT_HEADS, qb, qb), F32),
            pltpu.VMEM((ATT_HEADS, qb, qb), F32),
        ],
        compiler_params=pltpu.CompilerParams(
            dimension_semantics=("arbitrary", "arbitrary"), vmem_limit_bytes=VMEM_LIMIT_BYTES),
        name="sparse_attention",
    )(tab, qt, qit, wit, k, ki, vt)


def _back_body(x_ref, bg_ref, bp_ref, ba_ref, bc_ref, pre_ref, post_ref, wgate_ref, wbr_ref,
               wout_ref, o_ref):
    for r in range(0, x_ref.shape[0], FFN_ROWS):
        rows = slice(r, r + FFN_ROWS)
        x = x_ref[rows, :]
        xn = _rmsnorm(x, pre_ref[...]).astype(BF16)
        y = None
        for n, br in enumerate((bg_ref, bp_ref, ba_ref, bc_ref)):
            term = _sigmoid(_dot(xn, wgate_ref[n])) * _dot(br[rows, :], wbr_ref[n])
            y = term if y is None else y + term
        h = _dot(y.astype(BF16), wout_ref[...])
        o_ref[rows, :] = x + _rmsnorm(h, post_ref[...])


def _back(x, bg, bp, ba, bc, pre_g, post_g, wgate, wbr, wout):
    n, d = x.shape
    w = BRANCH_W
    tm = min(FFN_BLOCK, n)
    assert tm % FFN_ROWS == 0
    full = lambda a: pl.BlockSpec(a.shape, lambda i: (0,) * a.ndim)
    resident = lambda a: pl.BlockSpec(a.shape, lambda i: (0,) * a.ndim,
                                      pipeline_mode=pl.Buffered(1))
    tok = lambda width: pl.BlockSpec((tm, width), lambda i: (i, 0))
    params = (pre_g, post_g, wgate, wbr, wout)
    return pl.pallas_call(
        _back_body,
        grid=(n // tm,),
        in_specs=([tok(d), tok(w), tok(w), tok(w), tok(w), full(pre_g), full(post_g)]
                  + [resident(a) for a in (wgate, wbr, wout)]),
        out_specs=tok(d),
        out_shape=jax.ShapeDtypeStruct((n, d), F32),
        compiler_params=pltpu.CompilerParams(
            dimension_semantics=("arbitrary",), vmem_limit_bytes=VMEM_LIMIT_BYTES),
        name="mixer_back",
    )(x, bg, bp, ba, bc, *params)


def _split_ffn_weights(w_gu, w_down):
    return w_gu.astype(BF16), w_down.astype(BF16)


def _pack_front_weights(w_in):
    d = w_in.shape[0]
    w = BRANCH_W
    o = 0
    uv = w_in[:, o:o + 2 * w]; o += 2 * w
    pz = w_in[:, o:o + w]; o += w
    qz = w_in[:, o:o + w]; o += w
    kz = w_in[:, o:o + w]; o += w
    vz = w_in[:, o:o + w]; o += w
    qi = w_in[:, o:o + IDX_HEADS * IDX_HD]; o += IDX_HEADS * IDX_HD
    ki = w_in[:, o:o + IDX_HD]; o += IDX_HD
    wi = w_in[:, o:o + IDX_HEADS]; o += IDX_HEADS
    cz = w_in[:, o:o + 2 * w]; o += 2 * w
    gz = w_in[:, o:]
    w1 = jnp.concatenate([uv, pz, kz, cz] + [ki] * (LANES // IDX_HD), axis=1).astype(BF16)
    wi_pad = jnp.pad(wi, ((0, 0), (0, _W2_ROWS - _WIT0 - IDX_HEADS)))
    w2t = jnp.concatenate([qz, vz, qi, wi_pad], axis=1).T.astype(BF16)
    wgate = gz.reshape(d, N_BRANCH, d).transpose(1, 0, 2).astype(BF16)
    return w1, w2t, wgate


def _block_diag(pw):
    g, c, _ = pw.shape
    out = jnp.zeros((g * c, g * c), pw.dtype)
    for k in range(g):
        out = out.at[k * c:(k + 1) * c, k * c:(k + 1) * c].set(pw[k])
    return out


def kernel(x, ffn1_pre_g, ffn1_post_g, ffn1_w_gu, ffn1_w_down, mix_pre_g, mix_post_g, w_in,
           gm_v_g, gm_ws, gm_b, pool_w, pool_scale, conv_dw, conv_b, conv_ln_g, conv_ln_b,
           w_branch, w_out, ffn2_pre_g, ffn2_post_g, ffn2_w_gu, ffn2_w_down, rel_bias):
    batch, seq, d = x.shape
    depth = w_in.shape[0]
    n = batch * seq
    tm = min(512, seq)
    w = BRANCH_W
    row = lambda a: a.reshape(1, -1)
    tab = rel_bias.reshape(-1)

    xf = x.reshape(n, d)
    for l in range(depth):
        xf = _ffn(xf, row(ffn1_pre_g[l]), row(ffn1_post_g[l]),
                  *_split_ffn_weights(ffn1_w_gu[l], ffn1_w_down[l]))

        w1, w2t, wgate = _pack_front_weights(w_in[l])
        ws_cat = gm_ws[l].transpose(1, 0, 2).reshape(GM_CHUNK, GM_GROUPS * GM_CHUNK)
        gmb2d = jnp.repeat(gm_b[l].T, w // GM_GROUPS, axis=1)
        dw = jnp.pad(conv_dw[l], ((0, 1), (0, 0)))
        bg, bp, bc, k, ki, qt, vt, qit, wit = _front(
            xf, seq, row(mix_pre_g[l]), w1, w2t, row(gm_v_g[l]), ws_cat, gmb2d,
            _block_diag(pool_w[l]).astype(BF16), row(pool_scale[l]), dw, row(conv_b[l]),
            row(conv_ln_g[l]), row(conv_ln_b[l]), tm)
        ba = _attention(tab, qt, qit, wit, k.reshape(batch, seq, w),
                        ki.reshape(batch, seq, LANES), vt)
        xf = _back(xf, bg, bp, ba.reshape(n, w), bc, row(mix_pre_g[l]), row(mix_post_g[l]),
                   wgate, w_branch[l].astype(BF16), w_out[l].astype(BF16))

        xf = _ffn(xf, row(ffn2_pre_g[l]), row(ffn2_post_g[l]),
                  *_split_ffn_weights(ffn2_w_gu[l], ffn2_w_down[l]))
    return xf.reshape(batch, seq, d)
```

```python
import functools
import math

import jax
import jax.numpy as jnp
from jax import lax
from jax.experimental import pallas as pl
from jax.experimental.pallas import tpu as pltpu

F32 = jnp.float32
BF16 = jnp.bfloat16

EPS = 1e-6
BRANCH_W = 256
N_BRANCH = 4
GM_GROUPS = 4
GM_CHUNK = 128
POOL_WINDOWS = (2, 4, 8, 16)
ATT_HEADS = 4
ATT_HD = 64
IDX_HEADS = 4
IDX_HD = 32
TOPK_MAX = 256
REL_BUCKETS = 32
REL_MAX_DIST = 128
CONV_K = 31

VMEM_LIMIT_BYTES = 56 * 1024 * 1024
LANES = 128
SUBLANES = 8

FFN_CHUNK = 256
FFN_BLOCK = 1024
FFN_ROWS = 512
FRONT_BLOCK = 1024
FRONT_ROWS = 512
CONV_HALO = 32
POOL_HALO = 16
CONV_ROWS = 64
ATT_BLOCK = 256
ATTEND_UNROLL = 4
COARSE_STEPS = 10
BISECT_STEPS = 8
BF16_ROWS = 16
BF16_INTERVAL = 2.0 ** -6
TINY = 1e-30
LOG2E = math.log2(math.e)
Q_SCALE = ATT_HD ** -0.5 * LOG2E
NEG = -1e30
M_INIT = -1e29


def _rmsnorm(x, g):
    return x * lax.rsqrt(jnp.mean(x * x, axis=-1, keepdims=True) + EPS) * g


def _sigmoid(x):
    return 1.0 / (1.0 + jnp.exp(-x))


def _dot(a, b):
    return jnp.dot(a, b, preferred_element_type=F32)


def _fold8(x, op):
    return _fold(x, op, SUBLANES)


def _fold(x, op, rows):
    parts = [x[j * rows:(j + 1) * rows] for j in range(x.shape[0] // rows)]
    while len(parts) > 1:
        nxt = [op(parts[j], parts[j + 1]) for j in range(0, len(parts) - 1, 2)]
        if len(parts) % 2:
            nxt.append(parts[-1])
        parts = nxt
    return parts[0]


def _ffn_body(x_ref, pre_ref, post_ref, wgu_ref, wd_ref, o_ref, xn_ref, acc_ref):
    f = wd_ref.shape[0]
    tm = x_ref.shape[0]
    halves = [slice(r, r + FFN_ROWS) for r in range(0, tm, FFN_ROWS)]
    for rows in halves:
        xn_ref[rows, :] = _rmsnorm(x_ref[rows, :], pre_ref[...]).astype(BF16)
    for c in range(0, f, FFN_CHUNK):
        for rows in halves:
            xn = xn_ref[rows, :]
            a = _dot(xn, wgu_ref[:, c:c + FFN_CHUNK])
            b = _dot(xn, wgu_ref[:, f + c:f + c + FFN_CHUNK])
            hm = (a * _sigmoid(a) * b).astype(BF16)
            down = _dot(hm, wd_ref[c:c + FFN_CHUNK, :])
            acc_ref[rows, :] = down if c == 0 else acc_ref[rows, :] + down
    for rows in halves:
        o_ref[rows, :] = x_ref[rows, :] + 0.5 * _rmsnorm(acc_ref[rows, :], post_ref[...])


def _ffn(x, pre_g, post_g, wgu, wd):
    n, d = x.shape
    f = wd.shape[0]
    tm = min(FFN_BLOCK, n)
    assert f % FFN_CHUNK == 0 and wgu.shape == (d, 2 * f) and tm % FFN_ROWS == 0
    full = lambda shape: pl.BlockSpec(shape, lambda i: (0,) * len(shape))
    resident = lambda shape: pl.BlockSpec(shape, lambda i: (0,) * len(shape),
                                          pipeline_mode=pl.Buffered(1))
    return pl.pallas_call(
        _ffn_body,
        grid=(n // tm,),
        in_specs=[
            pl.BlockSpec((tm, d), lambda i: (i, 0)),
            full((1, d)), full((1, d)), resident((d, 2 * f)), resident((f, d)),
        ],
        out_specs=pl.BlockSpec((tm, d), lambda i: (i, 0)),
        out_shape=jax.ShapeDtypeStruct((n, d), F32),
        scratch_shapes=[pltpu.VMEM((tm, d), BF16), pltpu.VMEM((tm, d), F32)],
        compiler_params=pltpu.CompilerParams(
            dimension_semantics=("arbitrary",), vmem_limit_bytes=VMEM_LIMIT_BYTES),
        name="ffn",
    )(x, pre_g, post_g, wgu, wd)


_UV0, _PZ0, _K0, _CZ0, _KI0, _W1_COLS = 0, 512, 768, 1024, 1536, 1664
_QT0, _VT0, _QIT0, _WIT0, _W2_ROWS = 0, 256, 512, 640, 656


def _front_body(blocks_per_seq,
                x_ref, pre_ref, w1_ref, w2t_ref, gmvg_ref, ws_ref, gmb_ref, poolw_ref,
                pscale_ref, dw_ref, cb_ref, lng_ref, lnb_ref,
                bg_ref, bp_ref, bc_ref, k_ref, ki_ref, qt_ref, vt_ref, qit_ref, wit_ref,
                hbuf, pbuf):
    tm_step = x_ref.shape[0]
    tm = min(FRONT_ROWS, tm_step)
    w = BRANCH_W
    j = pl.program_id(0) % blocks_per_seq

    @pl.when(j == 0)
    def _():
        hbuf[...] = jnp.zeros_like(hbuf)
        pbuf[0:POOL_HALO, :] = jnp.zeros((POOL_HALO, w), F32)

    row = lax.broadcasted_iota(jnp.int32, (GM_CHUNK, GM_GROUPS * GM_CHUNK), 0)
    col = lax.broadcasted_iota(jnp.int32, (GM_CHUNK, GM_GROUPS * GM_CHUNK), 1)
    wsm = jnp.where((col % GM_CHUNK) <= row, ws_ref[...], 0.0).astype(BF16)
    lane_group = lax.broadcasted_iota(jnp.int32, (GM_CHUNK, w), 1) // (w // GM_GROUPS)
    lane_win = lax.broadcasted_iota(jnp.int32, (tm, w), 1) // (w // len(POOL_WINDOWS))

    for r0 in range(0, tm_step, tm):
        rows = slice(r0, r0 + tm)
        xn = _rmsnorm(x_ref[rows, :], pre_ref[...]).astype(BF16)

        k_ref[rows, :] = _dot(xn, w1_ref[:, _K0:_K0 + w]).astype(BF16)
        ki_ref[rows, :] = _dot(xn, w1_ref[:, _KI0:_KI0 + LANES]).astype(BF16)
        zt = lax.dot_general(w2t_ref[...], xn, (((1,), (1,)), ((), ())),
                             preferred_element_type=F32)
        for cc in range(tm // ATT_BLOCK):
            sl = slice(cc * ATT_BLOCK, (cc + 1) * ATT_BLOCK)
            oc = r0 // ATT_BLOCK + cc
            qt_ref[0, oc] = (zt[_QT0:_QT0 + w, sl] * Q_SCALE).astype(BF16)
            vt_ref[0, oc] = zt[_VT0:_VT0 + w, sl].astype(BF16)
            qit_ref[0, oc] = zt[_QIT0:_QIT0 + LANES, sl].astype(BF16)
            wit_ref[0, oc] = zt[_WIT0:_WIT0 + SUBLANES, sl]

        uv = _dot(xn, w1_ref[:, _UV0:_UV0 + 2 * w])
        guv = uv * (0.5 * (1.0 + jnp.tanh(math.sqrt(2.0 / math.pi) * (uv + 0.044715 * (uv ** 3)))))
        u = guv[:, 0:w]
        vv = _rmsnorm(guv[:, w:2 * w], gmvg_ref[...])
        for c in range(tm // GM_CHUNK):
            rs = slice(c * GM_CHUNK, (c + 1) * GM_CHUNK)
            vc = vv[rs, :]
            stacked = jnp.concatenate(
                [jnp.where(lane_group == g, vc, 0.0) for g in range(GM_GROUPS)], axis=0).astype(BF16)
            mixed = _dot(wsm, stacked) + gmb_ref[...]
            bg_ref[r0 + c * GM_CHUNK:r0 + (c + 1) * GM_CHUNK, :] = (u[rs, :] * mixed).astype(BF16)

        p = _dot(xn, w1_ref[:, _PZ0:_PZ0 + w])
        pbuf[POOL_HALO + r0:POOL_HALO + r0 + tm, :] = p
        pos1 = (j * tm_step + r0 + lax.broadcasted_iota(jnp.int32, (tm, w), 0) + 1).astype(F32)
        run = p
        win = jnp.zeros((tm, w), F32)
        cnt = jnp.zeros((tm, w), F32)
        shift = 1
        for g, wlen in enumerate(POOL_WINDOWS):
            while shift < wlen:
                run = run + pbuf[POOL_HALO + r0 - shift:POOL_HALO + r0 - shift + tm, :]
                shift += 1
            win = jnp.where(lane_win == g, run, win)
            cnt = jnp.where(lane_win == g, jnp.minimum(pos1, float(wlen)), cnt)
        dpool = (win / cnt - p).astype(BF16)
        bp_ref[rows, :] = (_dot(dpool, poolw_ref[...]) * pscale_ref[...]).astype(BF16)

        cz = _dot(xn, w1_ref[:, _CZ0:_CZ0 + 2 * w])
        hbuf[CONV_HALO + r0:CONV_HALO + r0 + tm, :] = cz[:, 0:w] * _sigmoid(cz[:, w:2 * w])
        lead = CONV_HALO - (CONV_K - 1)
        for r in range(r0, r0 + tm, CONV_ROWS):
            acc = None
            for shift in range(SUBLANES):
                part = None
                for t in range(CONV_K):
                    if (lead + t) % SUBLANES == shift:
                        start = r + lead + t - shift
                        term = hbuf[start:start + CONV_ROWS + SUBLANES, :] * dw_ref[t:t + 1, :]
                        part = term if part is None else part + term
                part = part[shift:shift + CONV_ROWS, :]
                acc = part if acc is None else acc + part
            hc = acc + cb_ref[...]
            mu = jnp.mean(hc, axis=-1, keepdims=True)
            xc = hc - mu
            yn = xc * lax.rsqrt(jnp.mean(xc * xc, axis=-1, keepdims=True) + EPS)
            yn = yn * lng_ref[...] + lnb_ref[...]
            bc_ref[r:r + CONV_ROWS, :] = (yn * _sigmoid(yn)).astype(BF16)

    pbuf[0:POOL_HALO, :] = pbuf[tm_step:tm_step + POOL_HALO, :]
    hbuf[0:CONV_HALO, :] = hbuf[tm_step:tm_step + CONV_HALO, :]


def _front(x, seq, pre_g, w1, w2t, gmvg, ws_cat, gmb2d, poolw, pscale, dw, cb, lng, lnb, tm):
    n, d = x.shape
    batch = n // seq
    bps = seq // tm
    w = BRANCH_W
    nch = seq // ATT_BLOCK
    cpb = tm // ATT_BLOCK
    full = lambda a: pl.BlockSpec(a.shape, lambda i: (0,) * a.ndim)
    tok = lambda width: pl.BlockSpec((tm, width), lambda i: (i, 0))
    chunked = lambda rows: pl.BlockSpec((1, cpb, rows, ATT_BLOCK),
                                        lambda i: (i // bps, i % bps, 0, 0))
    params = (pre_g, w1, w2t, gmvg, ws_cat, gmb2d, poolw, pscale, dw, cb, lng, lnb)
    out_shape = (
        jax.ShapeDtypeStruct((n, w), BF16),
        jax.ShapeDtypeStruct((n, w), BF16),
        jax.ShapeDtypeStruct((n, w), BF16),
        jax.ShapeDtypeStruct((n, w), BF16),
        jax.ShapeDtypeStruct((n, LANES), BF16),
        jax.ShapeDtypeStruct((batch, nch, w, ATT_BLOCK), BF16),
        jax.ShapeDtypeStruct((batch, nch, w, ATT_BLOCK), BF16),
        jax.ShapeDtypeStruct((batch, nch, LANES, ATT_BLOCK), BF16),
        jax.ShapeDtypeStruct((batch, nch, SUBLANES, ATT_BLOCK), F32),
    )
    out_specs = (tok(w), tok(w), tok(w), tok(w), tok(LANES),
                 chunked(w), chunked(w), chunked(LANES), chunked(SUBLANES))
    return pl.pallas_call(
        functools.partial(_front_body, bps),
        grid=(n // tm,),
        in_specs=[tok(d)] + [full(a) for a in params],
        out_specs=out_specs,
        out_shape=out_shape,
        scratch_shapes=[pltpu.VMEM((tm + CONV_HALO + SUBLANES, w), F32),
                        pltpu.VMEM((tm + POOL_HALO, w), F32)],
        compiler_params=pltpu.CompilerParams(
            dimension_semantics=("arbitrary",), vmem_limit_bytes=VMEM_LIMIT_BYTES),
        name="mixer_front",
    )(x, *params)


def _attn_body(topk,
               tab_ref, qt_ref, qit_ref, wit_ref, k_ref, ki_ref, vt_ref, o_ref,
               s_ref, s16_ref, bd_ref, bp_ref, tri_ref, qm_ref, qim_ref, acc_ref, lg0_ref, lg1_ref):
    qb = ATT_BLOCK
    nch = vt_ref.shape[1]
    b = pl.program_id(0)
    i = pl.program_id(1)
    key = lax.broadcasted_iota(jnp.int32, (qb, qb), 0)
    qry = lax.broadcasted_iota(jnp.int32, (qb, qb), 1)

    @pl.when((b == 0) & (i == 0))
    def _():
        tri_ref[...] = jnp.where(qry <= key, 1.0, 0.0).astype(BF16)
        max_exact = REL_BUCKETS // 2
        for ref, off in ((bd_ref, 0), (bp_ref, qb)):
            n = jnp.maximum(qry - key + off, 0)
            large = max_exact + (
                jnp.log(jnp.maximum(n, 1).astype(F32) / max_exact)
                / math.log(REL_MAX_DIST / max_exact) * (REL_BUCKETS - max_exact)).astype(jnp.int32)
            bucket = jnp.where(n < max_exact, n, jnp.minimum(large, REL_BUCKETS - 1))
            for h in range(ATT_HEADS):
                bias = jnp.zeros((qb, qb), F32)
                for k in range(REL_BUCKETS):
                    bias = jnp.where(bucket == k, tab_ref[k * ATT_HEADS + h], bias)
                ref[h] = (bias - tab_ref[(REL_BUCKETS - 1) * ATT_HEADS + h]) * LOG2E

    qt = qt_ref[0, 0]
    qit = qit_ref[0, 0]
    q_head = lax.broadcasted_iota(jnp.int32, qt.shape, 0) // ATT_HD
    qi_head = lax.broadcasted_iota(jnp.int32, qit.shape, 0) // IDX_HD
    for h in range(ATT_HEADS):
        qm_ref[h] = jnp.where(q_head == h, qt, jnp.zeros_like(qt))
    for h in range(IDX_HEADS):
        qim_ref[h] = jnp.where(qi_head == h, qit, jnp.zeros_like(qit))
    wv = wit_ref[0, 0] * ((IDX_HEADS ** -0.5) * (IDX_HD ** -0.5))

    t_pos = i * qb + qry
    n_pairs = (i + 2) // 2

    def keys_of(c):
        return pl.ds(pl.multiple_of(c * qb, qb), qb)

    def score_chunk(c, masked):
        kic = ki_ref[0, keys_of(jnp.minimum(c, nch - 1)), :]
        sc = jnp.zeros((qb, qb), F32)
        for h in range(IDX_HEADS):
            sc = sc + wv[h:h + 1, :] * jnp.maximum(_dot(kic, qim_ref[h]), 0.0)
        if masked:
            valid = (c * qb + key) <= t_pos
            top = _fold(jnp.where(valid, sc, -jnp.inf), jnp.maximum, BF16_ROWS)
            bot = _fold(jnp.where(valid, sc, jnp.inf), jnp.minimum, BF16_ROWS)
            sc = jnp.where(valid, sc, -jnp.inf)
        s_ref[c] = sc
        sc16 = sc.astype(BF16)
        s16_ref[c] = sc16
        if not masked:
            top = _fold(sc16, jnp.maximum, BF16_ROWS).astype(F32)
            bot = _fold(sc16, jnp.minimum, BF16_ROWS).astype(F32)
        return top, bot

    def score_chunks(first, count, carry, masked):
        top, bot = carry
        for j in range(count):
            top_j, bot_j = score_chunk(first + j, masked)
            top, bot = jnp.maximum(top, top_j), jnp.minimum(bot, bot_j)
        return top, bot

    quads = (n_pairs - 1) // 2
    carry = lax.fori_loop(
        0, quads, lambda g, cr: score_chunks(4 * g, 4, cr, False),
        (jnp.full((BF16_ROWS, qb), -jnp.inf, F32), jnp.full((BF16_ROWS, qb), jnp.inf, F32)))
    carry = lax.fori_loop(
        2 * quads, n_pairs - 1, lambda p, cr: score_chunks(2 * p, 2, cr, False), carry)
    top, bot = score_chunks(2 * (n_pairs - 1), 2, carry, True)
    rmin = jnp.min(bot, axis=0, keepdims=True)
    cap16 = jnp.max(top, axis=0, keepdims=True).astype(BF16).astype(F32)
    rmax = cap16 + (jnp.abs(cap16) * BF16_INTERVAL + TINY)

    n_valid = (i * qb + lax.broadcasted_iota(jnp.int32, (1, qb), 1) + 1).astype(F32)
    kp = jnp.minimum(float(topk), n_valid)

    def count_ge(thr):
        def body(p, acc):
            g0 = _fold8(jnp.where(s_ref[2 * p] >= thr, 1.0, 0.0), jnp.add)
            g1 = _fold8(jnp.where(s_ref[2 * p + 1] >= thr, 1.0, 0.0), jnp.add)
            return acc + (g0 + g1)
        acc = lax.fori_loop(0, n_pairs, body, jnp.zeros((SUBLANES, qb), F32))
        return jnp.sum(acc, axis=0, keepdims=True)

    one16 = jnp.ones((), BF16)
    zero16 = jnp.zeros((), BF16)

    def count_ge16(thr16):
        def body(p, acc):
            g0 = _fold(jnp.where(s16_ref[2 * p] >= thr16, one16, zero16), jnp.add, BF16_ROWS)
            g1 = _fold(jnp.where(s16_ref[2 * p + 1] >= thr16, one16, zero16), jnp.add, BF16_ROWS)
            return acc + (g0 + g1).astype(F32)
        acc = lax.fori_loop(0, n_pairs, body, jnp.zeros((BF16_ROWS, qb), F32))
        return jnp.sum(acc, axis=0, keepdims=True)

    def coarse(_, carry):
        lo, hi = carry
        mid16 = (0.5 * lo + 0.5 * jnp.minimum(hi, cap16)).astype(BF16)
        feas = count_ge16(mid16) >= kp
        mid = mid16.astype(F32)
        return jnp.where(feas, mid, lo), jnp.where(feas, hi, mid)

    lo16, hi = lax.fori_loop(0, COARSE_STEPS, coarse,
                             (rmin.astype(BF16).astype(F32), jnp.full((1, qb), jnp.inf, F32)))
    lo = lo16 - (jnp.abs(lo16) * BF16_INTERVAL + TINY)
    chi = count_ge(hi)

    def bisect(_, carry):
        lo, hi, chi = carry
        mid = 0.5 * lo + 0.5 * jnp.minimum(hi, rmax)
        cnt = count_ge(mid)
        feas = cnt >= kp
        return (jnp.where(feas, mid, lo), jnp.where(feas, hi, mid), jnp.where(feas, chi, cnt))

    lo, hi, chi = lax.fori_loop(0, BISECT_STEPS, bisect, (lo, hi, chi))

    def max_below(thr):
        def body(p, acc):
            x0 = s_ref[2 * p]
            x1 = s_ref[2 * p + 1]
            m0 = _fold8(jnp.where(x0 < thr, x0, -jnp.inf), jnp.maximum)
            m1 = _fold8(jnp.where(x1 < thr, x1, -jnp.inf), jnp.maximum)
            return jnp.maximum(acc, jnp.maximum(m0, m1))
        acc = lax.fori_loop(0, n_pairs, body, jnp.full((SUBLANES, qb), -jnp.inf, F32))
        return jnp.max(acc, axis=0, keepdims=True)

    def finish_cond(state):
        return state[4] > 0.0

    def finish_body(state):
        hi, chi, tau, done, _ = state
        m = max_below(hi)
        cnt = count_ge(m)
        feas = cnt >= kp
        active = done < 0.5
        tau = jnp.where(active & feas, m, tau)
        hi = jnp.where(active & (~feas), m, hi)
        chi = jnp.where(active & (~feas), cnt, chi)
        done = jnp.where(active & feas, 1.0, done)
        return hi, chi, tau, done, jnp.max(1.0 - done)

    hi, chi, tau, _, _ = lax.while_loop(
        finish_cond, finish_body,
        (hi, chi, lo, jnp.zeros((1, qb), F32), jnp.float32(1.0)))
    need = kp - chi

    acc_ref[...] = jnp.zeros_like(acc_ref)
    lg1_ref[...] = jnp.full(lg1_ref.shape, NEG, F32)
    heads = range(ATT_HEADS)

    def stage_a(c, bias, live, buf, need_left):
        x = s_ref[c]
        eq = jnp.where(x == tau, 1.0, 0.0)
        if live is not None:
            eq = jnp.where(live, eq, 0.0)
        prefix = _dot(tri_ref[...], eq.astype(BF16))
        maskadd = jnp.where(x >= jnp.where(prefix <= need_left, tau, hi), 0.0, NEG)
        if live is not None:
            maskadd = jnp.where(live, maskadd, NEG)
        kc = k_ref[0, keys_of(c), :]
        mxs = []
        for h in heads:
            lg = _dot(kc, qm_ref[h]) + maskadd
            if bias is not None:
                lg = lg + bias[h]
            buf[h] = lg
            mxs.append(jnp.max(_fold8(lg, jnp.maximum), axis=0, keepdims=True))
        return tuple(mxs), need_left - prefix[qb - 1:qb, :]

    ones_rows = jnp.ones((2 * SUBLANES, qb), BF16)

    def stage_b(c, buf, mxs, ms, ls):
        vtc = vt_ref[0, c]
        new_ms = [jnp.maximum(ms[h], mxs[h]) for h in heads]
        alphas = [jnp.exp2(ms[h] - new_ms[h]) for h in heads]
        pvs = [_dot(jnp.concatenate([vtc[h * ATT_HD:(h + 1) * ATT_HD, :], ones_rows], axis=0),
                    jnp.exp2(buf[h] - new_ms[h]).astype(BF16)) for h in heads]
        new_ls = [alphas[h] * ls[h] + pvs[h][ATT_HD:ATT_HD + 1, :] for h in heads]
        pv = jnp.concatenate([pvs[h][0:ATT_HD, :] for h in heads], axis=0)
        alpha_rows = jnp.concatenate(
            [jnp.broadcast_to(alphas[h], (ATT_HD, qb)) for h in heads], axis=0)
        acc_ref[...] = acc_ref[...] * alpha_rows + pv
        return tuple(new_ms), tuple(new_ls)

    lg_refs = (lg0_ref, lg1_ref)

    def run_stages(stages, carry):
        ms, ls, need_left, pend_mx, pend_c = carry
        for k, (c, bias, live) in enumerate(stages):
            mx, need_left = stage_a(c, bias, live, lg_refs[k % 2], need_left)
            ms, ls = stage_b(pend_c, lg_refs[1 - k % 2], pend_mx, ms, ls)
            pend_mx, pend_c = mx, c
        return ms, ls, need_left, pend_mx, pend_c

    n_far = jnp.maximum(i - 1, 0)
    m_init = tuple(jnp.full((1, qb), M_INIT, F32) for _ in heads)
    carry = (m_init, tuple(jnp.zeros((1, qb), F32) for _ in heads), need, m_init, jnp.int32(0))
    carry = lax.fori_loop(
        0, n_far // ATTEND_UNROLL,
        lambda g, cr: run_stages([(ATTEND_UNROLL * g + j, None, None)
                                  for j in range(ATTEND_UNROLL)], cr),
        carry)
    far_done = (n_far // ATTEND_UNROLL) * ATTEND_UNROLL

    def tail(n_left):
        def run(cr):
            stages = [(far_done + j, None, None) for j in range(n_left)]
            stages += [(jnp.maximum(i - 1, 0), bp_ref, i >= 1), (i, bd_ref, None)]
            ms, ls, _, pend_mx, pend_c = run_stages(stages, cr)
            return stage_b(pend_c, lg_refs[1 - len(stages) % 2], pend_mx, ms, ls)[1]
        return run

    ls = lax.switch(n_far - far_done, [tail(r) for r in range(ATTEND_UNROLL)], carry)

    for h in range(ATT_HEADS):
        rows = slice(h * ATT_HD, (h + 1) * ATT_HD)
        acc_ref[rows, :] = acc_ref[rows, :] / ls[h]
    o_ref[0] = acc_ref[...].T.astype(BF16)


def _attention(tab, qt, qit, wit, k, ki, vt):
    batch, nch, w, qb = qt.shape
    seq = nch * qb
    topk = min(TOPK_MAX, seq // 4)
    per_block = lambda rows: pl.BlockSpec((1, 1, rows, qb), lambda b, i: (b, i, 0, 0))
    return pl.pallas_call(
        functools.partial(_attn_body, topk),
        grid=(batch, nch),
        in_specs=[
            pl.BlockSpec(memory_space=pltpu.SMEM),
            per_block(w), per_block(LANES), per_block(SUBLANES),
            pl.BlockSpec((1, seq, w), lambda b, i: (b, 0, 0)),
            pl.BlockSpec((1, seq, LANES), lambda b, i: (b, 0, 0)),
            pl.BlockSpec((1, nch, w, qb), lambda b, i: (b, 0, 0, 0)),
        ],
        out_specs=pl.BlockSpec((1, qb, w), lambda b, i: (b, i, 0)),
        out_shape=jax.ShapeDtypeStruct((batch, seq, w), BF16),
        scratch_shapes=[
            pltpu.VMEM((nch + 1, qb, qb), F32),
            pltpu.VMEM((nch + 1, qb, qb), BF16),
            pltpu.VMEM((ATT_HEADS, qb, qb), F32),
            pltpu.VMEM((ATT_HEADS, qb, qb), F32),
            pltpu.VMEM((qb, qb), BF16),
            pltpu.VMEM((ATT_HEADS, w, qb), BF16),
            pltpu.VMEM((IDX_HEADS, LANES, qb), BF16),
            pltpu.VMEM((w, qb), F32),
            pltpu.VMEM((ATT_HEADS, qb, qb), F32),
            pltpu.VMEM((ATT_HEADS, qb, qb), F32),
        ],
        compiler_params=pltpu.CompilerParams(
            dimension_semantics=("arbitrary", "arbitrary"), vmem_limit_bytes=VMEM_LIMIT_BYTES),
        name="sparse_attention",
    )(tab, qt, qit, wit, k, ki, vt)


def _back_body(x_ref, bg_ref, bp_ref, ba_ref, bc_ref, pre_ref, post_ref, wgate_ref, wbr_ref,
               wout_ref, o_ref):
    for r in range(0, x_ref.shape[0], FFN_ROWS):
        rows = slice(r, r + FFN_ROWS)
        x = x_ref[rows, :]
        xn = _rmsnorm(x, pre_ref[...]).astype(BF16)
        y = None
        for n, br in enumerate((bg_ref, bp_ref, ba_ref, bc_ref)):
            term = _sigmoid(_dot(xn, wgate_ref[n])) * _dot(br[rows, :], wbr_ref[n])
            y = term if y is None else y + term
        h = _dot(y.astype(BF16), wout_ref[...])
        o_ref[rows, :] = x + _rmsnorm(h, post_ref[...])


def _back(x, bg, bp, ba, bc, pre_g, post_g, wgate, wbr, wout):
    n, d = x.shape
    w = BRANCH_W
    tm = min(FFN_BLOCK, n)
    assert tm % FFN_ROWS == 0
    full = lambda a: pl.BlockSpec(a.shape, lambda i: (0,) * a.ndim)
    resident = lambda a: pl.BlockSpec(a.shape, lambda i: (0,) * a.ndim,
                                      pipeline_mode=pl.Buffered(1))
    tok = lambda width: pl.BlockSpec((tm, width), lambda i: (i, 0))
    params = (pre_g, post_g, wgate, wbr, wout)
    return pl.pallas_call(
        _back_body,
        grid=(n // tm,),
        in_specs=([tok(d), tok(w), tok(w), tok(w), tok(w), full(pre_g), full(post_g)]
                  + [resident(a) for a in (wgate, wbr, wout)]),
        out_specs=tok(d),
        out_shape=jax.ShapeDtypeStruct((n, d), F32),
        compiler_params=pltpu.CompilerParams(
            dimension_semantics=("arbitrary",), vmem_limit_bytes=VMEM_LIMIT_BYTES),
        name="mixer_back",
    )(x, bg, bp, ba, bc, *params)


def _split_ffn_weights(w_gu, w_down):
    return w_gu.astype(BF16), w_down.astype(BF16)


def _pack_front_weights(w_in):
    d = w_in.shape[0]
    w = BRANCH_W
    o = 0
    uv = w_in[:, o:o + 2 * w]; o += 2 * w
    pz = w_in[:, o:o + w]; o += w
    qz = w_in[:, o:o + w]; o += w
    kz = w_in[:, o:o + w]; o += w
    vz = w_in[:, o:o + w]; o += w
    qi = w_in[:, o:o + IDX_HEADS * IDX_HD]; o += IDX_HEADS * IDX_HD
    ki = w_in[:, o:o + IDX_HD]; o += IDX_HD
    wi = w_in[:, o:o + IDX_HEADS]; o += IDX_HEADS
    cz = w_in[:, o:o + 2 * w]; o += 2 * w
    gz = w_in[:, o:]
    w1 = jnp.concatenate([uv, pz, kz, cz] + [ki] * (LANES // IDX_HD), axis=1).astype(BF16)
    wi_pad = jnp.pad(wi, ((0, 0), (0, _W2_ROWS - _WIT0 - IDX_HEADS)))
    w2t = jnp.concatenate([qz, vz, qi, wi_pad], axis=1).T.astype(BF16)
    wgate = gz.reshape(d, N_BRANCH, d).transpose(1, 0, 2).astype(BF16)
    return w1, w2t, wgate


def _block_diag(pw):
    g, c, _ = pw.shape
    out = jnp.zeros((g * c, g * c), pw.dtype)
    for k in range(g):
        out = out.at[k * c:(k + 1) * c, k * c:(k + 1) * c].set(pw[k])
    return out


def kernel(x, ffn1_pre_g, ffn1_post_g, ffn1_w_gu, ffn1_w_down, mix_pre_g, mix_post_g, w_in,
           gm_v_g, gm_ws, gm_b, pool_w, pool_scale, conv_dw, conv_b, conv_ln_g, conv_ln_b,
           w_branch, w_out, ffn2_pre_g, ffn2_post_g, ffn2_w_gu, ffn2_w_down, rel_bias):
    batch, seq, d = x.shape
    depth = w_in.shape[0]
    n = batch * seq
    tm = min(FRONT_BLOCK, seq)
    w = BRANCH_W
    row = lambda a: a.reshape(1, -1)
    tab = rel_bias.reshape(-1)

    xf = x.reshape(n, d)
    for l in range(depth):
        xf = _ffn(xf, row(ffn1_pre_g[l]), row(ffn1_post_g[l]),
                  *_split_ffn_weights(ffn1_w_gu[l], ffn1_w_down[l]))

        w1, w2t, wgate = _pack_front_weights(w_in[l])
        ws_cat = gm_ws[l].transpose(1, 0, 2).reshape(GM_CHUNK, GM_GROUPS * GM_CHUNK)
        gmb2d = jnp.repeat(gm_b[l].T, w // GM_GROUPS, axis=1)
        dw = jnp.pad(conv_dw[l], ((0, 1), (0, 0)))
        bg, bp, bc, k, ki, qt, vt, qit, wit = _front(
            xf, seq, row(mix_pre_g[l]), w1, w2t, row(gm_v_g[l]), ws_cat, gmb2d,
            _block_diag(pool_w[l]).astype(BF16), row(pool_scale[l]), dw, row(conv_b[l]),
            row(conv_ln_g[l]), row(conv_ln_b[l]), tm)
        ba = _attention(tab, qt, qit, wit, k.reshape(batch, seq, w),
                        ki.reshape(batch, seq, LANES), vt)
        xf = _back(xf, bg, bp, ba.reshape(n, w), bc, row(mix_pre_g[l]), row(mix_post_g[l]),
                   wgate, w_branch[l].astype(BF16), w_out[l].astype(BF16))

        xf = _ffn(xf, row(ffn2_pre_g[l]), row(ffn2_post_g[l]),
                  *_split_ffn_weights(ffn2_w_gu[l], ffn2_w_down[l]))
    return xf.reshape(batch, seq, d)
```

```python
import functools
import math

import jax
import jax.numpy as jnp
from jax import lax
from jax.experimental import pallas as pl
from jax.experimental.pallas import tpu as pltpu

F32 = jnp.float32
BF16 = jnp.bfloat16

EPS = 1e-6
BRANCH_W = 256
N_BRANCH = 4
GM_GROUPS = 4
GM_CHUNK = 128
POOL_WINDOWS = (2, 4, 8, 16)
ATT_HEADS = 4
ATT_HD = 64
IDX_HEADS = 4
IDX_HD = 32
TOPK_MAX = 256
REL_BUCKETS = 32
REL_MAX_DIST = 128
CONV_K = 31

VMEM_LIMIT_BYTES = 56 * 1024 * 1024
LANES = 128
SUBLANES = 8

FFN_CHUNK = 256
FFN_BLOCK = 1024
FFN_ROWS = 512
FRONT_BLOCK = 1024
FRONT_ROWS = 512
CONV_HALO = 32
POOL_HALO = 16
CONV_ROWS = 64
ATT_BLOCK = 256
ATTEND_UNROLL = 4
COARSE_STEPS = 10
BISECT_STEPS = 8
BF16_ROWS = 16
SORT_GROUP = 8
BF16_INTERVAL = 2.0 ** -6
TINY = 1e-30
LOG2E = math.log2(math.e)
Q_SCALE = ATT_HD ** -0.5 * LOG2E
NEG = -1e30
M_INIT = -1e29


def _rmsnorm(x, g):
    return x * lax.rsqrt(jnp.mean(x * x, axis=-1, keepdims=True) + EPS) * g


def _sigmoid(x):
    return 1.0 / (1.0 + jnp.exp(-x))


def _dot(a, b):
    return jnp.dot(a, b, preferred_element_type=F32)


def _fold8(x, op):
    return _fold(x, op, SUBLANES)


def _fold(x, op, rows):
    parts = [x[j * rows:(j + 1) * rows] for j in range(x.shape[0] // rows)]
    while len(parts) > 1:
        nxt = [op(parts[j], parts[j + 1]) for j in range(0, len(parts) - 1, 2)]
        if len(parts) % 2:
            nxt.append(parts[-1])
        parts = nxt
    return parts[0]


_SORT_NETWORK = (
    ((0, 1), (2, 3), (4, 5), (6, 7)), ((0, 2), (1, 3), (4, 6), (5, 7)),
    ((1, 2), (5, 6), (0, 4), (3, 7)), ((1, 5), (2, 6)), ((1, 4), (3, 6)),
    ((2, 4), (3, 5)), ((3, 4),))


def _sort_desc(vals):
    vals = list(vals)
    for layer in _SORT_NETWORK:
        for a, b in layer:
            vals[a], vals[b] = jnp.maximum(vals[a], vals[b]), jnp.minimum(vals[a], vals[b])
    return vals


def _count_ge_sorted(s, thr, dtype):
    c = lambda v: jnp.full((), v, dtype)
    ge4 = s[3] >= thr
    pivot = jnp.where(ge4, s[5], s[1])
    ge2 = pivot >= thr
    pivot = jnp.where(ge4, jnp.where(ge2, s[6], s[4]), jnp.where(ge2, s[2], s[0]))
    return ((jnp.where(ge4, c(4), c(0)) + jnp.where(ge2, c(2), c(0)))
            + (jnp.where(pivot >= thr, c(1), c(0)) + jnp.where(s[7] >= thr, c(1), c(0))))


def _max_below_sorted(s, thr):
    ge4 = s[3] >= thr
    pivot = jnp.where(ge4, s[5], s[1])
    ge2 = pivot >= thr
    pivot = jnp.where(ge4, jnp.where(ge2, s[6], s[4]), jnp.where(ge2, s[2], s[0]))
    last = jnp.where(s[7] < thr, s[7], -jnp.inf)
    after = jnp.where(ge4, jnp.where(ge2, last, s[5]), jnp.where(ge2, s[3], s[1]))
    return jnp.where(pivot >= thr, after, pivot)


def _ffn_body(x_ref, pre_ref, post_ref, wgu_ref, wd_ref, o_ref, xn_ref, acc_ref):
    f = wd_ref.shape[0]
    tm = x_ref.shape[0]
    halves = [slice(r, r + FFN_ROWS) for r in range(0, tm, FFN_ROWS)]
    for rows in halves:
        xn_ref[rows, :] = _rmsnorm(x_ref[rows, :], pre_ref[...]).astype(BF16)
    for c in range(0, f, FFN_CHUNK):
        for rows in halves:
            xn = xn_ref[rows, :]
            a = _dot(xn, wgu_ref[:, c:c + FFN_CHUNK])
            b = _dot(xn, wgu_ref[:, f + c:f + c + FFN_CHUNK])
            hm = (a * _sigmoid(a) * b).astype(BF16)
            down = _dot(hm, wd_ref[c:c + FFN_CHUNK, :])
            acc_ref[rows, :] = down if c == 0 else acc_ref[rows, :] + down
    for rows in halves:
        o_ref[rows, :] = x_ref[rows, :] + 0.5 * _rmsnorm(acc_ref[rows, :], post_ref[...])


def _ffn(x, pre_g, post_g, wgu, wd):
    n, d = x.shape
    f = wd.shape[0]
    tm = min(FFN_BLOCK, n)
    assert f % FFN_CHUNK == 0 and wgu.shape == (d, 2 * f) and tm % FFN_ROWS == 0
    full = lambda shape: pl.BlockSpec(shape, lambda i: (0,) * len(shape))
    resident = lambda shape: pl.BlockSpec(shape, lambda i: (0,) * len(shape),
                                          pipeline_mode=pl.Buffered(1))
    return pl.pallas_call(
        _ffn_body,
        grid=(n // tm,),
        in_specs=[
            pl.BlockSpec((tm, d), lambda i: (i, 0)),
            full((1, d)), full((1, d)), resident((d, 2 * f)), resident((f, d)),
        ],
        out_specs=pl.BlockSpec((tm, d), lambda i: (i, 0)),
        out_shape=jax.ShapeDtypeStruct((n, d), F32),
        scratch_shapes=[pltpu.VMEM((tm, d), BF16), pltpu.VMEM((tm, d), F32)],
        compiler_params=pltpu.CompilerParams(
            dimension_semantics=("arbitrary",), vmem_limit_bytes=VMEM_LIMIT_BYTES),
        name="ffn",
    )(x, pre_g, post_g, wgu, wd)


_UV0, _PZ0, _K0, _CZ0, _KI0, _W1_COLS = 0, 512, 768, 1024, 1536, 1664
_QT0, _VT0, _QIT0, _WIT0, _W2_ROWS = 0, 256, 512, 640, 656


def _front_body(blocks_per_seq,
                x_ref, pre_ref, w1_ref, w2t_ref, gmvg_ref, ws_ref, gmb_ref, poolw_ref,
                pscale_ref, dw_ref, cb_ref, lng_ref, lnb_ref,
                bg_ref, bp_ref, bc_ref, k_ref, ki_ref, qt_ref, vt_ref, qit_ref, wit_ref,
                hbuf, pbuf):
    tm_step = x_ref.shape[0]
    tm = min(FRONT_ROWS, tm_step)
    w = BRANCH_W
    j = pl.program_id(0) % blocks_per_seq

    @pl.when(j == 0)
    def _():
        hbuf[...] = jnp.zeros_like(hbuf)
        pbuf[0:POOL_HALO, :] = jnp.zeros((POOL_HALO, w), F32)

    row = lax.broadcasted_iota(jnp.int32, (GM_CHUNK, GM_GROUPS * GM_CHUNK), 0)
    col = lax.broadcasted_iota(jnp.int32, (GM_CHUNK, GM_GROUPS * GM_CHUNK), 1)
    wsm = jnp.where((col % GM_CHUNK) <= row, ws_ref[...], 0.0).astype(BF16)
    lane_group = lax.broadcasted_iota(jnp.int32, (GM_CHUNK, w), 1) // (w // GM_GROUPS)
    lane_win = lax.broadcasted_iota(jnp.int32, (tm, w), 1) // (w // len(POOL_WINDOWS))

    for r0 in range(0, tm_step, tm):
        rows = slice(r0, r0 + tm)
        xn = _rmsnorm(x_ref[rows, :], pre_ref[...]).astype(BF16)

        k_ref[rows, :] = _dot(xn, w1_ref[:, _K0:_K0 + w]).astype(BF16)
        ki_ref[rows, :] = _dot(xn, w1_ref[:, _KI0:_KI0 + LANES]).astype(BF16)
        zt = lax.dot_general(w2t_ref[...], xn, (((1,), (1,)), ((), ())),
                             preferred_element_type=F32)
        for cc in range(tm // ATT_BLOCK):
            sl = slice(cc * ATT_BLOCK, (cc + 1) * ATT_BLOCK)
            oc = r0 // ATT_BLOCK + cc
            qt_ref[0, oc] = (zt[_QT0:_QT0 + w, sl] * Q_SCALE).astype(BF16)
            vt_ref[0, oc] = zt[_VT0:_VT0 + w, sl].astype(BF16)
            qit_ref[0, oc] = zt[_QIT0:_QIT0 + LANES, sl].astype(BF16)
            wit_ref[0, oc] = zt[_WIT0:_WIT0 + SUBLANES, sl]

        uv = _dot(xn, w1_ref[:, _UV0:_UV0 + 2 * w])
        guv = uv * (0.5 * (1.0 + jnp.tanh(math.sqrt(2.0 / math.pi) * (uv + 0.044715 * (uv ** 3)))))
        u = guv[:, 0:w]
        vv = _rmsnorm(guv[:, w:2 * w], gmvg_ref[...])
        for c in range(tm // GM_CHUNK):
            rs = slice(c * GM_CHUNK, (c + 1) * GM_CHUNK)
            vc = vv[rs, :]
            stacked = jnp.concatenate(
                [jnp.where(lane_group == g, vc, 0.0) for g in range(GM_GROUPS)], axis=0).astype(BF16)
            mixed = _dot(wsm, stacked) + gmb_ref[...]
            bg_ref[r0 + c * GM_CHUNK:r0 + (c + 1) * GM_CHUNK, :] = (u[rs, :] * mixed).astype(BF16)

        p = _dot(xn, w1_ref[:, _PZ0:_PZ0 + w])
        pbuf[POOL_HALO + r0:POOL_HALO + r0 + tm, :] = p
        pos1 = (j * tm_step + r0 + lax.broadcasted_iota(jnp.int32, (tm, w), 0) + 1).astype(F32)
        run = p
        win = jnp.zeros((tm, w), F32)
        cnt = jnp.zeros((tm, w), F32)
        shift = 1
        for g, wlen in enumerate(POOL_WINDOWS):
            while shift < wlen:
                run = run + pbuf[POOL_HALO + r0 - shift:POOL_HALO + r0 - shift + tm, :]
                shift += 1
            win = jnp.where(lane_win == g, run, win)
            cnt = jnp.where(lane_win == g, jnp.minimum(pos1, float(wlen)), cnt)
        dpool = (win / cnt - p).astype(BF16)
        bp_ref[rows, :] = (_dot(dpool, poolw_ref[...]) * pscale_ref[...]).astype(BF16)

        cz = _dot(xn, w1_ref[:, _CZ0:_CZ0 + 2 * w])
        hbuf[CONV_HALO + r0:CONV_HALO + r0 + tm, :] = cz[:, 0:w] * _sigmoid(cz[:, w:2 * w])
        lead = CONV_HALO - (CONV_K - 1)
        for r in range(r0, r0 + tm, CONV_ROWS):
            acc = None
            for shift in range(SUBLANES):
                part = None
                for t in range(CONV_K):
                    if (lead + t) % SUBLANES == shift:
                        start = r + lead + t - shift
                        term = hbuf[start:start + CONV_ROWS + SUBLANES, :] * dw_ref[t:t + 1, :]
                        part = term if part is None else part + term
                part = part[shift:shift + CONV_ROWS, :]
                acc = part if acc is None else acc + part
            hc = acc + cb_ref[...]
            mu = jnp.mean(hc, axis=-1, keepdims=True)
            xc = hc - mu
            yn = xc * lax.rsqrt(jnp.mean(xc * xc, axis=-1, keepdims=True) + EPS)
            yn = yn * lng_ref[...] + lnb_ref[...]
            bc_ref[r:r + CONV_ROWS, :] = (yn * _sigmoid(yn)).astype(BF16)

    pbuf[0:POOL_HALO, :] = pbuf[tm_step:tm_step + POOL_HALO, :]
    hbuf[0:CONV_HALO, :] = hbuf[tm_step:tm_step + CONV_HALO, :]


def _front(x, seq, pre_g, w1, w2t, gmvg, ws_cat, gmb2d, poolw, pscale, dw, cb, lng, lnb, tm):
    n, d = x.shape
    batch = n // seq
    bps = seq // tm
    w = BRANCH_W
    nch = seq // ATT_BLOCK
    cpb = tm // ATT_BLOCK
    full = lambda a: pl.BlockSpec(a.shape, lambda i: (0,) * a.ndim)
    tok = lambda width: pl.BlockSpec((tm, width), lambda i: (i, 0))
    chunked = lambda rows: pl.BlockSpec((1, cpb, rows, ATT_BLOCK),
                                        lambda i: (i // bps, i % bps, 0, 0))
    params = (pre_g, w1, w2t, gmvg, ws_cat, gmb2d, poolw, pscale, dw, cb, lng, lnb)
    out_shape = (
        jax.ShapeDtypeStruct((n, w), BF16),
        jax.ShapeDtypeStruct((n, w), BF16),
        jax.ShapeDtypeStruct((n, w), BF16),
        jax.ShapeDtypeStruct((n, w), BF16),
        jax.ShapeDtypeStruct((n, LANES), BF16),
        jax.ShapeDtypeStruct((batch, nch, w, ATT_BLOCK), BF16),
        jax.ShapeDtypeStruct((batch, nch, w, ATT_BLOCK), BF16),
        jax.ShapeDtypeStruct((batch, nch, LANES, ATT_BLOCK), BF16),
        jax.ShapeDtypeStruct((batch, nch, SUBLANES, ATT_BLOCK), F32),
    )
    out_specs = (tok(w), tok(w), tok(w), tok(w), tok(LANES),
                 chunked(w), chunked(w), chunked(LANES), chunked(SUBLANES))
    return pl.pallas_call(
        functools.partial(_front_body, bps),
        grid=(n // tm,),
        in_specs=[tok(d)] + [full(a) for a in params],
        out_specs=out_specs,
        out_shape=out_shape,
        scratch_shapes=[pltpu.VMEM((tm + CONV_HALO + SUBLANES, w), F32),
                        pltpu.VMEM((tm + POOL_HALO, w), F32)],
        compiler_params=pltpu.CompilerParams(
            dimension_semantics=("arbitrary",), vmem_limit_bytes=VMEM_LIMIT_BYTES),
        name="mixer_front",
    )(x, *params)


def _attn_body(topk,
               tab_ref, qt_ref, qit_ref, wit_ref, k_ref, ki_ref, vt_ref, o_ref,
               s_ref, ssort_ref, s16sort_ref, bd_ref, bp_ref, tri_ref, qm_ref, qim_ref, acc_ref,
               lg0_ref, lg1_ref):
    qb = ATT_BLOCK
    nch = vt_ref.shape[1]
    b = pl.program_id(0)
    i = pl.program_id(1)
    key = lax.broadcasted_iota(jnp.int32, (qb, qb), 0)
    qry = lax.broadcasted_iota(jnp.int32, (qb, qb), 1)

    @pl.when((b == 0) & (i == 0))
    def _():
        tri_ref[...] = jnp.where(qry <= key, 1.0, 0.0).astype(BF16)
        max_exact = REL_BUCKETS // 2
        for ref, off in ((bd_ref, 0), (bp_ref, qb)):
            n = jnp.maximum(qry - key + off, 0)
            large = max_exact + (
                jnp.log(jnp.maximum(n, 1).astype(F32) / max_exact)
                / math.log(REL_MAX_DIST / max_exact) * (REL_BUCKETS - max_exact)).astype(jnp.int32)
            bucket = jnp.where(n < max_exact, n, jnp.minimum(large, REL_BUCKETS - 1))
            for h in range(ATT_HEADS):
                bias = jnp.zeros((qb, qb), F32)
                for k in range(REL_BUCKETS):
                    bias = jnp.where(bucket == k, tab_ref[k * ATT_HEADS + h], bias)
                ref[h] = (bias - tab_ref[(REL_BUCKETS - 1) * ATT_HEADS + h]) * LOG2E

    qt = qt_ref[0, 0]
    qit = qit_ref[0, 0]
    q_head = lax.broadcasted_iota(jnp.int32, qt.shape, 0) // ATT_HD
    qi_head = lax.broadcasted_iota(jnp.int32, qit.shape, 0) // IDX_HD
    for h in range(ATT_HEADS):
        qm_ref[h] = jnp.where(q_head == h, qt, jnp.zeros_like(qt))
    for h in range(IDX_HEADS):
        qim_ref[h] = jnp.where(qi_head == h, qit, jnp.zeros_like(qit))
    wv = wit_ref[0, 0] * ((IDX_HEADS ** -0.5) * (IDX_HD ** -0.5))

    t_pos = i * qb + qry
    n_pairs = (i + 2) // 2

    def keys_of(c):
        return pl.ds(pl.multiple_of(c * qb, qb), qb)

    def score_chunk(c, masked):
        kic = ki_ref[0, keys_of(jnp.minimum(c, nch - 1)), :]
        sc = jnp.zeros((qb, qb), F32)
        for h in range(IDX_HEADS):
            sc = sc + wv[h:h + 1, :] * jnp.maximum(_dot(kic, qim_ref[h]), 0.0)
        if masked:
            valid = (c * qb + key) <= t_pos
            bot = _fold8(jnp.where(valid, sc, jnp.inf), jnp.minimum)
            sc = jnp.where(valid, sc, -jnp.inf)
        s_ref[c] = sc
        slabs = [sc[r:r + SUBLANES] for r in range(0, qb, SUBLANES)]
        groups = [_sort_desc(slabs[g:g + SORT_GROUP]) for g in range(0, len(slabs), SORT_GROUP)]
        ssort_ref[c] = jnp.concatenate([slab for grp in groups for slab in grp], axis=0)
        s16sort_ref[c] = jnp.concatenate(
            [slab for g in range(0, len(groups), 2) for pair in zip(groups[g], groups[g + 1])
             for slab in pair], axis=0).astype(BF16)
        top = functools.reduce(jnp.maximum, [grp[0] for grp in groups])
        if not masked:
            bot = functools.reduce(jnp.minimum, [grp[-1] for grp in groups])
        return top, bot

    def score_chunks(first, count, carry, masked):
        top, bot = carry
        for j in range(count):
            top_j, bot_j = score_chunk(first + j, masked)
            top, bot = jnp.maximum(top, top_j), jnp.minimum(bot, bot_j)
        return top, bot

    quads = (n_pairs - 1) // 2
    carry = lax.fori_loop(
        0, quads, lambda g, cr: score_chunks(4 * g, 4, cr, False),
        (jnp.full((SUBLANES, qb), -jnp.inf, F32), jnp.full((SUBLANES, qb), jnp.inf, F32)))
    carry = lax.fori_loop(
        2 * quads, n_pairs - 1, lambda p, cr: score_chunks(2 * p, 2, cr, False), carry)
    top, bot = score_chunks(2 * (n_pairs - 1), 2, carry, True)
    rmin = jnp.min(bot, axis=0, keepdims=True)
    rmax = jnp.max(top, axis=0, keepdims=True)
    cap16 = rmax.astype(BF16).astype(F32)

    n_valid = (i * qb + lax.broadcasted_iota(jnp.int32, (1, qb), 1) + 1).astype(F32)
    kp = jnp.minimum(float(topk), n_valid)

    def count_sorted(x, thr, rows, dtype):
        counts = [_count_ge_sorted([x[r + j * rows:r + (j + 1) * rows] for j in range(SORT_GROUP)],
                                   thr, dtype)
                  for r in range(0, qb, SORT_GROUP * rows)]
        return functools.reduce(jnp.add, counts)

    def count_ge(thr):
        def body(p, acc):
            return acc + (count_sorted(ssort_ref[2 * p], thr, SUBLANES, F32)
                          + count_sorted(ssort_ref[2 * p + 1], thr, SUBLANES, F32))
        acc = lax.fori_loop(0, n_pairs, body, jnp.zeros((SUBLANES, qb), F32))
        return jnp.sum(acc, axis=0, keepdims=True)

    def count_ge16(thr16):
        def body(p, acc):
            both = (count_sorted(s16sort_ref[2 * p], thr16, BF16_ROWS, BF16)
                    + count_sorted(s16sort_ref[2 * p + 1], thr16, BF16_ROWS, BF16))
            return acc + both.astype(F32)
        acc = lax.fori_loop(0, n_pairs, body, jnp.zeros((BF16_ROWS, qb), F32))
        return jnp.sum(acc, axis=0, keepdims=True)

    def coarse(_, carry):
        lo, hi = carry
        mid16 = (0.5 * lo + 0.5 * jnp.minimum(hi, cap16)).astype(BF16)
        feas = count_ge16(mid16) >= kp
        mid = mid16.astype(F32)
        return jnp.where(feas, mid, lo), jnp.where(feas, hi, mid)

    lo16, hi = lax.fori_loop(0, COARSE_STEPS, coarse,
                             (rmin.astype(BF16).astype(F32), jnp.full((1, qb), jnp.inf, F32)))
    lo = lo16 - (jnp.abs(lo16) * BF16_INTERVAL + TINY)
    chi = count_ge(hi)

    def bisect(_, carry):
        lo, hi, chi = carry
        mid = 0.5 * lo + 0.5 * jnp.minimum(hi, rmax)
        cnt = count_ge(mid)
        feas = cnt >= kp
        return (jnp.where(feas, mid, lo), jnp.where(feas, hi, mid), jnp.where(feas, chi, cnt))

    lo, hi, chi = lax.fori_loop(0, BISECT_STEPS, bisect, (lo, hi, chi))

    def max_below(thr):
        def body(p, acc):
            for c in (2 * p, 2 * p + 1):
                x = ssort_ref[c]
                for r in range(0, qb, SORT_GROUP * SUBLANES):
                    group = [x[r + j * SUBLANES:r + (j + 1) * SUBLANES] for j in range(SORT_GROUP)]
                    acc = jnp.maximum(acc, _max_below_sorted(group, thr))
            return acc
        acc = lax.fori_loop(0, n_pairs, body, jnp.full((SUBLANES, qb), -jnp.inf, F32))
        return jnp.max(acc, axis=0, keepdims=True)

    def finish_cond(state):
        return state[4] > 0.0

    def finish_body(state):
        hi, chi, tau, done, _ = state
        m = max_below(hi)
        cnt = count_ge(m)
        feas = cnt >= kp
        active = done < 0.5
        tau = jnp.where(active & feas, m, tau)
        hi = jnp.where(active & (~feas), m, hi)
        chi = jnp.where(active & (~feas), cnt, chi)
        done = jnp.where(active & feas, 1.0, done)
        return hi, chi, tau, done, jnp.max(1.0 - done)

    hi, chi, tau, _, _ = lax.while_loop(
        finish_cond, finish_body,
        (hi, chi, lo, jnp.zeros((1, qb), F32), jnp.float32(1.0)))
    need = kp - chi

    acc_ref[...] = jnp.zeros_like(acc_ref)
    lg1_ref[...] = jnp.full(lg1_ref.shape, NEG, F32)
    heads = range(ATT_HEADS)

    def stage_a(c, bias, live, buf, need_left):
        x = s_ref[c]
        eq = jnp.where(x == tau, 1.0, 0.0)
        if live is not None:
            eq = jnp.where(live, eq, 0.0)
        prefix = _dot(tri_ref[...], eq.astype(BF16))
        maskadd = jnp.where(x >= jnp.where(prefix <= need_left, tau, hi), 0.0, NEG)
        if live is not None:
            maskadd = jnp.where(live, maskadd, NEG)
        kc = k_ref[0, keys_of(c), :]
        mxs = []
        for h in heads:
            lg = _dot(kc, qm_ref[h]) + maskadd
            if bias is not None:
                lg = lg + bias[h]
            buf[h] = lg
            mxs.append(jnp.max(_fold8(lg, jnp.maximum), axis=0, keepdims=True))
        return tuple(mxs), need_left - prefix[qb - 1:qb, :]

    ones_rows = jnp.ones((2 * SUBLANES, qb), BF16)

    def stage_b(c, buf, mxs, ms, ls):
        vtc = vt_ref[0, c]
        new_ms = [jnp.maximum(ms[h], mxs[h]) for h in heads]
        alphas = [jnp.exp2(ms[h] - new_ms[h]) for h in heads]
        pvs = [_dot(jnp.concatenate([vtc[h * ATT_HD:(h + 1) * ATT_HD, :], ones_rows], axis=0),
                    jnp.exp2(buf[h] - new_ms[h]).astype(BF16)) for h in heads]
        new_ls = [alphas[h] * ls[h] + pvs[h][ATT_HD:ATT_HD + 1, :] for h in heads]
        pv = jnp.concatenate([pvs[h][0:ATT_HD, :] for h in heads], axis=0)
        alpha_rows = jnp.concatenate(
            [jnp.broadcast_to(alphas[h], (ATT_HD, qb)) for h in heads], axis=0)
        acc_ref[...] = acc_ref[...] * alpha_rows + pv
        return tuple(new_ms), tuple(new_ls)

    lg_refs = (lg0_ref, lg1_ref)

    def run_stages(stages, carry):
        ms, ls, need_left, pend_mx, pend_c = carry
        for k, (c, bias, live) in enumerate(stages):
            mx, need_left = stage_a(c, bias, live, lg_refs[k % 2], need_left)
            ms, ls = stage_b(pend_c, lg_refs[1 - k % 2], pend_mx, ms, ls)
            pend_mx, pend_c = mx, c
        return ms, ls, need_left, pend_mx, pend_c

    n_far = jnp.maximum(i - 1, 0)
    m_init = tuple(jnp.full((1, qb), M_INIT, F32) for _ in heads)
    carry = (m_init, tuple(jnp.zeros((1, qb), F32) for _ in heads), need, m_init, jnp.int32(0))
    carry = lax.fori_loop(
        0, n_far // ATTEND_UNROLL,
        lambda g, cr: run_stages([(ATTEND_UNROLL * g + j, None, None)
                                  for j in range(ATTEND_UNROLL)], cr),
        carry)
    far_done = (n_far // ATTEND_UNROLL) * ATTEND_UNROLL

    def tail(n_left):
        def run(cr):
            stages = [(far_done + j, None, None) for j in range(n_left)]
            stages += [(jnp.maximum(i - 1, 0), bp_ref, i >= 1), (i, bd_ref, None)]
            ms, ls, _, pend_mx, pend_c = run_stages(stages, cr)
            return stage_b(pend_c, lg_refs[1 - len(stages) % 2], pend_mx, ms, ls)[1]
        return run

    ls = lax.switch(n_far - far_done, [tail(r) for r in range(ATTEND_UNROLL)], carry)

    for h in range(ATT_HEADS):
        rows = slice(h * ATT_HD, (h + 1) * ATT_HD)
        acc_ref[rows, :] = acc_ref[rows, :] / ls[h]
    o_ref[0] = acc_ref[...].T.astype(BF16)


def _attention(tab, qt, qit, wit, k, ki, vt):
    batch, nch, w, qb = qt.shape
    seq = nch * qb
    topk = min(TOPK_MAX, seq // 4)
    per_block = lambda rows: pl.BlockSpec((1, 1, rows, qb), lambda b, i: (b, i, 0, 0))
    return pl.pallas_call(
        functools.partial(_attn_body, topk),
        grid=(batch, nch),
        in_specs=[
            pl.BlockSpec(memory_space=pltpu.SMEM),
            per_block(w), per_block(LANES), per_block(SUBLANES),
            pl.BlockSpec((1, seq, w), lambda b, i: (b, 0, 0)),
            pl.BlockSpec((1, seq, LANES), lambda b, i: (b, 0, 0)),
            pl.BlockSpec((1, nch, w, qb), lambda b, i: (b, 0, 0, 0)),
        ],
        out_specs=pl.BlockSpec((1, qb, w), lambda b, i: (b, i, 0)),
        out_shape=jax.ShapeDtypeStruct((batch, seq, w), BF16),
        scratch_shapes=[
            pltpu.VMEM((nch + 1, qb, qb), F32),
            pltpu.VMEM((nch + 1, qb, qb), F32),
            pltpu.VMEM((nch + 1, qb, qb), BF16),
            pltpu.VMEM((ATT_HEADS, qb, qb), F32),
            pltpu.VMEM((ATT_HEADS, qb, qb), F32),
            pltpu.VMEM((qb, qb), BF16),
            pltpu.VMEM((ATT_HEADS, w, qb), BF16),
            pltpu.VMEM((IDX_HEADS, LANES, qb), BF16),
            pltpu.VMEM((w, qb), F32),
            pltpu.VMEM((ATT_HEADS, qb, qb), F32),
            pltpu.VMEM((ATT_HEADS, qb, qb), F32),
        ],
        compiler_params=pltpu.CompilerParams(
            dimension_semantics=("arbitrary", "arbitrary"), vmem_limit_bytes=VMEM_LIMIT_BYTES),
        name="sparse_attention",
    )(tab, qt, qit, wit, k, ki, vt)


def _back_body(x_ref, bg_ref, bp_ref, ba_ref, bc_ref, pre_ref, post_ref, wgate_ref, wbr_ref,
               wout_ref, o_ref):
    for r in range(0, x_ref.shape[0], FFN_ROWS):
        rows = slice(r, r + FFN_ROWS)
        x = x_ref[rows, :]
        xn = _rmsnorm(x, pre_ref[...]).astype(BF16)
        y = None
        for n, br in enumerate((bg_ref, bp_ref, ba_ref, bc_ref)):
            term = _sigmoid(_dot(xn, wgate_ref[n])) * _dot(br[rows, :], wbr_ref[n])
            y = term if y is None else y + term
        h = _dot(y.astype(BF16), wout_ref[...])
        o_ref[rows, :] = x + _rmsnorm(h, post_ref[...])


def _back(x, bg, bp, ba, bc, pre_g, post_g, wgate, wbr, wout):
    n, d = x.shape
    w = BRANCH_W
    tm = min(FFN_BLOCK, n)
    assert tm % FFN_ROWS == 0
    full = lambda a: pl.BlockSpec(a.shape, lambda i: (0,) * a.ndim)
    resident = lambda a: pl.BlockSpec(a.shape, lambda i: (0,) * a.ndim,
                                      pipeline_mode=pl.Buffered(1))
    tok = lambda width: pl.BlockSpec((tm, width), lambda i: (i, 0))
    params = (pre_g, post_g, wgate, wbr, wout)
    return pl.pallas_call(
        _back_body,
        grid=(n // tm,),
        in_specs=([tok(d), tok(w), tok(w), tok(w), tok(w), full(pre_g), full(post_g)]
                  + [resident(a) for a in (wgate, wbr, wout)]),
        out_specs=tok(d),
        out_shape=jax.ShapeDtypeStruct((n, d), F32),
        compiler_params=pltpu.CompilerParams(
            dimension_semantics=("arbitrary",), vmem_limit_bytes=VMEM_LIMIT_BYTES),
        name="mixer_back",
    )(x, bg, bp, ba, bc, *params)


def _split_ffn_weights(w_gu, w_down):
    return w_gu.astype(BF16), w_down.astype(BF16)


def _pack_front_weights(w_in):
    d = w_in.shape[0]
    w = BRANCH_W
    o = 0
    uv = w_in[:, o:o + 2 * w]; o += 2 * w
    pz = w_in[:, o:o + w]; o += w
    qz = w_in[:, o:o + w]; o += w
    kz = w_in[:, o:o + w]; o += w
    vz = w_in[:, o:o + w]; o += w
    qi = w_in[:, o:o + IDX_HEADS * IDX_HD]; o += IDX_HEADS * IDX_HD
    ki = w_in[:, o:o + IDX_HD]; o += IDX_HD
    wi = w_in[:, o:o + IDX_HEADS]; o += IDX_HEADS
    cz = w_in[:, o:o + 2 * w]; o += 2 * w
    gz = w_in[:, o:]
    w1 = jnp.concatenate([uv, pz, kz, cz] + [ki] * (LANES // IDX_HD), axis=1).astype(BF16)
    wi_pad = jnp.pad(wi, ((0, 0), (0, _W2_ROWS - _WIT0 - IDX_HEADS)))
    w2t = jnp.concatenate([qz, vz, qi, wi_pad], axis=1).T.astype(BF16)
    wgate = gz.reshape(d, N_BRANCH, d).transpose(1, 0, 2).astype(BF16)
    return w1, w2t, wgate


def _block_diag(pw):
    g, c, _ = pw.shape
    out = jnp.zeros((g * c, g * c), pw.dtype)
    for k in range(g):
        out = out.at[k * c:(k + 1) * c, k * c:(k + 1) * c].set(pw[k])
    return out


def kernel(x, ffn1_pre_g, ffn1_post_g, ffn1_w_gu, ffn1_w_down, mix_pre_g, mix_post_g, w_in,
           gm_v_g, gm_ws, gm_b, pool_w, pool_scale, conv_dw, conv_b, conv_ln_g, conv_ln_b,
           w_branch, w_out, ffn2_pre_g, ffn2_post_g, ffn2_w_gu, ffn2_w_down, rel_bias):
    batch, seq, d = x.shape
    depth = w_in.shape[0]
    n = batch * seq
    tm = min(FRONT_BLOCK, seq)
    w = BRANCH_W
    row = lambda a: a.reshape(1, -1)
    tab = rel_bias.reshape(-1)

    xf = x.reshape(n, d)
    for l in range(depth):
        xf = _ffn(xf, row(ffn1_pre_g[l]), row(ffn1_post_g[l]),
                  *_split_ffn_weights(ffn1_w_gu[l], ffn1_w_down[l]))

        w1, w2t, wgate = _pack_front_weights(w_in[l])
        ws_cat = gm_ws[l].transpose(1, 0, 2).reshape(GM_CHUNK, GM_GROUPS * GM_CHUNK)
        gmb2d = jnp.repeat(gm_b[l].T, w // GM_GROUPS, axis=1)
        dw = jnp.pad(conv_dw[l], ((0, 1), (0, 0)))
        bg, bp, bc, k, ki, qt, vt, qit, wit = _front(
            xf, seq, row(mix_pre_g[l]), w1, w2t, row(gm_v_g[l]), ws_cat, gmb2d,
            _block_diag(pool_w[l]).astype(BF16), row(pool_scale[l]), dw, row(conv_b[l]),
            row(conv_ln_g[l]), row(conv_ln_b[l]), tm)
        ba = _attention(tab, qt, qit, wit, k.reshape(batch, seq, w),
                        ki.reshape(batch, seq, LANES), vt)
        xf = _back(xf, bg, bp, ba.reshape(n, w), bc, row(mix_pre_g[l]), row(mix_post_g[l]),
                   wgate, w_branch[l].astype(BF16), w_out[l].astype(BF16))

        xf = _ffn(xf, row(ffn2_pre_g[l]), row(ffn2_post_g[l]),
                  *_split_ffn_weights(ffn2_w_gu[l], ffn2_w_down[l]))
    return xf.reshape(batch, seq, d)
```

```python
import functools
import math

import jax
import jax.numpy as jnp
from jax import lax
from jax.experimental import pallas as pl
from jax.experimental.pallas import tpu as pltpu

F32 = jnp.float32
BF16 = jnp.bfloat16

EPS = 1e-6
BRANCH_W = 256
N_BRANCH = 4
GM_GROUPS = 4
GM_CHUNK = 128
POOL_WINDOWS = (2, 4, 8, 16)
ATT_HEADS = 4
ATT_HD = 64
IDX_HEADS = 4
IDX_HD = 32
TOPK_MAX = 256
REL_BUCKETS = 32
REL_MAX_DIST = 128
CONV_K = 31

VMEM_LIMIT_BYTES = 56 * 1024 * 1024
LANES = 128
SUBLANES = 8

FFN_CHUNK = 256
FFN_BLOCK = 1024
FFN_ROWS = 512
FRONT_BLOCK = 1024
FRONT_ROWS = 512
CONV_HALO = 32
POOL_HALO = 16
CONV_ROWS = 64
ATT_BLOCK = 256
ATTEND_UNROLL = 4
COARSE_STEPS = 10
BISECT_STEPS = 8
BF16_ROWS = 16
SORT_GROUP = 8
BF16_INTERVAL = 2.0 ** -6
TINY = 1e-30
LOG2E = math.log2(math.e)
Q_SCALE = ATT_HD ** -0.5 * LOG2E
NEG = -1e30
M_INIT = -1e29


def _rmsnorm(x, g):
    return x * lax.rsqrt(jnp.mean(x * x, axis=-1, keepdims=True) + EPS) * g


def _sigmoid(x):
    return 1.0 / (1.0 + jnp.exp(-x))


def _dot(a, b):
    return jnp.dot(a, b, preferred_element_type=F32)


def _fold8(x, op):
    return _fold(x, op, SUBLANES)


def _fold(x, op, rows):
    parts = [x[j * rows:(j + 1) * rows] for j in range(x.shape[0] // rows)]
    while len(parts) > 1:
        nxt = [op(parts[j], parts[j + 1]) for j in range(0, len(parts) - 1, 2)]
        if len(parts) % 2:
            nxt.append(parts[-1])
        parts = nxt
    return parts[0]


def _sorting_network(n):
    net = []

    def merge(lo, n, r):
        step = r * 2
        if step < n:
            merge(lo, n, step)
            merge(lo + r, n, step)
            net.extend((i, i + r) for i in range(lo + r, lo + n - r, step))
        else:
            net.append((lo, lo + r))

    def sort(lo, n):
        if n > 1:
            sort(lo, n // 2)
            sort(lo + n // 2, n // 2)
            merge(lo, n, 1)

    sort(0, n)
    return tuple(net)


def _sort_desc(vals):
    vals = list(vals)
    for a, b in _sorting_network(len(vals)):
        vals[a], vals[b] = jnp.maximum(vals[a], vals[b]), jnp.minimum(vals[a], vals[b])
    return vals


def _pick(masks, cands):
    for m in reversed(masks):
        cands = [jnp.where(m, cands[2 * i + 1], cands[2 * i]) for i in range(len(cands) // 2)]
    return cands[0]


def _search_sorted(s, thr):
    g = len(s)
    ge, pivot = [], None
    for k in range(g.bit_length() - 1):
        pivot = _pick(ge, [s[v * (g >> k) + (g >> (k + 1)) - 1] for v in range(1 << k)])
        ge.append(pivot >= thr)
    return ge, pivot


def _count_ge_sorted(s, thr, dtype):
    g = len(s)
    c = lambda v: jnp.full((), v, dtype)
    ge, _ = _search_sorted(s, thr)
    terms = [jnp.where(m, c(g >> (l + 1)), c(0)) for l, m in enumerate(ge)]
    terms.append(jnp.where(s[g - 1] >= thr, c(1), c(0)))
    while len(terms) > 1:
        terms = [terms[i] + terms[i + 1] for i in range(0, len(terms) - 1, 2)] + (
            [terms[-1]] if len(terms) % 2 else [])
    return terms[0]


def _max_below_sorted(s, thr):
    g = len(s)
    ge, pivot = _search_sorted(s, thr)
    after = [s[2 * v + 1] for v in range(g // 2)]
    after[-1] = jnp.where(s[g - 1] < thr, s[g - 1], -jnp.inf)
    return jnp.where(ge[-1], _pick(ge[:-1], after), pivot)


def _ffn_body(x_ref, pre_ref, post_ref, wgu_ref, wd_ref, o_ref, xn_ref, acc_ref):
    f = wd_ref.shape[0]
    tm = x_ref.shape[0]
    halves = [slice(r, r + FFN_ROWS) for r in range(0, tm, FFN_ROWS)]
    for rows in halves:
        xn_ref[rows, :] = _rmsnorm(x_ref[rows, :], pre_ref[...]).astype(BF16)
    for c in range(0, f, FFN_CHUNK):
        for rows in halves:
            xn = xn_ref[rows, :]
            a = _dot(xn, wgu_ref[:, c:c + FFN_CHUNK])
            b = _dot(xn, wgu_ref[:, f + c:f + c + FFN_CHUNK])
            hm = (a * _sigmoid(a) * b).astype(BF16)
            down = _dot(hm, wd_ref[c:c + FFN_CHUNK, :])
            acc_ref[rows, :] = down if c == 0 else acc_ref[rows, :] + down
    for rows in halves:
        o_ref[rows, :] = x_ref[rows, :] + 0.5 * _rmsnorm(acc_ref[rows, :], post_ref[...])


def _ffn(x, pre_g, post_g, wgu, wd):
    n, d = x.shape
    f = wd.shape[0]
    tm = min(FFN_BLOCK, n)
    assert f % FFN_CHUNK == 0 and wgu.shape == (d, 2 * f) and tm % FFN_ROWS == 0
    full = lambda shape: pl.BlockSpec(shape, lambda i: (0,) * len(shape))
    resident = lambda shape: pl.BlockSpec(shape, lambda i: (0,) * len(shape),
                                          pipeline_mode=pl.Buffered(1))
    return pl.pallas_call(
        _ffn_body,
        grid=(n // tm,),
        in_specs=[
            pl.BlockSpec((tm, d), lambda i: (i, 0)),
            full((1, d)), full((1, d)), resident((d, 2 * f)), resident((f, d)),
        ],
        out_specs=pl.BlockSpec((tm, d), lambda i: (i, 0)),
        out_shape=jax.ShapeDtypeStruct((n, d), F32),
        scratch_shapes=[pltpu.VMEM((tm, d), BF16), pltpu.VMEM((tm, d), F32)],
        compiler_params=pltpu.CompilerParams(
            dimension_semantics=("arbitrary",), vmem_limit_bytes=VMEM_LIMIT_BYTES),
        name="ffn",
    )(x, pre_g, post_g, wgu, wd)


_UV0, _PZ0, _K0, _CZ0, _KI0, _W1_COLS = 0, 512, 768, 1024, 1536, 1664
_QT0, _VT0, _QIT0, _WIT0, _W2_ROWS = 0, 256, 512, 640, 656


def _front_body(blocks_per_seq,
                x_ref, pre_ref, w1_ref, w2t_ref, gmvg_ref, ws_ref, gmb_ref, poolw_ref,
                pscale_ref, dw_ref, cb_ref, lng_ref, lnb_ref,
                bg_ref, bp_ref, bc_ref, k_ref, ki_ref, qt_ref, vt_ref, qit_ref, wit_ref,
                hbuf, pbuf):
    tm_step = x_ref.shape[0]
    tm = min(FRONT_ROWS, tm_step)
    w = BRANCH_W
    j = pl.program_id(0) % blocks_per_seq

    @pl.when(j == 0)
    def _():
        hbuf[...] = jnp.zeros_like(hbuf)
        pbuf[0:POOL_HALO, :] = jnp.zeros((POOL_HALO, w), F32)

    row = lax.broadcasted_iota(jnp.int32, (GM_CHUNK, GM_GROUPS * GM_CHUNK), 0)
    col = lax.broadcasted_iota(jnp.int32, (GM_CHUNK, GM_GROUPS * GM_CHUNK), 1)
    wsm = jnp.where((col % GM_CHUNK) <= row, ws_ref[...], 0.0).astype(BF16)
    lane_group = lax.broadcasted_iota(jnp.int32, (GM_CHUNK, w), 1) // (w // GM_GROUPS)
    lane_win = lax.broadcasted_iota(jnp.int32, (tm, w), 1) // (w // len(POOL_WINDOWS))

    for r0 in range(0, tm_step, tm):
        rows = slice(r0, r0 + tm)
        xn = _rmsnorm(x_ref[rows, :], pre_ref[...]).astype(BF16)

        k_ref[rows, :] = _dot(xn, w1_ref[:, _K0:_K0 + w]).astype(BF16)
        ki_ref[rows, :] = _dot(xn, w1_ref[:, _KI0:_KI0 + LANES]).astype(BF16)
        zt = lax.dot_general(w2t_ref[...], xn, (((1,), (1,)), ((), ())),
                             preferred_element_type=F32)
        for cc in range(tm // ATT_BLOCK):
            sl = slice(cc * ATT_BLOCK, (cc + 1) * ATT_BLOCK)
            oc = r0 // ATT_BLOCK + cc
            qt_ref[0, oc] = (zt[_QT0:_QT0 + w, sl] * Q_SCALE).astype(BF16)
            vt_ref[0, oc] = zt[_VT0:_VT0 + w, sl].astype(BF16)
            qit_ref[0, oc] = zt[_QIT0:_QIT0 + LANES, sl].astype(BF16)
            wit_ref[0, oc] = zt[_WIT0:_WIT0 + SUBLANES, sl]

        uv = _dot(xn, w1_ref[:, _UV0:_UV0 + 2 * w])
        guv = uv * (0.5 * (1.0 + jnp.tanh(math.sqrt(2.0 / math.pi) * (uv + 0.044715 * (uv ** 3)))))
        u = guv[:, 0:w]
        vv = _rmsnorm(guv[:, w:2 * w], gmvg_ref[...])
        for c in range(tm // GM_CHUNK):
            rs = slice(c * GM_CHUNK, (c + 1) * GM_CHUNK)
            vc = vv[rs, :]
            stacked = jnp.concatenate(
                [jnp.where(lane_group == g, vc, 0.0) for g in range(GM_GROUPS)], axis=0).astype(BF16)
            mixed = _dot(wsm, stacked) + gmb_ref[...]
            bg_ref[r0 + c * GM_CHUNK:r0 + (c + 1) * GM_CHUNK, :] = (u[rs, :] * mixed).astype(BF16)

        p = _dot(xn, w1_ref[:, _PZ0:_PZ0 + w])
        pbuf[POOL_HALO + r0:POOL_HALO + r0 + tm, :] = p
        pos1 = (j * tm_step + r0 + lax.broadcasted_iota(jnp.int32, (tm, w), 0) + 1).astype(F32)
        run = p
        win = jnp.zeros((tm, w), F32)
        cnt = jnp.zeros((tm, w), F32)
        shift = 1
        for g, wlen in enumerate(POOL_WINDOWS):
            while shift < wlen:
                run = run + pbuf[POOL_HALO + r0 - shift:POOL_HALO + r0 - shift + tm, :]
                shift += 1
            win = jnp.where(lane_win == g, run, win)
            cnt = jnp.where(lane_win == g, jnp.minimum(pos1, float(wlen)), cnt)
        dpool = (win / cnt - p).astype(BF16)
        bp_ref[rows, :] = (_dot(dpool, poolw_ref[...]) * pscale_ref[...]).astype(BF16)

        cz = _dot(xn, w1_ref[:, _CZ0:_CZ0 + 2 * w])
        hbuf[CONV_HALO + r0:CONV_HALO + r0 + tm, :] = cz[:, 0:w] * _sigmoid(cz[:, w:2 * w])
        lead = CONV_HALO - (CONV_K - 1)
        for r in range(r0, r0 + tm, CONV_ROWS):
            acc = None
            for shift in range(SUBLANES):
                part = None
                for t in range(CONV_K):
                    if (lead + t) % SUBLANES == shift:
                        start = r + lead + t - shift
                        term = hbuf[start:start + CONV_ROWS + SUBLANES, :] * dw_ref[t:t + 1, :]
                        part = term if part is None else part + term
                part = part[shift:shift + CONV_ROWS, :]
                acc = part if acc is None else acc + part
            hc = acc + cb_ref[...]
            mu = jnp.mean(hc, axis=-1, keepdims=True)
            xc = hc - mu
            yn = xc * lax.rsqrt(jnp.mean(xc * xc, axis=-1, keepdims=True) + EPS)
            yn = yn * lng_ref[...] + lnb_ref[...]
            bc_ref[r:r + CONV_ROWS, :] = (yn * _sigmoid(yn)).astype(BF16)

    pbuf[0:POOL_HALO, :] = pbuf[tm_step:tm_step + POOL_HALO, :]
    hbuf[0:CONV_HALO, :] = hbuf[tm_step:tm_step + CONV_HALO, :]


def _front(x, seq, pre_g, w1, w2t, gmvg, ws_cat, gmb2d, poolw, pscale, dw, cb, lng, lnb, tm):
    n, d = x.shape
    batch = n // seq
    bps = seq // tm
    w = BRANCH_W
    nch = seq // ATT_BLOCK
    cpb = tm // ATT_BLOCK
    full = lambda a: pl.BlockSpec(a.shape, lambda i: (0,) * a.ndim)
    tok = lambda width: pl.BlockSpec((tm, width), lambda i: (i, 0))
    chunked = lambda rows: pl.BlockSpec((1, cpb, rows, ATT_BLOCK),
                                        lambda i: (i // bps, i % bps, 0, 0))
    params = (pre_g, w1, w2t, gmvg, ws_cat, gmb2d, poolw, pscale, dw, cb, lng, lnb)
    out_shape = (
        jax.ShapeDtypeStruct((n, w), BF16),
        jax.ShapeDtypeStruct((n, w), BF16),
        jax.ShapeDtypeStruct((n, w), BF16),
        jax.ShapeDtypeStruct((n, w), BF16),
        jax.ShapeDtypeStruct((n, LANES), BF16),
        jax.ShapeDtypeStruct((batch, nch, w, ATT_BLOCK), BF16),
        jax.ShapeDtypeStruct((batch, nch, w, ATT_BLOCK), BF16),
        jax.ShapeDtypeStruct((batch, nch, LANES, ATT_BLOCK), BF16),
        jax.ShapeDtypeStruct((batch, nch, SUBLANES, ATT_BLOCK), F32),
    )
    out_specs = (tok(w), tok(w), tok(w), tok(w), tok(LANES),
                 chunked(w), chunked(w), chunked(LANES), chunked(SUBLANES))
    return pl.pallas_call(
        functools.partial(_front_body, bps),
        grid=(n // tm,),
        in_specs=[tok(d)] + [full(a) for a in params],
        out_specs=out_specs,
        out_shape=out_shape,
        scratch_shapes=[pltpu.VMEM((tm + CONV_HALO + SUBLANES, w), F32),
                        pltpu.VMEM((tm + POOL_HALO, w), F32)],
        compiler_params=pltpu.CompilerParams(
            dimension_semantics=("arbitrary",), vmem_limit_bytes=VMEM_LIMIT_BYTES),
        name="mixer_front",
    )(x, *params)


def _attn_body(topk,
               tab_ref, qt_ref, qit_ref, wit_ref, k_ref, ki_ref, vt_ref, o_ref,
               s_ref, ssort_ref, s16sort_ref, bd_ref, bp_ref, tri_ref, qm_ref, qim_ref, acc_ref,
               lg0_ref, lg1_ref):
    qb = ATT_BLOCK
    nch = vt_ref.shape[1]
    b = pl.program_id(0)
    i = pl.program_id(1)
    key = lax.broadcasted_iota(jnp.int32, (qb, qb), 0)
    qry = lax.broadcasted_iota(jnp.int32, (qb, qb), 1)

    @pl.when((b == 0) & (i == 0))
    def _():
        tri_ref[...] = jnp.where(qry <= key, 1.0, 0.0).astype(BF16)
        max_exact = REL_BUCKETS // 2
        for ref, off in ((bd_ref, 0), (bp_ref, qb)):
            n = jnp.maximum(qry - key + off, 0)
            large = max_exact + (
                jnp.log(jnp.maximum(n, 1).astype(F32) / max_exact)
                / math.log(REL_MAX_DIST / max_exact) * (REL_BUCKETS - max_exact)).astype(jnp.int32)
            bucket = jnp.where(n < max_exact, n, jnp.minimum(large, REL_BUCKETS - 1))
            for h in range(ATT_HEADS):
                bias = jnp.zeros((qb, qb), F32)
                for k in range(REL_BUCKETS):
                    bias = jnp.where(bucket == k, tab_ref[k * ATT_HEADS + h], bias)
                ref[h] = (bias - tab_ref[(REL_BUCKETS - 1) * ATT_HEADS + h]) * LOG2E

    qt = qt_ref[0, 0]
    qit = qit_ref[0, 0]
    q_head = lax.broadcasted_iota(jnp.int32, qt.shape, 0) // ATT_HD
    qi_head = lax.broadcasted_iota(jnp.int32, qit.shape, 0) // IDX_HD
    for h in range(ATT_HEADS):
        qm_ref[h] = jnp.where(q_head == h, qt, jnp.zeros_like(qt))
    for h in range(IDX_HEADS):
        qim_ref[h] = jnp.where(qi_head == h, qit, jnp.zeros_like(qit))
    wv = wit_ref[0, 0] * ((IDX_HEADS ** -0.5) * (IDX_HD ** -0.5))

    t_pos = i * qb + qry
    n_pairs = (i + 2) // 2

    def keys_of(c):
        return pl.ds(pl.multiple_of(c * qb, qb), qb)

    def score_chunk(c, masked):
        kic = ki_ref[0, keys_of(jnp.minimum(c, nch - 1)), :]
        sc = jnp.zeros((qb, qb), F32)
        for h in range(IDX_HEADS):
            sc = sc + wv[h:h + 1, :] * jnp.maximum(_dot(kic, qim_ref[h]), 0.0)
        if masked:
            valid = (c * qb + key) <= t_pos
            bot = _fold8(jnp.where(valid, sc, jnp.inf), jnp.minimum)
            sc = jnp.where(valid, sc, -jnp.inf)
        s_ref[c] = sc
        slabs = [sc[r:r + SUBLANES] for r in range(0, qb, SUBLANES)]
        groups = [_sort_desc(slabs[g:g + SORT_GROUP]) for g in range(0, len(slabs), SORT_GROUP)]
        ssort_ref[c] = jnp.concatenate([slab for grp in groups for slab in grp], axis=0)
        s16sort_ref[c] = jnp.concatenate(
            [slab for g in range(0, len(groups), 2) for pair in zip(groups[g], groups[g + 1])
             for slab in pair], axis=0).astype(BF16)
        top = functools.reduce(jnp.maximum, [grp[0] for grp in groups])
        if not masked:
            bot = functools.reduce(jnp.minimum, [grp[-1] for grp in groups])
        return top, bot

    def score_chunks(first, count, carry, masked):
        top, bot = carry
        for j in range(count):
            top_j, bot_j = score_chunk(first + j, masked)
            top, bot = jnp.maximum(top, top_j), jnp.minimum(bot, bot_j)
        return top, bot

    quads = (n_pairs - 1) // 2
    carry = lax.fori_loop(
        0, quads, lambda g, cr: score_chunks(4 * g, 4, cr, False),
        (jnp.full((SUBLANES, qb), -jnp.inf, F32), jnp.full((SUBLANES, qb), jnp.inf, F32)))
    carry = lax.fori_loop(
        2 * quads, n_pairs - 1, lambda p, cr: score_chunks(2 * p, 2, cr, False), carry)
    top, bot = score_chunks(2 * (n_pairs - 1), 2, carry, True)
    rmin = jnp.min(bot, axis=0, keepdims=True)
    rmax = jnp.max(top, axis=0, keepdims=True)
    cap16 = rmax.astype(BF16).astype(F32)

    n_valid = (i * qb + lax.broadcasted_iota(jnp.int32, (1, qb), 1) + 1).astype(F32)
    kp = jnp.minimum(float(topk), n_valid)

    def count_sorted(x, thr, rows, dtype):
        counts = [_count_ge_sorted([x[r + j * rows:r + (j + 1) * rows] for j in range(SORT_GROUP)],
                                   thr, dtype)
                  for r in range(0, qb, SORT_GROUP * rows)]
        return functools.reduce(jnp.add, counts)

    def count_ge(thr):
        def body(p, acc):
            return acc + (count_sorted(ssort_ref[2 * p], thr, SUBLANES, F32)
                          + count_sorted(ssort_ref[2 * p + 1], thr, SUBLANES, F32))
        acc = lax.fori_loop(0, n_pairs, body, jnp.zeros((SUBLANES, qb), F32))
        return jnp.sum(acc, axis=0, keepdims=True)

    def count_ge16(thr16):
        def body(p, acc):
            both = (count_sorted(s16sort_ref[2 * p], thr16, BF16_ROWS, BF16)
                    + count_sorted(s16sort_ref[2 * p + 1], thr16, BF16_ROWS, BF16))
            return acc + both.astype(F32)
        acc = lax.fori_loop(0, n_pairs, body, jnp.zeros((BF16_ROWS, qb), F32))
        return jnp.sum(acc, axis=0, keepdims=True)

    def coarse(_, carry):
        lo, hi = carry
        mid16 = (0.5 * lo + 0.5 * jnp.minimum(hi, cap16)).astype(BF16)
        feas = count_ge16(mid16) >= kp
        mid = mid16.astype(F32)
        return jnp.where(feas, mid, lo), jnp.where(feas, hi, mid)

    lo16, hi = lax.fori_loop(0, COARSE_STEPS, coarse,
                             (rmin.astype(BF16).astype(F32), jnp.full((1, qb), jnp.inf, F32)))
    lo = lo16 - (jnp.abs(lo16) * BF16_INTERVAL + TINY)
    chi = count_ge(hi)

    def bisect(_, carry):
        lo, hi, chi = carry
        mid = 0.5 * lo + 0.5 * jnp.minimum(hi, rmax)
        cnt = count_ge(mid)
        feas = cnt >= kp
        return (jnp.where(feas, mid, lo), jnp.where(feas, hi, mid), jnp.where(feas, chi, cnt))

    lo, hi, chi = lax.fori_loop(0, BISECT_STEPS, bisect, (lo, hi, chi))

    def max_below(thr):
        def body(p, acc):
            for c in (2 * p, 2 * p + 1):
                x = ssort_ref[c]
                for r in range(0, qb, SORT_GROUP * SUBLANES):
                    group = [x[r + j * SUBLANES:r + (j + 1) * SUBLANES] for j in range(SORT_GROUP)]
                    acc = jnp.maximum(acc, _max_below_sorted(group, thr))
            return acc
        acc = lax.fori_loop(0, n_pairs, body, jnp.full((SUBLANES, qb), -jnp.inf, F32))
        return jnp.max(acc, axis=0, keepdims=True)

    def finish_cond(state):
        return state[4] > 0.0

    def finish_body(state):
        hi, chi, tau, done, _ = state
        m = max_below(hi)
        cnt = count_ge(m)
        feas = cnt >= kp
        active = done < 0.5
        tau = jnp.where(active & feas, m, tau)
        hi = jnp.where(active & (~feas), m, hi)
        chi = jnp.where(active & (~feas), cnt, chi)
        done = jnp.where(active & feas, 1.0, done)
        return hi, chi, tau, done, jnp.max(1.0 - done)

    hi, chi, tau, _, _ = lax.while_loop(
        finish_cond, finish_body,
        (hi, chi, lo, jnp.zeros((1, qb), F32), jnp.float32(1.0)))
    need = kp - chi

    acc_ref[...] = jnp.zeros_like(acc_ref)
    lg1_ref[...] = jnp.full(lg1_ref.shape, NEG, F32)
    heads = range(ATT_HEADS)

    def stage_a(c, bias, live, buf, need_left):
        x = s_ref[c]
        eq = jnp.where(x == tau, 1.0, 0.0)
        if live is not None:
            eq = jnp.where(live, eq, 0.0)
        prefix = _dot(tri_ref[...], eq.astype(BF16))
        maskadd = jnp.where(x >= jnp.where(prefix <= need_left, tau, hi), 0.0, NEG)
        if live is not None:
            maskadd = jnp.where(live, maskadd, NEG)
        kc = k_ref[0, keys_of(c), :]
        mxs = []
        for h in heads:
            lg = _dot(kc, qm_ref[h]) + maskadd
            if bias is not None:
                lg = lg + bias[h]
            buf[h] = lg
            mxs.append(jnp.max(_fold8(lg, jnp.maximum), axis=0, keepdims=True))
        return tuple(mxs), need_left - prefix[qb - 1:qb, :]

    ones_rows = jnp.ones((2 * SUBLANES, qb), BF16)

    def stage_b(c, buf, mxs, ms, ls):
        vtc = vt_ref[0, c]
        new_ms = [jnp.maximum(ms[h], mxs[h]) for h in heads]
        alphas = [jnp.exp2(ms[h] - new_ms[h]) for h in heads]
        pvs = [_dot(jnp.concatenate([vtc[h * ATT_HD:(h + 1) * ATT_HD, :], ones_rows], axis=0),
                    jnp.exp2(buf[h] - new_ms[h]).astype(BF16)) for h in heads]
        new_ls = [alphas[h] * ls[h] + pvs[h][ATT_HD:ATT_HD + 1, :] for h in heads]
        pv = jnp.concatenate([pvs[h][0:ATT_HD, :] for h in heads], axis=0)
        alpha_rows = jnp.concatenate(
            [jnp.broadcast_to(alphas[h], (ATT_HD, qb)) for h in heads], axis=0)
        acc_ref[...] = acc_ref[...] * alpha_rows + pv
        return tuple(new_ms), tuple(new_ls)

    lg_refs = (lg0_ref, lg1_ref)

    def run_stages(stages, carry):
        ms, ls, need_left, pend_mx, pend_c = carry
        for k, (c, bias, live) in enumerate(stages):
            mx, need_left = stage_a(c, bias, live, lg_refs[k % 2], need_left)
            ms, ls = stage_b(pend_c, lg_refs[1 - k % 2], pend_mx, ms, ls)
            pend_mx, pend_c = mx, c
        return ms, ls, need_left, pend_mx, pend_c

    n_far = jnp.maximum(i - 1, 0)
    m_init = tuple(jnp.full((1, qb), M_INIT, F32) for _ in heads)
    carry = (m_init, tuple(jnp.zeros((1, qb), F32) for _ in heads), need, m_init, jnp.int32(0))
    carry = lax.fori_loop(
        0, n_far // ATTEND_UNROLL,
        lambda g, cr: run_stages([(ATTEND_UNROLL * g + j, None, None)
                                  for j in range(ATTEND_UNROLL)], cr),
        carry)
    far_done = (n_far // ATTEND_UNROLL) * ATTEND_UNROLL

    def tail(n_left):
        def run(cr):
            stages = [(far_done + j, None, None) for j in range(n_left)]
            stages += [(jnp.maximum(i - 1, 0), bp_ref, i >= 1), (i, bd_ref, None)]
            ms, ls, _, pend_mx, pend_c = run_stages(stages, cr)
            return stage_b(pend_c, lg_refs[1 - len(stages) % 2], pend_mx, ms, ls)[1]
        return run

    ls = lax.switch(n_far - far_done, [tail(r) for r in range(ATTEND_UNROLL)], carry)

    for h in range(ATT_HEADS):
        rows = slice(h * ATT_HD, (h + 1) * ATT_HD)
        acc_ref[rows, :] = acc_ref[rows, :] / ls[h]
    o_ref[0] = acc_ref[...].T.astype(BF16)


def _attention(tab, qt, qit, wit, k, ki, vt):
    batch, nch, w, qb = qt.shape
    seq = nch * qb
    topk = min(TOPK_MAX, seq // 4)
    per_block = lambda rows: pl.BlockSpec((1, 1, rows, qb), lambda b, i: (b, i, 0, 0))
    return pl.pallas_call(
        functools.partial(_attn_body, topk),
        grid=(batch, nch),
        in_specs=[
            pl.BlockSpec(memory_space=pltpu.SMEM),
            per_block(w), per_block(LANES), per_block(SUBLANES),
            pl.BlockSpec((1, seq, w), lambda b, i: (b, 0, 0)),
            pl.BlockSpec((1, seq, LANES), lambda b, i: (b, 0, 0)),
            pl.BlockSpec((1, nch, w, qb), lambda b, i: (b, 0, 0, 0)),
        ],
        out_specs=pl.BlockSpec((1, qb, w), lambda b, i: (b, i, 0)),
        out_shape=jax.ShapeDtypeStruct((batch, seq, w), BF16),
        scratch_shapes=[
            pltpu.VMEM((nch + 1, qb, qb), F32),
            pltpu.VMEM((nch + 1, qb, qb), F32),
            pltpu.VMEM((nch + 1, qb, qb), BF16),
            pltpu.VMEM((ATT_HEADS, qb, qb), F32),
            pltpu.VMEM((ATT_HEADS, qb, qb), F32),
            pltpu.VMEM((qb, qb), BF16),
            pltpu.VMEM((ATT_HEADS, w, qb), BF16),
            pltpu.VMEM((IDX_HEADS, LANES, qb), BF16),
            pltpu.VMEM((w, qb), F32),
            pltpu.VMEM((ATT_HEADS, qb, qb), F32),
            pltpu.VMEM((ATT_HEADS, qb, qb), F32),
        ],
        compiler_params=pltpu.CompilerParams(
            dimension_semantics=("arbitrary", "arbitrary"), vmem_limit_bytes=VMEM_LIMIT_BYTES),
        name="sparse_attention",
    )(tab, qt, qit, wit, k, ki, vt)


def _back_body(x_ref, bg_ref, bp_ref, ba_ref, bc_ref, pre_ref, post_ref, wgate_ref, wbr_ref,
               wout_ref, o_ref):
    d_model = x_ref.shape[1]
    for r in range(0, x_ref.shape[0], FFN_ROWS):
        rows = slice(r, r + FFN_ROWS)
        x = x_ref[rows, :]
        xn = _rmsnorm(x, pre_ref[...]).astype(BF16)
        y = None
        for n, br in enumerate((bg_ref, bp_ref, ba_ref, bc_ref)):
            gate = _sigmoid(_dot(xn, wgate_ref[:, n * d_model:(n + 1) * d_model]))
            term = gate * _dot(br[rows, :], wbr_ref[n])
            y = term if y is None else y + term
        h = _dot(y.astype(BF16), wout_ref[...])
        o_ref[rows, :] = x + _rmsnorm(h, post_ref[...])


def _back(x, bg, bp, ba, bc, pre_g, post_g, wgate, wbr, wout):
    n, d = x.shape
    w = BRANCH_W
    tm = min(FFN_BLOCK, n)
    assert tm % FFN_ROWS == 0
    full = lambda a: pl.BlockSpec(a.shape, lambda i: (0,) * a.ndim)
    resident = lambda a: pl.BlockSpec(a.shape, lambda i: (0,) * a.ndim,
                                      pipeline_mode=pl.Buffered(1))
    tok = lambda width: pl.BlockSpec((tm, width), lambda i: (i, 0))
    params = (pre_g, post_g, wgate, wbr, wout)
    return pl.pallas_call(
        _back_body,
        grid=(n // tm,),
        in_specs=([tok(d), tok(w), tok(w), tok(w), tok(w), full(pre_g), full(post_g)]
                  + [resident(a) for a in (wgate, wbr, wout)]),
        out_specs=tok(d),
        out_shape=jax.ShapeDtypeStruct((n, d), F32),
        compiler_params=pltpu.CompilerParams(
            dimension_semantics=("arbitrary",), vmem_limit_bytes=VMEM_LIMIT_BYTES),
        name="mixer_back",
    )(x, bg, bp, ba, bc, *params)


def _split_ffn_weights(w_gu, w_down):
    return w_gu.astype(BF16), w_down.astype(BF16)


def _pack_front_weights(w_in):
    d = w_in.shape[0]
    w = BRANCH_W
    o = 0
    uv = w_in[:, o:o + 2 * w]; o += 2 * w
    pz = w_in[:, o:o + w]; o += w
    qz = w_in[:, o:o + w]; o += w
    kz = w_in[:, o:o + w]; o += w
    vz = w_in[:, o:o + w]; o += w
    qi = w_in[:, o:o + IDX_HEADS * IDX_HD]; o += IDX_HEADS * IDX_HD
    ki = w_in[:, o:o + IDX_HD]; o += IDX_HD
    wi = w_in[:, o:o + IDX_HEADS]; o += IDX_HEADS
    cz = w_in[:, o:o + 2 * w]; o += 2 * w
    gz = w_in[:, o:]
    w1 = jnp.concatenate([uv, pz, kz, cz] + [ki] * (LANES // IDX_HD), axis=1).astype(BF16)
    wi_pad = jnp.pad(wi, ((0, 0), (0, _W2_ROWS - _WIT0 - IDX_HEADS)))
    w2t = jnp.concatenate([qz, vz, qi, wi_pad], axis=1).T.astype(BF16)
    return w1, w2t, gz.astype(BF16)


def _block_diag(pw):
    g, c, _ = pw.shape
    out = jnp.zeros((g * c, g * c), pw.dtype)
    for k in range(g):
        out = out.at[k * c:(k + 1) * c, k * c:(k + 1) * c].set(pw[k])
    return out


def kernel(x, ffn1_pre_g, ffn1_post_g, ffn1_w_gu, ffn1_w_down, mix_pre_g, mix_post_g, w_in,
           gm_v_g, gm_ws, gm_b, pool_w, pool_scale, conv_dw, conv_b, conv_ln_g, conv_ln_b,
           w_branch, w_out, ffn2_pre_g, ffn2_post_g, ffn2_w_gu, ffn2_w_down, rel_bias):
    batch, seq, d = x.shape
    depth = w_in.shape[0]
    n = batch * seq
    tm = min(FRONT_BLOCK, seq)
    w = BRANCH_W
    row = lambda a: a.reshape(1, -1)
    tab = rel_bias.reshape(-1)

    xf = x.reshape(n, d)
    for l in range(depth):
        xf = _ffn(xf, row(ffn1_pre_g[l]), row(ffn1_post_g[l]),
                  *_split_ffn_weights(ffn1_w_gu[l], ffn1_w_down[l]))

        w1, w2t, wgate = _pack_front_weights(w_in[l])
        ws_cat = gm_ws[l].transpose(1, 0, 2).reshape(GM_CHUNK, GM_GROUPS * GM_CHUNK)
        gmb2d = jnp.repeat(gm_b[l].T, w // GM_GROUPS, axis=1)
        dw = jnp.pad(conv_dw[l], ((0, 1), (0, 0)))
        bg, bp, bc, k, ki, qt, vt, qit, wit = _front(
            xf, seq, row(mix_pre_g[l]), w1, w2t, row(gm_v_g[l]), ws_cat, gmb2d,
            _block_diag(pool_w[l]).astype(BF16), row(pool_scale[l]), dw, row(conv_b[l]),
            row(conv_ln_g[l]), row(conv_ln_b[l]), tm)
        ba = _attention(tab, qt, qit, wit, k.reshape(batch, seq, w),
                        ki.reshape(batch, seq, LANES), vt)
        xf = _back(xf, bg, bp, ba.reshape(n, w), bc, row(mix_pre_g[l]), row(mix_post_g[l]),
                   wgate, w_branch[l].astype(BF16), w_out[l].astype(BF16))

        xf = _ffn(xf, row(ffn2_pre_g[l]), row(ffn2_post_g[l]),
                  *_split_ffn_weights(ffn2_w_gu[l], ffn2_w_down[l]))
    return xf.reshape(batch, seq, d)
```

```python
import functools
import math

import jax
import jax.numpy as jnp
from jax import lax
from jax.experimental import pallas as pl
from jax.experimental.pallas import tpu as pltpu

F32 = jnp.float32
BF16 = jnp.bfloat16

EPS = 1e-6
BRANCH_W = 256
N_BRANCH = 4
GM_GROUPS = 4
GM_CHUNK = 128
POOL_WINDOWS = (2, 4, 8, 16)
ATT_HEADS = 4
ATT_HD = 64
IDX_HEADS = 4
IDX_HD = 32
TOPK_MAX = 256
REL_BUCKETS = 32
REL_MAX_DIST = 128
CONV_K = 31

VMEM_LIMIT_BYTES = 56 * 1024 * 1024
LANES = 128
SUBLANES = 8

FFN_CHUNK = 256
FFN_BLOCK = 1024
FFN_ROWS = 512
CAST_ROWS = 128
BACK_COLS = 256
FRONT_BLOCK = 1024
FRONT_ROWS = 512
CONV_HALO = 32
POOL_HALO = 16
CONV_ROWS = 64
ATT_BLOCK = 256
ATTEND_UNROLL = 4
COARSE_STEPS = 10
BISECT_STEPS = 8
BF16_ROWS = 16
SORT_GROUP = 8
BF16_INTERVAL = 2.0 ** -6
TINY = 1e-30
LOG2E = math.log2(math.e)
Q_SCALE = ATT_HD ** -0.5 * LOG2E
NEG = -1e30
M_INIT = -1e29


def _rmsnorm(x, g):
    return x * lax.rsqrt(jnp.mean(x * x, axis=-1, keepdims=True) + EPS) * g


def _sigmoid(x):
    return 1.0 / (1.0 + jnp.exp(-x))


def _dot(a, b):
    return jnp.dot(a, b, preferred_element_type=F32)


def _fold8(x, op):
    return _fold(x, op, SUBLANES)


def _fold(x, op, rows):
    parts = [x[j * rows:(j + 1) * rows] for j in range(x.shape[0] // rows)]
    while len(parts) > 1:
        nxt = [op(parts[j], parts[j + 1]) for j in range(0, len(parts) - 1, 2)]
        if len(parts) % 2:
            nxt.append(parts[-1])
        parts = nxt
    return parts[0]


def _sorting_network(n):
    net = []

    def merge(lo, n, r):
        step = r * 2
        if step < n:
            merge(lo, n, step)
            merge(lo + r, n, step)
            net.extend((i, i + r) for i in range(lo + r, lo + n - r, step))
        else:
            net.append((lo, lo + r))

    def sort(lo, n):
        if n > 1:
            sort(lo, n // 2)
            sort(lo + n // 2, n // 2)
            merge(lo, n, 1)

    sort(0, n)
    return tuple(net)


def _sort_desc(vals):
    vals = list(vals)
    for a, b in _sorting_network(len(vals)):
        vals[a], vals[b] = jnp.maximum(vals[a], vals[b]), jnp.minimum(vals[a], vals[b])
    return vals


def _pick(masks, cands):
    for m in reversed(masks):
        cands = [jnp.where(m, cands[2 * i + 1], cands[2 * i]) for i in range(len(cands) // 2)]
    return cands[0]


def _search_sorted(s, thr):
    g = len(s)
    ge, pivot = [], None
    for k in range(g.bit_length() - 1):
        pivot = _pick(ge, [s[v * (g >> k) + (g >> (k + 1)) - 1] for v in range(1 << k)])
        ge.append(pivot >= thr)
    return ge, pivot


def _count_ge_sorted(s, thr, dtype):
    g = len(s)
    c = lambda v: jnp.full((), v, dtype)
    ge, _ = _search_sorted(s, thr)
    terms = [jnp.where(m, c(g >> (l + 1)), c(0)) for l, m in enumerate(ge)]
    terms.append(jnp.where(s[g - 1] >= thr, c(1), c(0)))
    while len(terms) > 1:
        terms = [terms[i] + terms[i + 1] for i in range(0, len(terms) - 1, 2)] + (
            [terms[-1]] if len(terms) % 2 else [])
    return terms[0]


def _max_below_sorted(s, thr):
    g = len(s)
    ge, pivot = _search_sorted(s, thr)
    after = [s[2 * v + 1] for v in range(g // 2)]
    after[-1] = jnp.where(s[g - 1] < thr, s[g - 1], -jnp.inf)
    return jnp.where(ge[-1], _pick(ge[:-1], after), pivot)


def _ffn_body(x_ref, pre_ref, post_ref, wgu_ref, wd_ref, o_ref, xn_ref, acc_ref):
    f = wd_ref.shape[0]
    tm = x_ref.shape[0]
    halves = [slice(r, r + FFN_ROWS) for r in range(0, tm, FFN_ROWS)]
    for rows in halves:
        xn_ref[rows, :] = _rmsnorm(x_ref[rows, :], pre_ref[...]).astype(BF16)
    for c in range(0, f, FFN_CHUNK):
        for rows in halves:
            xn = xn_ref[rows, :]
            a = _dot(xn, wgu_ref[:, c:c + FFN_CHUNK])
            b = _dot(xn, wgu_ref[:, f + c:f + c + FFN_CHUNK])
            hm = (a * _sigmoid(a) * b).astype(BF16)
            down = _dot(hm, wd_ref[c:c + FFN_CHUNK, :])
            acc_ref[rows, :] = down if c == 0 else acc_ref[rows, :] + down
    for rows in halves:
        o_ref[rows, :] = x_ref[rows, :] + 0.5 * _rmsnorm(acc_ref[rows, :], post_ref[...])


def _ffn(x, pre_g, post_g, wgu, wd, layer):
    n, d = x.shape
    f = wd.shape[1]
    tm = min(FFN_BLOCK, n)
    assert f % FFN_CHUNK == 0 and wgu.shape[1:] == (d, 2 * f) and tm % FFN_ROWS == 0
    full = lambda shape: pl.BlockSpec(shape, lambda i: (0,) * len(shape))
    resident = lambda rows, cols: pl.BlockSpec((None, rows, cols), lambda i: (layer, 0, 0),
                                               pipeline_mode=pl.Buffered(1))
    return pl.pallas_call(
        _ffn_body,
        grid=(n // tm,),
        in_specs=[
            pl.BlockSpec((tm, d), lambda i: (i, 0)),
            full((1, d)), full((1, d)), resident(d, 2 * f), resident(f, d),
        ],
        out_specs=pl.BlockSpec((tm, d), lambda i: (i, 0)),
        out_shape=jax.ShapeDtypeStruct((n, d), F32),
        scratch_shapes=[pltpu.VMEM((tm, d), BF16), pltpu.VMEM((tm, d), F32)],
        compiler_params=pltpu.CompilerParams(
            dimension_semantics=("arbitrary",), vmem_limit_bytes=VMEM_LIMIT_BYTES),
        name="ffn",
    )(x, pre_g, post_g, wgu, wd)


_UV0, _PZ0, _K0, _CZ0, _KI0, _W1_COLS = 0, 512, 768, 1024, 1536, 1664
_QT0, _VT0, _QIT0, _WIT0, _W2_ROWS = 0, 256, 512, 640, 656


def _front_body(blocks_per_seq,
                x_ref, pre_ref, w1_ref, w2t_ref, gmvg_ref, ws_ref, gmb_ref, poolw_ref,
                pscale_ref, dw_ref, cb_ref, lng_ref, lnb_ref,
                bg_ref, bp_ref, bc_ref, k_ref, ki_ref, qt_ref, vt_ref, qit_ref, wit_ref,
                hbuf, pbuf):
    tm_step = x_ref.shape[0]
    tm = min(FRONT_ROWS, tm_step)
    w = BRANCH_W
    j = pl.program_id(0) % blocks_per_seq

    @pl.when(j == 0)
    def _():
        hbuf[...] = jnp.zeros_like(hbuf)
        pbuf[0:POOL_HALO, :] = jnp.zeros((POOL_HALO, w), F32)

    row = lax.broadcasted_iota(jnp.int32, (GM_CHUNK, GM_GROUPS * GM_CHUNK), 0)
    col = lax.broadcasted_iota(jnp.int32, (GM_CHUNK, GM_GROUPS * GM_CHUNK), 1)
    wsm = jnp.where((col % GM_CHUNK) <= row, ws_ref[...], 0.0).astype(BF16)
    lane_group = lax.broadcasted_iota(jnp.int32, (GM_CHUNK, w), 1) // (w // GM_GROUPS)
    lane_win = lax.broadcasted_iota(jnp.int32, (tm, w), 1) // (w // len(POOL_WINDOWS))

    for r0 in range(0, tm_step, tm):
        rows = slice(r0, r0 + tm)
        xn = _rmsnorm(x_ref[rows, :], pre_ref[...]).astype(BF16)

        k_ref[rows, :] = _dot(xn, w1_ref[:, _K0:_K0 + w]).astype(BF16)
        ki_ref[rows, :] = _dot(xn, w1_ref[:, _KI0:_KI0 + LANES]).astype(BF16)
        zt = lax.dot_general(w2t_ref[...], xn, (((1,), (1,)), ((), ())),
                             preferred_element_type=F32)
        for cc in range(tm // ATT_BLOCK):
            sl = slice(cc * ATT_BLOCK, (cc + 1) * ATT_BLOCK)
            oc = r0 // ATT_BLOCK + cc
            qt_ref[0, oc] = (zt[_QT0:_QT0 + w, sl] * Q_SCALE).astype(BF16)
            vt_ref[0, oc] = zt[_VT0:_VT0 + w, sl].astype(BF16)
            qit_ref[0, oc] = zt[_QIT0:_QIT0 + LANES, sl].astype(BF16)
            wit_ref[0, oc] = zt[_WIT0:_WIT0 + SUBLANES, sl]

        uv = _dot(xn, w1_ref[:, _UV0:_UV0 + 2 * w])
        guv = uv * (0.5 * (1.0 + jnp.tanh(math.sqrt(2.0 / math.pi) * (uv + 0.044715 * (uv ** 3)))))
        u = guv[:, 0:w]
        vv = _rmsnorm(guv[:, w:2 * w], gmvg_ref[...])
        for c in range(tm // GM_CHUNK):
            rs = slice(c * GM_CHUNK, (c + 1) * GM_CHUNK)
            vc = vv[rs, :]
            stacked = jnp.concatenate(
                [jnp.where(lane_group == g, vc, 0.0) for g in range(GM_GROUPS)], axis=0).astype(BF16)
            mixed = _dot(wsm, stacked) + gmb_ref[...]
            bg_ref[r0 + c * GM_CHUNK:r0 + (c + 1) * GM_CHUNK, :] = (u[rs, :] * mixed).astype(BF16)

        p = _dot(xn, w1_ref[:, _PZ0:_PZ0 + w])
        pbuf[POOL_HALO + r0:POOL_HALO + r0 + tm, :] = p
        pos1 = (j * tm_step + r0 + lax.broadcasted_iota(jnp.int32, (tm, w), 0) + 1).astype(F32)
        run = p
        win = jnp.zeros((tm, w), F32)
        cnt = jnp.zeros((tm, w), F32)
        shift = 1
        for g, wlen in enumerate(POOL_WINDOWS):
            while shift < wlen:
                run = run + pbuf[POOL_HALO + r0 - shift:POOL_HALO + r0 - shift + tm, :]
                shift += 1
            win = jnp.where(lane_win == g, run, win)
            cnt = jnp.where(lane_win == g, jnp.minimum(pos1, float(wlen)), cnt)
        dpool = (win / cnt - p).astype(BF16)
        bp_ref[rows, :] = (_dot(dpool, poolw_ref[...]) * pscale_ref[...]).astype(BF16)

        cz = _dot(xn, w1_ref[:, _CZ0:_CZ0 + 2 * w])
        hbuf[CONV_HALO + r0:CONV_HALO + r0 + tm, :] = cz[:, 0:w] * _sigmoid(cz[:, w:2 * w])
        lead = CONV_HALO - (CONV_K - 1)
        for r in range(r0, r0 + tm, CONV_ROWS):
            acc = None
            for shift in range(SUBLANES):
                part = None
                for t in range(CONV_K):
                    if (lead + t) % SUBLANES == shift:
                        start = r + lead + t - shift
                        term = hbuf[start:start + CONV_ROWS + SUBLANES, :] * dw_ref[t:t + 1, :]
                        part = term if part is None else part + term
                part = part[shift:shift + CONV_ROWS, :]
                acc = part if acc is None else acc + part
            hc = acc + cb_ref[...]
            mu = jnp.mean(hc, axis=-1, keepdims=True)
            xc = hc - mu
            yn = xc * lax.rsqrt(jnp.mean(xc * xc, axis=-1, keepdims=True) + EPS)
            yn = yn * lng_ref[...] + lnb_ref[...]
            bc_ref[r:r + CONV_ROWS, :] = (yn * _sigmoid(yn)).astype(BF16)

    pbuf[0:POOL_HALO, :] = pbuf[tm_step:tm_step + POOL_HALO, :]
    hbuf[0:CONV_HALO, :] = hbuf[tm_step:tm_step + CONV_HALO, :]


def _front(x, seq, pre_g, w1, w2t, gmvg, ws_cat, gmb2d, poolw, pscale, dw, cb, lng, lnb, tm):
    n, d = x.shape
    batch = n // seq
    bps = seq // tm
    w = BRANCH_W
    nch = seq // ATT_BLOCK
    cpb = tm // ATT_BLOCK
    full = lambda a: pl.BlockSpec(a.shape, lambda i: (0,) * a.ndim)
    tok = lambda width: pl.BlockSpec((tm, width), lambda i: (i, 0))
    chunked = lambda rows: pl.BlockSpec((1, cpb, rows, ATT_BLOCK),
                                        lambda i: (i // bps, i % bps, 0, 0))
    params = (pre_g, w1, w2t, gmvg, ws_cat, gmb2d, poolw, pscale, dw, cb, lng, lnb)
    out_shape = (
        jax.ShapeDtypeStruct((n, w), BF16),
        jax.ShapeDtypeStruct((n, w), BF16),
        jax.ShapeDtypeStruct((n, w), BF16),
        jax.ShapeDtypeStruct((n, w), BF16),
        jax.ShapeDtypeStruct((n, LANES), BF16),
        jax.ShapeDtypeStruct((batch, nch, w, ATT_BLOCK), BF16),
        jax.ShapeDtypeStruct((batch, nch, w, ATT_BLOCK), BF16),
        jax.ShapeDtypeStruct((batch, nch, LANES, ATT_BLOCK), BF16),
        jax.ShapeDtypeStruct((batch, nch, SUBLANES, ATT_BLOCK), F32),
    )
    out_specs = (tok(w), tok(w), tok(w), tok(w), tok(LANES),
                 chunked(w), chunked(w), chunked(LANES), chunked(SUBLANES))
    return pl.pallas_call(
        functools.partial(_front_body, bps),
        grid=(n // tm,),
        in_specs=[tok(d)] + [full(a) for a in params],
        out_specs=out_specs,
        out_shape=out_shape,
        scratch_shapes=[pltpu.VMEM((tm + CONV_HALO + SUBLANES, w), F32),
                        pltpu.VMEM((tm + POOL_HALO, w), F32)],
        compiler_params=pltpu.CompilerParams(
            dimension_semantics=("arbitrary",), vmem_limit_bytes=VMEM_LIMIT_BYTES),
        name="mixer_front",
    )(x, *params)


def _attn_body(topk,
               tab_ref, qt_ref, qit_ref, wit_ref, k_ref, ki_ref, vt_ref, o_ref,
               s_ref, ssort_ref, s16sort_ref, bd_ref, bp_ref, tri_ref, qm_ref, qim_ref, acc_ref,
               lg0_ref, lg1_ref):
    qb = ATT_BLOCK
    nch = vt_ref.shape[1]
    b = pl.program_id(0)
    i = pl.program_id(1)
    key = lax.broadcasted_iota(jnp.int32, (qb, qb), 0)
    qry = lax.broadcasted_iota(jnp.int32, (qb, qb), 1)

    @pl.when((b == 0) & (i == 0))
    def _():
        tri_ref[...] = jnp.where(qry <= key, 1.0, 0.0).astype(BF16)
        max_exact = REL_BUCKETS // 2
        for ref, off in ((bd_ref, 0), (bp_ref, qb)):
            n = jnp.maximum(qry - key + off, 0)
            large = max_exact + (
                jnp.log(jnp.maximum(n, 1).astype(F32) / max_exact)
                / math.log(REL_MAX_DIST / max_exact) * (REL_BUCKETS - max_exact)).astype(jnp.int32)
            bucket = jnp.where(n < max_exact, n, jnp.minimum(large, REL_BUCKETS - 1))
            for h in range(ATT_HEADS):
                bias = jnp.zeros((qb, qb), F32)
                for k in range(REL_BUCKETS):
                    bias = jnp.where(bucket == k, tab_ref[k * ATT_HEADS + h], bias)
                ref[h] = (bias - tab_ref[(REL_BUCKETS - 1) * ATT_HEADS + h]) * LOG2E

    qt = qt_ref[0, 0]
    qit = qit_ref[0, 0]
    q_head = lax.broadcasted_iota(jnp.int32, qt.shape, 0) // ATT_HD
    qi_head = lax.broadcasted_iota(jnp.int32, qit.shape, 0) // IDX_HD
    for h in range(ATT_HEADS):
        qm_ref[h] = jnp.where(q_head == h, qt, jnp.zeros_like(qt))
    for h in range(IDX_HEADS):
        qim_ref[h] = jnp.where(qi_head == h, qit, jnp.zeros_like(qit))
    wv = wit_ref[0, 0] * ((IDX_HEADS ** -0.5) * (IDX_HD ** -0.5))

    t_pos = i * qb + qry
    n_pairs = (i + 2) // 2

    def keys_of(c):
        return pl.ds(pl.multiple_of(c * qb, qb), qb)

    def score_chunk(c, masked):
        kic = ki_ref[0, keys_of(jnp.minimum(c, nch - 1)), :]
        sc = jnp.zeros((qb, qb), F32)
        for h in range(IDX_HEADS):
            sc = sc + wv[h:h + 1, :] * jnp.maximum(_dot(kic, qim_ref[h]), 0.0)
        if masked:
            valid = (c * qb + key) <= t_pos
            bot = _fold8(jnp.where(valid, sc, jnp.inf), jnp.minimum)
            sc = jnp.where(valid, sc, -jnp.inf)
        s_ref[c] = sc
        slabs = [sc[r:r + SUBLANES] for r in range(0, qb, SUBLANES)]
        groups = [_sort_desc(slabs[g:g + SORT_GROUP]) for g in range(0, len(slabs), SORT_GROUP)]
        ssort_ref[c] = jnp.concatenate([slab for grp in groups for slab in grp], axis=0)
        s16sort_ref[c] = jnp.concatenate(
            [slab for g in range(0, len(groups), 2) for pair in zip(groups[g], groups[g + 1])
             for slab in pair], axis=0).astype(BF16)
        top = functools.reduce(jnp.maximum, [grp[0] for grp in groups])
        if not masked:
            bot = functools.reduce(jnp.minimum, [grp[-1] for grp in groups])
        return top, bot

    def score_chunks(first, count, carry, masked):
        top, bot = carry
        for j in range(count):
            top_j, bot_j = score_chunk(first + j, masked)
            top, bot = jnp.maximum(top, top_j), jnp.minimum(bot, bot_j)
        return top, bot

    quads = (n_pairs - 1) // 2
    carry = lax.fori_loop(
        0, quads, lambda g, cr: score_chunks(4 * g, 4, cr, False),
        (jnp.full((SUBLANES, qb), -jnp.inf, F32), jnp.full((SUBLANES, qb), jnp.inf, F32)))
    carry = lax.fori_loop(
        2 * quads, n_pairs - 1, lambda p, cr: score_chunks(2 * p, 2, cr, False), carry)
    last = 2 * (n_pairs - 1)
    carry = score_chunks(last, 1, carry, True)

    def after_diagonal(cr):
        ssort_ref[last + 1] = jnp.full((qb, qb), -jnp.inf, F32)
        s16sort_ref[last + 1] = jnp.full((qb, qb), -jnp.inf, BF16)
        return cr

    top, bot = lax.cond(i % 2 == 1, lambda cr: score_chunks(last + 1, 1, cr, True),
                        after_diagonal, carry)
    rmin = jnp.min(bot, axis=0, keepdims=True)
    rmax = jnp.max(top, axis=0, keepdims=True)
    cap16 = rmax.astype(BF16).astype(F32)

    n_valid = (i * qb + lax.broadcasted_iota(jnp.int32, (1, qb), 1) + 1).astype(F32)
    kp = jnp.minimum(float(topk), n_valid)

    def count_sorted(x, thr, rows, dtype):
        counts = [_count_ge_sorted([x[r + j * rows:r + (j + 1) * rows] for j in range(SORT_GROUP)],
                                   thr, dtype)
                  for r in range(0, qb, SORT_GROUP * rows)]
        return functools.reduce(jnp.add, counts)

    def count_ge(thr):
        def body(p, acc):
            return acc + (count_sorted(ssort_ref[2 * p], thr, SUBLANES, F32)
                          + count_sorted(ssort_ref[2 * p + 1], thr, SUBLANES, F32))
        acc = lax.fori_loop(0, n_pairs, body, jnp.zeros((SUBLANES, qb), F32))
        return jnp.sum(acc, axis=0, keepdims=True)

    def count_ge16(thr16):
        def body(p, acc):
            both = (count_sorted(s16sort_ref[2 * p], thr16, BF16_ROWS, BF16)
                    + count_sorted(s16sort_ref[2 * p + 1], thr16, BF16_ROWS, BF16))
            return acc + both.astype(F32)
        acc = lax.fori_loop(0, n_pairs, body, jnp.zeros((BF16_ROWS, qb), F32))
        return jnp.sum(acc, axis=0, keepdims=True)

    def coarse(_, carry):
        lo, hi = carry
        mid16 = (0.5 * lo + 0.5 * jnp.minimum(hi, cap16)).astype(BF16)
        feas = count_ge16(mid16) >= kp
        mid = mid16.astype(F32)
        return jnp.where(feas, mid, lo), jnp.where(feas, hi, mid)

    lo16, hi = lax.fori_loop(0, COARSE_STEPS, coarse,
                             (rmin.astype(BF16).astype(F32), jnp.full((1, qb), jnp.inf, F32)))
    lo = lo16 - (jnp.abs(lo16) * BF16_INTERVAL + TINY)
    chi = count_ge(hi)

    def bisect(_, carry):
        lo, hi, chi = carry
        mid = 0.5 * lo + 0.5 * jnp.minimum(hi, rmax)
        cnt = count_ge(mid)
        feas = cnt >= kp
        return (jnp.where(feas, mid, lo), jnp.where(feas, hi, mid), jnp.where(feas, chi, cnt))

    lo, hi, chi = lax.fori_loop(0, BISECT_STEPS, bisect, (lo, hi, chi))

    def max_below(thr):
        def body(p, acc):
            for c in (2 * p, 2 * p + 1):
                x = ssort_ref[c]
                for r in range(0, qb, SORT_GROUP * SUBLANES):
                    group = [x[r + j * SUBLANES:r + (j + 1) * SUBLANES] for j in range(SORT_GROUP)]
                    acc = jnp.maximum(acc, _max_below_sorted(group, thr))
            return acc
        acc = lax.fori_loop(0, n_pairs, body, jnp.full((SUBLANES, qb), -jnp.inf, F32))
        return jnp.max(acc, axis=0, keepdims=True)

    def finish_cond(state):
        return state[4] > 0.0

    def finish_body(state):
        hi, chi, tau, done, _ = state
        m = max_below(hi)
        cnt = count_ge(m)
        feas = cnt >= kp
        active = done < 0.5
        tau = jnp.where(active & feas, m, tau)
        hi = jnp.where(active & (~feas), m, hi)
        chi = jnp.where(active & (~feas), cnt, chi)
        done = jnp.where(active & feas, 1.0, done)
        return hi, chi, tau, done, jnp.max(1.0 - done)

    hi, chi, tau, _, _ = lax.while_loop(
        finish_cond, finish_body,
        (hi, chi, lo, jnp.zeros((1, qb), F32), jnp.float32(1.0)))
    need = kp - chi

    acc_ref[...] = jnp.zeros_like(acc_ref)
    lg1_ref[...] = jnp.full(lg1_ref.shape, NEG, F32)
    heads = range(ATT_HEADS)

    def stage_a(c, bias, live, buf, need_left):
        x = s_ref[c]
        eq = jnp.where(x == tau, 1.0, 0.0)
        if live is not None:
            eq = jnp.where(live, eq, 0.0)
        prefix = _dot(tri_ref[...], eq.astype(BF16))
        maskadd = jnp.where(x >= jnp.where(prefix <= need_left, tau, hi), 0.0, NEG)
        if live is not None:
            maskadd = jnp.where(live, maskadd, NEG)
        kc = k_ref[0, keys_of(c), :]
        mxs = []
        for h in heads:
            lg = _dot(kc, qm_ref[h]) + maskadd
            if bias is not None:
                lg = lg + bias[h]
            buf[h] = lg
            mxs.append(jnp.max(_fold8(lg, jnp.maximum), axis=0, keepdims=True))
        return tuple(mxs), need_left - prefix[qb - 1:qb, :]

    ones_rows = jnp.ones((2 * SUBLANES, qb), BF16)

    def stage_b(c, buf, mxs, ms, ls):
        vtc = vt_ref[0, c]
        new_ms = [jnp.maximum(ms[h], mxs[h]) for h in heads]
        alphas = [jnp.exp2(ms[h] - new_ms[h]) for h in heads]
        pvs = [_dot(jnp.concatenate([vtc[h * ATT_HD:(h + 1) * ATT_HD, :], ones_rows], axis=0),
                    jnp.exp2(buf[h] - new_ms[h]).astype(BF16)) for h in heads]
        new_ls = [alphas[h] * ls[h] + pvs[h][ATT_HD:ATT_HD + 1, :] for h in heads]
        pv = jnp.concatenate([pvs[h][0:ATT_HD, :] for h in heads], axis=0)
        alpha_rows = jnp.concatenate(
            [jnp.broadcast_to(alphas[h], (ATT_HD, qb)) for h in heads], axis=0)
        acc_ref[...] = acc_ref[...] * alpha_rows + pv
        return tuple(new_ms), tuple(new_ls)

    lg_refs = (lg0_ref, lg1_ref)

    def run_stages(stages, carry):
        ms, ls, need_left, pend_mx, pend_c = carry
        for k, (c, bias, live) in enumerate(stages):
            mx, need_left = stage_a(c, bias, live, lg_refs[k % 2], need_left)
            ms, ls = stage_b(pend_c, lg_refs[1 - k % 2], pend_mx, ms, ls)
            pend_mx, pend_c = mx, c
        return ms, ls, need_left, pend_mx, pend_c

    n_far = jnp.maximum(i - 1, 0)
    m_init = tuple(jnp.full((1, qb), M_INIT, F32) for _ in heads)
    carry = (m_init, tuple(jnp.zeros((1, qb), F32) for _ in heads), need, m_init, jnp.int32(0))
    carry = lax.fori_loop(
        0, n_far // ATTEND_UNROLL,
        lambda g, cr: run_stages([(ATTEND_UNROLL * g + j, None, None)
                                  for j in range(ATTEND_UNROLL)], cr),
        carry)
    far_done = (n_far // ATTEND_UNROLL) * ATTEND_UNROLL

    def tail(n_left):
        def run(cr):
            stages = [(far_done + j, None, None) for j in range(n_left)]
            stages += [(jnp.maximum(i - 1, 0), bp_ref, i >= 1), (i, bd_ref, None)]
            ms, ls, _, pend_mx, pend_c = run_stages(stages, cr)
            return stage_b(pend_c, lg_refs[1 - len(stages) % 2], pend_mx, ms, ls)[1]
        return run

    ls = lax.switch(n_far - far_done, [tail(r) for r in range(ATTEND_UNROLL)], carry)

    for h in range(ATT_HEADS):
        rows = slice(h * ATT_HD, (h + 1) * ATT_HD)
        acc_ref[rows, :] = acc_ref[rows, :] / ls[h]
    o_ref[0] = acc_ref[...].T.astype(BF16)


def _attention(tab, qt, qit, wit, k, ki, vt):
    batch, nch, w, qb = qt.shape
    seq = nch * qb
    topk = min(TOPK_MAX, seq // 4)
    per_block = lambda rows: pl.BlockSpec((1, 1, rows, qb), lambda b, i: (b, i, 0, 0))
    return pl.pallas_call(
        functools.partial(_attn_body, topk),
        grid=(batch, nch),
        in_specs=[
            pl.BlockSpec(memory_space=pltpu.SMEM),
            per_block(w), per_block(LANES), per_block(SUBLANES),
            pl.BlockSpec((1, seq, w), lambda b, i: (b, 0, 0)),
            pl.BlockSpec((1, seq, LANES), lambda b, i: (b, 0, 0)),
            pl.BlockSpec((1, nch, w, qb), lambda b, i: (b, 0, 0, 0)),
        ],
        out_specs=pl.BlockSpec((1, qb, w), lambda b, i: (b, i, 0)),
        out_shape=jax.ShapeDtypeStruct((batch, seq, w), BF16),
        scratch_shapes=[
            pltpu.VMEM((nch + 1, qb, qb), F32),
            pltpu.VMEM((nch + 1, qb, qb), F32),
            pltpu.VMEM((nch + 1, qb, qb), BF16),
            pltpu.VMEM((ATT_HEADS, qb, qb), F32),
            pltpu.VMEM((ATT_HEADS, qb, qb), F32),
            pltpu.VMEM((qb, qb), BF16),
            pltpu.VMEM((ATT_HEADS, w, qb), BF16),
            pltpu.VMEM((IDX_HEADS, LANES, qb), BF16),
            pltpu.VMEM((w, qb), F32),
            pltpu.VMEM((ATT_HEADS, qb, qb), F32),
            pltpu.VMEM((ATT_HEADS, qb, qb), F32),
        ],
        compiler_params=pltpu.CompilerParams(
            dimension_semantics=("arbitrary", "arbitrary"), vmem_limit_bytes=VMEM_LIMIT_BYTES),
        name="sparse_attention",
    )(tab, qt, qit, wit, k, ki, vt)


def _back_body(x_ref, bg_ref, bp_ref, ba_ref, bc_ref, pre_ref, post_ref, wgate_ref, wbr_ref,
               wout_ref, o_ref):
    for r in range(0, x_ref.shape[0], FFN_ROWS):
        rows = slice(r, r + FFN_ROWS)
        x = x_ref[rows, :]
        xn = _rmsnorm(x, pre_ref[...]).astype(BF16)
        h = None
        for c in range(0, x_ref.shape[1], BACK_COLS):
            cols = slice(c, c + BACK_COLS)
            y = None
            for n, br in enumerate((bg_ref, bp_ref, ba_ref, bc_ref)):
                term = (_sigmoid(_dot(xn, wgate_ref[n, :, cols]))
                        * _dot(br[rows, :], wbr_ref[n, :, cols]))
                y = term if y is None else y + term
            part = _dot(y.astype(BF16), wout_ref[cols, :])
            h = part if h is None else h + part
        o_ref[rows, :] = x + _rmsnorm(h, post_ref[...])


def _back(x, bg, bp, ba, bc, pre_g, post_g, wgate, wbr, wout):
    n, d = x.shape
    w = BRANCH_W
    tm = min(FFN_BLOCK, n)
    assert tm % FFN_ROWS == 0
    full = lambda a: pl.BlockSpec(a.shape, lambda i: (0,) * a.ndim)
    resident = lambda a: pl.BlockSpec(a.shape, lambda i: (0,) * a.ndim,
                                      pipeline_mode=pl.Buffered(1))
    tok = lambda width: pl.BlockSpec((tm, width), lambda i: (i, 0))
    params = (pre_g, post_g, wgate, wbr, wout)
    return pl.pallas_call(
        _back_body,
        grid=(n // tm,),
        in_specs=([tok(d), tok(w), tok(w), tok(w), tok(w), full(pre_g), full(post_g)]
                  + [resident(a) for a in (wgate, wbr, wout)]),
        out_specs=tok(d),
        out_shape=jax.ShapeDtypeStruct((n, d), F32),
        compiler_params=pltpu.CompilerParams(
            dimension_semantics=("arbitrary",), vmem_limit_bytes=VMEM_LIMIT_BYTES),
        name="mixer_back",
    )(x, bg, bp, ba, bc, *params)


def _cast_body(w_ref, o_ref):
    o_ref[...] = w_ref[...].astype(BF16)


def _cast_bf16(w):
    layers, rows, cols = w.shape
    rb = CAST_ROWS
    assert rows % rb == 0
    spec = pl.BlockSpec((1, rb, cols), lambda l, r: (l, r, 0))
    return pl.pallas_call(
        _cast_body,
        grid=(layers, rows // rb),
        in_specs=[spec],
        out_specs=spec,
        out_shape=jax.ShapeDtypeStruct(w.shape, BF16),
        compiler_params=pltpu.CompilerParams(
            dimension_semantics=("arbitrary", "arbitrary"), vmem_limit_bytes=VMEM_LIMIT_BYTES),
        name="cast_weights",
    )(w)


def _pack_front_weights(w_in):
    d = w_in.shape[0]
    w = BRANCH_W
    o = 0
    uv = w_in[:, o:o + 2 * w]; o += 2 * w
    pz = w_in[:, o:o + w]; o += w
    qz = w_in[:, o:o + w]; o += w
    kz = w_in[:, o:o + w]; o += w
    vz = w_in[:, o:o + w]; o += w
    qi = w_in[:, o:o + IDX_HEADS * IDX_HD]; o += IDX_HEADS * IDX_HD
    ki = w_in[:, o:o + IDX_HD]; o += IDX_HD
    wi = w_in[:, o:o + IDX_HEADS]; o += IDX_HEADS
    cz = w_in[:, o:o + 2 * w]; o += 2 * w
    gz = w_in[:, o:]
    w1 = jnp.concatenate([uv, pz, kz, cz] + [ki] * (LANES // IDX_HD), axis=1).astype(BF16)
    wi_pad = jnp.pad(wi, ((0, 0), (0, _W2_ROWS - _WIT0 - IDX_HEADS)))
    w2t = jnp.concatenate([qz, vz, qi, wi_pad], axis=1).T.astype(BF16)
    wgate = gz.reshape(d, N_BRANCH, d).transpose(1, 0, 2).astype(BF16)
    return w1, w2t, wgate


def _block_diag(pw):
    g, c, _ = pw.shape
    out = jnp.zeros((g * c, g * c), pw.dtype)
    for k in range(g):
        out = out.at[k * c:(k + 1) * c, k * c:(k + 1) * c].set(pw[k])
    return out


def kernel(x, ffn1_pre_g, ffn1_post_g, ffn1_w_gu, ffn1_w_down, mix_pre_g, mix_post_g, w_in,
           gm_v_g, gm_ws, gm_b, pool_w, pool_scale, conv_dw, conv_b, conv_ln_g, conv_ln_b,
           w_branch, w_out, ffn2_pre_g, ffn2_post_g, ffn2_w_gu, ffn2_w_down, rel_bias):
    batch, seq, d = x.shape
    depth = w_in.shape[0]
    n = batch * seq
    tm = min(FRONT_BLOCK, seq)
    w = BRANCH_W
    row = lambda a: a.reshape(1, -1)
    tab = rel_bias.reshape(-1)

    ffn1_w = (_cast_bf16(ffn1_w_gu), _cast_bf16(ffn1_w_down))
    ffn2_w = (_cast_bf16(ffn2_w_gu), _cast_bf16(ffn2_w_down))

    xf = x.reshape(n, d)
    for l in range(depth):
        xf = _ffn(xf, row(ffn1_pre_g[l]), row(ffn1_post_g[l]), *ffn1_w, l)

        w1, w2t, wgate = _pack_front_weights(w_in[l])
        ws_cat = gm_ws[l].transpose(1, 0, 2).reshape(GM_CHUNK, GM_GROUPS * GM_CHUNK)
        gmb2d = jnp.repeat(gm_b[l].T, w // GM_GROUPS, axis=1)
        dw = jnp.pad(conv_dw[l], ((0, 1), (0, 0)))
        bg, bp, bc, k, ki, qt, vt, qit, wit = _front(
            xf, seq, row(mix_pre_g[l]), w1, w2t, row(gm_v_g[l]), ws_cat, gmb2d,
            _block_diag(pool_w[l]).astype(BF16), row(pool_scale[l]), dw, row(conv_b[l]),
            row(conv_ln_g[l]), row(conv_ln_b[l]), tm)
        ba = _attention(tab, qt, qit, wit, k.reshape(batch, seq, w),
                        ki.reshape(batch, seq, LANES), vt)
        xf = _back(xf, bg, bp, ba.reshape(n, w), bc, row(mix_pre_g[l]), row(mix_post_g[l]),
                   wgate, w_branch[l].astype(BF16), w_out[l].astype(BF16))

        xf = _ffn(xf, row(ffn2_pre_g[l]), row(ffn2_post_g[l]), *ffn2_w, l)
    return xf.reshape(batch, seq, d)
```

```python
import functools
import math

import jax
import jax.numpy as jnp
from jax import lax
from jax.experimental import pallas as pl
from jax.experimental.pallas import tpu as pltpu

F32 = jnp.float32
BF16 = jnp.bfloat16

EPS = 1e-6
BRANCH_W = 256
N_BRANCH = 4
GM_GROUPS = 4
GM_CHUNK = 128
POOL_WINDOWS = (2, 4, 8, 16)
ATT_HEADS = 4
ATT_HD = 64
IDX_HEADS = 4
IDX_HD = 32
TOPK_MAX = 256
REL_BUCKETS = 32
REL_MAX_DIST = 128
CONV_K = 31

VMEM_LIMIT_BYTES = 56 * 1024 * 1024
LANES = 128
SUBLANES = 8

FFN_CHUNK = 256
FFN_BLOCK = 1024
FFN_ROWS = 512
CAST_ROWS = 128
BACK_COLS = 256
FRONT_BLOCK = 1024
FRONT_ROWS = 512
CONV_HALO = 32
POOL_HALO = 16
CONV_ROWS = 64
ATT_BLOCK = 256
ATTEND_UNROLL = 4
COARSE_STEPS = 10
BISECT_STEPS = 8
BF16_ROWS = 16
SORT_GROUP = 8
BF16_INTERVAL = 2.0 ** -6
TINY = 1e-30
LOG2E = math.log2(math.e)
Q_SCALE = ATT_HD ** -0.5 * LOG2E
NEG = -1e30
M_INIT = -1e29


def _rmsnorm(x, g):
    return x * lax.rsqrt(jnp.mean(x * x, axis=-1, keepdims=True) + EPS) * g


def _sigmoid(x):
    return 1.0 / (1.0 + jnp.exp(-x))


def _gelu_tanh(x):
    return x * (0.5 * (1.0 + jnp.tanh(math.sqrt(2.0 / math.pi) * (x + 0.044715 * (x ** 3)))))


def _dot(a, b):
    return jnp.dot(a, b, preferred_element_type=F32)


def _fold8(x, op):
    return _fold(x, op, SUBLANES)


def _fold(x, op, rows):
    parts = [x[j * rows:(j + 1) * rows] for j in range(x.shape[0] // rows)]
    while len(parts) > 1:
        nxt = [op(parts[j], parts[j + 1]) for j in range(0, len(parts) - 1, 2)]
        if len(parts) % 2:
            nxt.append(parts[-1])
        parts = nxt
    return parts[0]


def _sorting_network(n):
    net = []

    def merge(lo, n, r):
        step = r * 2
        if step < n:
            merge(lo, n, step)
            merge(lo + r, n, step)
            net.extend((i, i + r) for i in range(lo + r, lo + n - r, step))
        else:
            net.append((lo, lo + r))

    def sort(lo, n):
        if n > 1:
            sort(lo, n // 2)
            sort(lo + n // 2, n // 2)
            merge(lo, n, 1)

    sort(0, n)
    return tuple(net)


def _sort_desc(vals):
    vals = list(vals)
    for a, b in _sorting_network(len(vals)):
        vals[a], vals[b] = jnp.maximum(vals[a], vals[b]), jnp.minimum(vals[a], vals[b])
    return vals


def _pick(masks, cands):
    for m in reversed(masks):
        cands = [jnp.where(m, cands[2 * i + 1], cands[2 * i]) for i in range(len(cands) // 2)]
    return cands[0]


def _search_sorted(s, thr):
    g = len(s)
    ge, pivot = [], None
    for k in range(g.bit_length() - 1):
        pivot = _pick(ge, [s[v * (g >> k) + (g >> (k + 1)) - 1] for v in range(1 << k)])
        ge.append(pivot >= thr)
    return ge, pivot


def _count_ge_sorted(s, thr, dtype):
    g = len(s)
    c = lambda v: jnp.full((), v, dtype)
    ge, _ = _search_sorted(s, thr)
    terms = [jnp.where(m, c(g >> (l + 1)), c(0)) for l, m in enumerate(ge)]
    terms.append(jnp.where(s[g - 1] >= thr, c(1), c(0)))
    while len(terms) > 1:
        terms = [terms[i] + terms[i + 1] for i in range(0, len(terms) - 1, 2)] + (
            [terms[-1]] if len(terms) % 2 else [])
    return terms[0]


def _max_below_sorted(s, thr):
    g = len(s)
    ge, pivot = _search_sorted(s, thr)
    after = [s[2 * v + 1] for v in range(g // 2)]
    after[-1] = jnp.where(s[g - 1] < thr, s[g - 1], -jnp.inf)
    return jnp.where(ge[-1], _pick(ge[:-1], after), pivot)


def _ffn_body(x_ref, pre_ref, post_ref, wgu_ref, wd_ref, o_ref, xn_ref, acc_ref):
    f = wd_ref.shape[0]
    tm = x_ref.shape[0]
    halves = [slice(r, r + FFN_ROWS) for r in range(0, tm, FFN_ROWS)]
    for rows in halves:
        xn_ref[rows, :] = _rmsnorm(x_ref[rows, :], pre_ref[...]).astype(BF16)
    for c in range(0, f, FFN_CHUNK):
        for rows in halves:
            xn = xn_ref[rows, :]
            a = _dot(xn, wgu_ref[:, c:c + FFN_CHUNK])
            b = _dot(xn, wgu_ref[:, f + c:f + c + FFN_CHUNK])
            hm = (a * _sigmoid(a) * b).astype(BF16)
            down = _dot(hm, wd_ref[c:c + FFN_CHUNK, :])
            acc_ref[rows, :] = down if c == 0 else acc_ref[rows, :] + down
    for rows in halves:
        o_ref[rows, :] = x_ref[rows, :] + 0.5 * _rmsnorm(acc_ref[rows, :], post_ref[...])


def _ffn(x, pre_g, post_g, wgu, wd, layer):
    n, d = x.shape
    f = wd.shape[1]
    tm = min(FFN_BLOCK, n)
    assert f % FFN_CHUNK == 0 and wgu.shape[1:] == (d, 2 * f) and tm % FFN_ROWS == 0
    full = lambda shape: pl.BlockSpec(shape, lambda i: (0,) * len(shape))
    resident = lambda rows, cols: pl.BlockSpec((None, rows, cols), lambda i: (layer, 0, 0),
                                               pipeline_mode=pl.Buffered(1))
    return pl.pallas_call(
        _ffn_body,
        grid=(n // tm,),
        in_specs=[
            pl.BlockSpec((tm, d), lambda i: (i, 0)),
            full((1, d)), full((1, d)), resident(d, 2 * f), resident(f, d),
        ],
        out_specs=pl.BlockSpec((tm, d), lambda i: (i, 0)),
        out_shape=jax.ShapeDtypeStruct((n, d), F32),
        scratch_shapes=[pltpu.VMEM((tm, d), BF16), pltpu.VMEM((tm, d), F32)],
        compiler_params=pltpu.CompilerParams(
            dimension_semantics=("arbitrary",), vmem_limit_bytes=VMEM_LIMIT_BYTES),
        name="ffn",
    )(x, pre_g, post_g, wgu, wd)


_UV0, _PZ0, _K0, _CZ0, _KI0, _W1_COLS = 0, 512, 768, 1024, 1536, 1664
_QT0, _VT0, _QIT0, _WIT0, _W2_ROWS = 0, 256, 512, 640, 656


def _front_body(blocks_per_seq,
                x_ref, pre_ref, w1_ref, w2t_ref, gmvg_ref, ws_ref, gmb_ref, poolw_ref,
                pscale_ref, dw_ref, cb_ref, lng_ref, lnb_ref,
                bg_ref, bp_ref, bc_ref, k_ref, ki_ref, qt_ref, vt_ref, qit_ref, wit_ref,
                hbuf, pbuf):
    tm_step = x_ref.shape[0]
    tm = min(FRONT_ROWS, tm_step)
    w = BRANCH_W
    j = pl.program_id(0) % blocks_per_seq

    @pl.when(j == 0)
    def _():
        hbuf[...] = jnp.zeros_like(hbuf)
        pbuf[0:POOL_HALO, :] = jnp.zeros((POOL_HALO, w), F32)

    row = lax.broadcasted_iota(jnp.int32, (GM_CHUNK, GM_GROUPS * GM_CHUNK), 0)
    col = lax.broadcasted_iota(jnp.int32, (GM_CHUNK, GM_GROUPS * GM_CHUNK), 1)
    wsm = jnp.where((col % GM_CHUNK) <= row, ws_ref[...], 0.0).astype(BF16)
    lane_group = lax.broadcasted_iota(jnp.int32, (GM_CHUNK, w), 1) // (w // GM_GROUPS)

    for r0 in range(0, tm_step, tm):
        rows = slice(r0, r0 + tm)
        xn = _rmsnorm(x_ref[rows, :], pre_ref[...]).astype(BF16)

        k_ref[rows, :] = _dot(xn, w1_ref[:, _K0:_K0 + w]).astype(BF16)
        ki_ref[rows, :] = _dot(xn, w1_ref[:, _KI0:_KI0 + LANES]).astype(BF16)
        zt = lax.dot_general(w2t_ref[...], xn, (((1,), (1,)), ((), ())),
                             preferred_element_type=F32)
        for cc in range(tm // ATT_BLOCK):
            sl = slice(cc * ATT_BLOCK, (cc + 1) * ATT_BLOCK)
            oc = r0 // ATT_BLOCK + cc
            qt_ref[0, oc] = (zt[_QT0:_QT0 + w, sl] * Q_SCALE).astype(BF16)
            vt_ref[0, oc] = zt[_VT0:_VT0 + w, sl].astype(BF16)
            qit_ref[0, oc] = zt[_QIT0:_QIT0 + LANES, sl].astype(BF16)
            wit_ref[0, oc] = zt[_WIT0:_WIT0 + SUBLANES, sl]

        u = _gelu_tanh(_dot(xn, w1_ref[:, _UV0:_UV0 + w]))
        vv = _rmsnorm(_gelu_tanh(_dot(xn, w1_ref[:, _UV0 + w:_UV0 + 2 * w])), gmvg_ref[...])
        for c in range(tm // GM_CHUNK):
            rs = slice(c * GM_CHUNK, (c + 1) * GM_CHUNK)
            vc = vv[rs, :]
            stacked = jnp.concatenate(
                [jnp.where(lane_group == g, vc, 0.0) for g in range(GM_GROUPS)], axis=0).astype(BF16)
            mixed = _dot(wsm, stacked) + gmb_ref[...]
            bg_ref[r0 + c * GM_CHUNK:r0 + (c + 1) * GM_CHUNK, :] = (u[rs, :] * mixed).astype(BF16)

        p = _dot(xn, w1_ref[:, _PZ0:_PZ0 + w])
        base = POOL_HALO + r0
        pbuf[base:base + tm, :] = p
        pos1 = (j * tm_step + r0 + lax.broadcasted_iota(jnp.int32, (tm, LANES), 0) + 1).astype(F32)
        small = lax.broadcasted_iota(jnp.int32, (tm, LANES), 1) < w // len(POOL_WINDOWS)
        w2, w4, w8, w16 = (float(v) for v in POOL_WINDOWS)
        lo_half = slice(0, LANES)
        s2 = p[:, lo_half] + pbuf[base - 1:base - 1 + tm, lo_half]
        s4 = s2 + pbuf[base - 2:base - 2 + tm, lo_half] + pbuf[base - 3:base - 3 + tm, lo_half]
        d_lo = (jnp.where(small, s2, s4) / jnp.minimum(pos1, jnp.where(small, w2, w4))
                - p[:, lo_half])
        hi_half = slice(LANES, 2 * LANES)
        ext = None
        for shift in range(SUBLANES):
            term = pbuf[base - SUBLANES - shift:base - shift + tm, hi_half]
            ext = term if ext is None else ext + term
        s8 = ext[SUBLANES:, :]
        s16 = s8 + ext[0:tm, :]
        d_hi = (jnp.where(small, s8, s16) / jnp.minimum(pos1, jnp.where(small, w8, w16))
                - p[:, hi_half])
        dpool = jnp.concatenate([d_lo, d_hi], axis=1).astype(BF16)
        bp_ref[rows, :] = (_dot(dpool, poolw_ref[...]) * pscale_ref[...]).astype(BF16)

        hbuf[CONV_HALO + r0:CONV_HALO + r0 + tm, :] = (
            _dot(xn, w1_ref[:, _CZ0:_CZ0 + w]) * _sigmoid(_dot(xn, w1_ref[:, _CZ0 + w:_CZ0 + 2 * w])))
        lead = CONV_HALO - (CONV_K - 1)
        for r in range(r0, r0 + tm, CONV_ROWS):
            acc = None
            for shift in range(SUBLANES):
                part = None
                for t in range(CONV_K):
                    if (lead + t) % SUBLANES == shift:
                        start = r + lead + t - shift
                        term = hbuf[start:start + CONV_ROWS + SUBLANES, :] * dw_ref[t:t + 1, :]
                        part = term if part is None else part + term
                part = part[shift:shift + CONV_ROWS, :]
                acc = part if acc is None else acc + part
            hc = acc + cb_ref[...]
            mu = jnp.mean(hc, axis=-1, keepdims=True)
            xc = hc - mu
            yn = xc * lax.rsqrt(jnp.mean(xc * xc, axis=-1, keepdims=True) + EPS)
            yn = yn * lng_ref[...] + lnb_ref[...]
            bc_ref[r:r + CONV_ROWS, :] = (yn * _sigmoid(yn)).astype(BF16)

    pbuf[0:POOL_HALO, :] = pbuf[tm_step:tm_step + POOL_HALO, :]
    hbuf[0:CONV_HALO, :] = hbuf[tm_step:tm_step + CONV_HALO, :]


def _front(x, seq, pre_g, w1, w2t, gmvg, ws_cat, gmb2d, poolw, pscale, dw, cb, lng, lnb, tm):
    n, d = x.shape
    batch = n // seq
    bps = seq // tm
    w = BRANCH_W
    assert POOL_WINDOWS == (2, 4, 8, 16) and w == 2 * LANES and POOL_HALO >= 2 * SUBLANES - 1
    nch = seq // ATT_BLOCK
    cpb = tm // ATT_BLOCK
    full = lambda a: pl.BlockSpec(a.shape, lambda i: (0,) * a.ndim)
    tok = lambda width: pl.BlockSpec((tm, width), lambda i: (i, 0))
    chunked = lambda rows: pl.BlockSpec((1, cpb, rows, ATT_BLOCK),
                                        lambda i: (i // bps, i % bps, 0, 0))
    params = (pre_g, w1, w2t, gmvg, ws_cat, gmb2d, poolw, pscale, dw, cb, lng, lnb)
    out_shape = (
        jax.ShapeDtypeStruct((n, w), BF16),
        jax.ShapeDtypeStruct((n, w), BF16),
        jax.ShapeDtypeStruct((n, w), BF16),
        jax.ShapeDtypeStruct((n, w), BF16),
        jax.ShapeDtypeStruct((n, LANES), BF16),
        jax.ShapeDtypeStruct((batch, nch, w, ATT_BLOCK), BF16),
        jax.ShapeDtypeStruct((batch, nch, w, ATT_BLOCK), BF16),
        jax.ShapeDtypeStruct((batch, nch, LANES, ATT_BLOCK), BF16),
        jax.ShapeDtypeStruct((batch, nch, SUBLANES, ATT_BLOCK), F32),
    )
    out_specs = (tok(w), tok(w), tok(w), tok(w), tok(LANES),
                 chunked(w), chunked(w), chunked(LANES), chunked(SUBLANES))
    return pl.pallas_call(
        functools.partial(_front_body, bps),
        grid=(n // tm,),
        in_specs=[tok(d)] + [full(a) for a in params],
        out_specs=out_specs,
        out_shape=out_shape,
        scratch_shapes=[pltpu.VMEM((tm + CONV_HALO + SUBLANES, w), F32),
                        pltpu.VMEM((tm + POOL_HALO, w), F32)],
        compiler_params=pltpu.CompilerParams(
            dimension_semantics=("arbitrary",), vmem_limit_bytes=VMEM_LIMIT_BYTES),
        name="mixer_front",
    )(x, *params)


def _attn_body(topk,
               tab_ref, qt_ref, qit_ref, wit_ref, k_ref, ki_ref, vt_ref, o_ref,
               s_ref, ssort_ref, s16sort_ref, bd_ref, bp_ref, tri_ref, qm_ref, qim_ref, acc_ref,
               lg0_ref, lg1_ref):
    qb = ATT_BLOCK
    nch = vt_ref.shape[1]
    b = pl.program_id(0)
    i = pl.program_id(1)
    key = lax.broadcasted_iota(jnp.int32, (qb, qb), 0)
    qry = lax.broadcasted_iota(jnp.int32, (qb, qb), 1)

    @pl.when((b == 0) & (i == 0))
    def _():
        tri_ref[...] = jnp.where(qry <= key, 1.0, 0.0).astype(BF16)
        max_exact = REL_BUCKETS // 2
        for ref, off in ((bd_ref, 0), (bp_ref, qb)):
            n = jnp.maximum(qry - key + off, 0)
            large = max_exact + (
                jnp.log(jnp.maximum(n, 1).astype(F32) / max_exact)
                / math.log(REL_MAX_DIST / max_exact) * (REL_BUCKETS - max_exact)).astype(jnp.int32)
            bucket = jnp.where(n < max_exact, n, jnp.minimum(large, REL_BUCKETS - 1))
            for h in range(ATT_HEADS):
                bias = jnp.zeros((qb, qb), F32)
                for k in range(REL_BUCKETS):
                    bias = jnp.where(bucket == k, tab_ref[k * ATT_HEADS + h], bias)
                ref[h] = (bias - tab_ref[(REL_BUCKETS - 1) * ATT_HEADS + h]) * LOG2E

    qt = qt_ref[0, 0]
    qit = qit_ref[0, 0]
    q_head = lax.broadcasted_iota(jnp.int32, qt.shape, 0) // ATT_HD
    qi_head = lax.broadcasted_iota(jnp.int32, qit.shape, 0) // IDX_HD
    for h in range(ATT_HEADS):
        qm_ref[h] = jnp.where(q_head == h, qt, jnp.zeros_like(qt))
    for h in range(IDX_HEADS):
        qim_ref[h] = jnp.where(qi_head == h, qit, jnp.zeros_like(qit))
    wv = wit_ref[0, 0] * ((IDX_HEADS ** -0.5) * (IDX_HD ** -0.5))

    t_pos = i * qb + qry
    n_pairs = (i + 2) // 2

    def keys_of(c):
        return pl.ds(pl.multiple_of(c * qb, qb), qb)

    def score_chunk(c, masked):
        kic = ki_ref[0, keys_of(jnp.minimum(c, nch - 1)), :]
        sc = jnp.zeros((qb, qb), F32)
        for h in range(IDX_HEADS):
            sc = sc + wv[h:h + 1, :] * jnp.maximum(_dot(kic, qim_ref[h]), 0.0)
        if masked:
            valid = (c * qb + key) <= t_pos
            bot = _fold8(jnp.where(valid, sc, jnp.inf), jnp.minimum)
            sc = jnp.where(valid, sc, -jnp.inf)
        s_ref[c] = sc
        slabs = [sc[r:r + SUBLANES] for r in range(0, qb, SUBLANES)]
        groups = [_sort_desc(slabs[g:g + SORT_GROUP]) for g in range(0, len(slabs), SORT_GROUP)]
        ssort_ref[c] = jnp.concatenate([slab for grp in groups for slab in grp], axis=0)
        s16sort_ref[c] = jnp.concatenate(
            [slab for g in range(0, len(groups), 2) for pair in zip(groups[g], groups[g + 1])
             for slab in pair], axis=0).astype(BF16)
        top = functools.reduce(jnp.maximum, [grp[0] for grp in groups])
        if not masked:
            bot = functools.reduce(jnp.minimum, [grp[-1] for grp in groups])
        return top, bot

    def score_chunks(first, count, carry, masked):
        top, bot = carry
        for j in range(count):
            top_j, bot_j = score_chunk(first + j, masked)
            top, bot = jnp.maximum(top, top_j), jnp.minimum(bot, bot_j)
        return top, bot

    quads = (n_pairs - 1) // 2
    carry = lax.fori_loop(
        0, quads, lambda g, cr: score_chunks(4 * g, 4, cr, False),
        (jnp.full((SUBLANES, qb), -jnp.inf, F32), jnp.full((SUBLANES, qb), jnp.inf, F32)))
    carry = lax.fori_loop(
        2 * quads, n_pairs - 1, lambda p, cr: score_chunks(2 * p, 2, cr, False), carry)
    last = 2 * (n_pairs - 1)
    carry = score_chunks(last, 1, carry, True)

    def after_diagonal(cr):
        ssort_ref[last + 1] = jnp.full((qb, qb), -jnp.inf, F32)
        s16sort_ref[last + 1] = jnp.full((qb, qb), -jnp.inf, BF16)
        return cr

    top, bot = lax.cond(i % 2 == 1, lambda cr: score_chunks(last + 1, 1, cr, True),
                        after_diagonal, carry)
    rmin = jnp.min(bot, axis=0, keepdims=True)
    rmax = jnp.max(top, axis=0, keepdims=True)
    cap16 = rmax.astype(BF16).astype(F32)

    n_valid = (i * qb + lax.broadcasted_iota(jnp.int32, (1, qb), 1) + 1).astype(F32)
    kp = jnp.minimum(float(topk), n_valid)

    def count_sorted(x, thr, rows, dtype):
        counts = [_count_ge_sorted([x[r + j * rows:r + (j + 1) * rows] for j in range(SORT_GROUP)],
                                   thr, dtype)
                  for r in range(0, qb, SORT_GROUP * rows)]
        return functools.reduce(jnp.add, counts)

    def count_ge(thr):
        def body(p, acc):
            return acc + (count_sorted(ssort_ref[2 * p], thr, SUBLANES, F32)
                          + count_sorted(ssort_ref[2 * p + 1], thr, SUBLANES, F32))
        acc = lax.fori_loop(0, n_pairs, body, jnp.zeros((SUBLANES, qb), F32))
        return jnp.sum(acc, axis=0, keepdims=True)

    def count_ge16(thr16):
        def body(p, acc):
            both = (count_sorted(s16sort_ref[2 * p], thr16, BF16_ROWS, BF16)
                    + count_sorted(s16sort_ref[2 * p + 1], thr16, BF16_ROWS, BF16))
            return acc + both.astype(F32)
        acc = lax.fori_loop(0, n_pairs, body, jnp.zeros((BF16_ROWS, qb), F32))
        return jnp.sum(acc, axis=0, keepdims=True)

    def coarse(_, carry):
        lo, hi = carry
        mid16 = (0.5 * lo + 0.5 * jnp.minimum(hi, cap16)).astype(BF16)
        feas = count_ge16(mid16) >= kp
        mid = mid16.astype(F32)
        return jnp.where(feas, mid, lo), jnp.where(feas, hi, mid)

    lo16, hi = lax.fori_loop(0, COARSE_STEPS, coarse,
                             (rmin.astype(BF16).astype(F32), jnp.full((1, qb), jnp.inf, F32)))
    lo = lo16 - (jnp.abs(lo16) * BF16_INTERVAL + TINY)
    chi = count_ge(hi)

    def bisect(_, carry):
        lo, hi, chi = carry
        mid = 0.5 * lo + 0.5 * jnp.minimum(hi, rmax)
        cnt = count_ge(mid)
        feas = cnt >= kp
        return (jnp.where(feas, mid, lo), jnp.where(feas, hi, mid), jnp.where(feas, chi, cnt))

    lo, hi, chi = lax.fori_loop(0, BISECT_STEPS, bisect, (lo, hi, chi))

    def max_below(thr):
        def body(p, acc):
            for c in (2 * p, 2 * p + 1):
                x = ssort_ref[c]
                for r in range(0, qb, SORT_GROUP * SUBLANES):
                    group = [x[r + j * SUBLANES:r + (j + 1) * SUBLANES] for j in range(SORT_GROUP)]
                    acc = jnp.maximum(acc, _max_below_sorted(group, thr))
            return acc
        acc = lax.fori_loop(0, n_pairs, body, jnp.full((SUBLANES, qb), -jnp.inf, F32))
        return jnp.max(acc, axis=0, keepdims=True)

    def finish_cond(state):
        return state[4] > 0.0

    def finish_body(state):
        hi, chi, tau, done, _ = state
        m = max_below(hi)
        cnt = count_ge(m)
        feas = cnt >= kp
        active = done < 0.5
        tau = jnp.where(active & feas, m, tau)
        hi = jnp.where(active & (~feas), m, hi)
        chi = jnp.where(active & (~feas), cnt, chi)
        done = jnp.where(active & feas, 1.0, done)
        return hi, chi, tau, done, jnp.max(1.0 - done)

    hi, chi, tau, _, _ = lax.while_loop(
        finish_cond, finish_body,
        (hi, chi, lo, jnp.zeros((1, qb), F32), jnp.float32(1.0)))
    need = kp - chi

    acc_ref[...] = jnp.zeros_like(acc_ref)
    lg1_ref[...] = jnp.full(lg1_ref.shape, NEG, F32)
    heads = range(ATT_HEADS)

    def stage_a(c, bias, live, buf, need_left):
        x = s_ref[c]
        eq = jnp.where(x == tau, 1.0, 0.0)
        if live is not None:
            eq = jnp.where(live, eq, 0.0)
        prefix = _dot(tri_ref[...], eq.astype(BF16))
        maskadd = jnp.where(x >= jnp.where(prefix <= need_left, tau, hi), 0.0, NEG)
        if live is not None:
            maskadd = jnp.where(live, maskadd, NEG)
        kc = k_ref[0, keys_of(c), :]
        mxs = []
        for h in heads:
            lg = _dot(kc, qm_ref[h]) + maskadd
            if bias is not None:
                lg = lg + bias[h]
            buf[h] = lg
            mxs.append(jnp.max(_fold8(lg, jnp.maximum), axis=0, keepdims=True))
        return tuple(mxs), need_left - prefix[qb - 1:qb, :]

    ones_rows = jnp.ones((2 * SUBLANES, qb), BF16)

    def stage_b(c, buf, mxs, ms, ls):
        vtc = vt_ref[0, c]
        new_ms = [jnp.maximum(ms[h], mxs[h]) for h in heads]
        alphas = [jnp.exp2(ms[h] - new_ms[h]) for h in heads]
        pvs = [_dot(jnp.concatenate([vtc[h * ATT_HD:(h + 1) * ATT_HD, :], ones_rows], axis=0),
                    jnp.exp2(buf[h] - new_ms[h]).astype(BF16)) for h in heads]
        new_ls = [alphas[h] * ls[h] + pvs[h][ATT_HD:ATT_HD + 1, :] for h in heads]
        pv = jnp.concatenate([pvs[h][0:ATT_HD, :] for h in heads], axis=0)
        alpha_rows = jnp.concatenate(
            [jnp.broadcast_to(alphas[h], (ATT_HD, qb)) for h in heads], axis=0)
        acc_ref[...] = acc_ref[...] * alpha_rows + pv
        return tuple(new_ms), tuple(new_ls)

    lg_refs = (lg0_ref, lg1_ref)

    def run_stages(stages, carry):
        ms, ls, need_left, pend_mx, pend_c = carry
        for k, (c, bias, live) in enumerate(stages):
            mx, need_left = stage_a(c, bias, live, lg_refs[k % 2], need_left)
            ms, ls = stage_b(pend_c, lg_refs[1 - k % 2], pend_mx, ms, ls)
            pend_mx, pend_c = mx, c
        return ms, ls, need_left, pend_mx, pend_c

    n_far = jnp.maximum(i - 1, 0)
    m_init = tuple(jnp.full((1, qb), M_INIT, F32) for _ in heads)
    carry = (m_init, tuple(jnp.zeros((1, qb), F32) for _ in heads), need, m_init, jnp.int32(0))
    carry = lax.fori_loop(
        0, n_far // ATTEND_UNROLL,
        lambda g, cr: run_stages([(ATTEND_UNROLL * g + j, None, None)
                                  for j in range(ATTEND_UNROLL)], cr),
        carry)
    far_done = (n_far // ATTEND_UNROLL) * ATTEND_UNROLL

    def tail(n_left):
        def run(cr):
            stages = [(far_done + j, None, None) for j in range(n_left)]
            stages += [(jnp.maximum(i - 1, 0), bp_ref, i >= 1), (i, bd_ref, None)]
            ms, ls, _, pend_mx, pend_c = run_stages(stages, cr)
            return stage_b(pend_c, lg_refs[1 - len(stages) % 2], pend_mx, ms, ls)[1]
        return run

    ls = lax.switch(n_far - far_done, [tail(r) for r in range(ATTEND_UNROLL)], carry)

    for h in range(ATT_HEADS):
        rows = slice(h * ATT_HD, (h + 1) * ATT_HD)
        acc_ref[rows, :] = acc_ref[rows, :] / ls[h]
    o_ref[0] = acc_ref[...].T.astype(BF16)


def _attention(tab, qt, qit, wit, k, ki, vt):
    batch, nch, w, qb = qt.shape
    seq = nch * qb
    topk = min(TOPK_MAX, seq // 4)
    per_block = lambda rows: pl.BlockSpec((1, 1, rows, qb), lambda b, i: (b, i, 0, 0))
    return pl.pallas_call(
        functools.partial(_attn_body, topk),
        grid=(batch, nch),
        in_specs=[
            pl.BlockSpec(memory_space=pltpu.SMEM),
            per_block(w), per_block(LANES), per_block(SUBLANES),
            pl.BlockSpec((1, seq, w), lambda b, i: (b, 0, 0)),
            pl.BlockSpec((1, seq, LANES), lambda b, i: (b, 0, 0)),
            pl.BlockSpec((1, nch, w, qb), lambda b, i: (b, 0, 0, 0)),
        ],
        out_specs=pl.BlockSpec((1, qb, w), lambda b, i: (b, i, 0)),
        out_shape=jax.ShapeDtypeStruct((batch, seq, w), BF16),
        scratch_shapes=[
            pltpu.VMEM((nch + 1, qb, qb), F32),
            pltpu.VMEM((nch + 1, qb, qb), F32),
            pltpu.VMEM((nch + 1, qb, qb), BF16),
            pltpu.VMEM((ATT_HEADS, qb, qb), F32),
            pltpu.VMEM((ATT_HEADS, qb, qb), F32),
            pltpu.VMEM((qb, qb), BF16),
            pltpu.VMEM((ATT_HEADS, w, qb), BF16),
            pltpu.VMEM((IDX_HEADS, LANES, qb), BF16),
            pltpu.VMEM((w, qb), F32),
            pltpu.VMEM((ATT_HEADS, qb, qb), F32),
            pltpu.VMEM((ATT_HEADS, qb, qb), F32),
        ],
        compiler_params=pltpu.CompilerParams(
            dimension_semantics=("arbitrary", "arbitrary"), vmem_limit_bytes=VMEM_LIMIT_BYTES),
        name="sparse_attention",
    )(tab, qt, qit, wit, k, ki, vt)


def _back_body(x_ref, bg_ref, bp_ref, ba_ref, bc_ref, pre_ref, post_ref, wgate_ref, wbr_ref,
               wout_ref, o_ref):
    for r in range(0, x_ref.shape[0], FFN_ROWS):
        rows = slice(r, r + FFN_ROWS)
        x = x_ref[rows, :]
        xn = _rmsnorm(x, pre_ref[...]).astype(BF16)
        h = None
        for c in range(0, x_ref.shape[1], BACK_COLS):
            cols = slice(c, c + BACK_COLS)
            y = None
            for n, br in enumerate((bg_ref, bp_ref, ba_ref, bc_ref)):
                term = (_sigmoid(_dot(xn, wgate_ref[n, :, cols]))
                        * _dot(br[rows, :], wbr_ref[n, :, cols]))
                y = term if y is None else y + term
            part = _dot(y.astype(BF16), wout_ref[cols, :])
            h = part if h is None else h + part
        o_ref[rows, :] = x + _rmsnorm(h, post_ref[...])


def _back(x, bg, bp, ba, bc, pre_g, post_g, wgate, wbr, wout):
    n, d = x.shape
    w = BRANCH_W
    tm = min(FFN_BLOCK, n)
    assert tm % FFN_ROWS == 0
    full = lambda a: pl.BlockSpec(a.shape, lambda i: (0,) * a.ndim)
    resident = lambda a: pl.BlockSpec(a.shape, lambda i: (0,) * a.ndim,
                                      pipeline_mode=pl.Buffered(1))
    tok = lambda width: pl.BlockSpec((tm, width), lambda i: (i, 0))
    params = (pre_g, post_g, wgate, wbr, wout)
    return pl.pallas_call(
        _back_body,
        grid=(n // tm,),
        in_specs=([tok(d), tok(w), tok(w), tok(w), tok(w), full(pre_g), full(post_g)]
                  + [resident(a) for a in (wgate, wbr, wout)]),
        out_specs=tok(d),
        out_shape=jax.ShapeDtypeStruct((n, d), F32),
        compiler_params=pltpu.CompilerParams(
            dimension_semantics=("arbitrary",), vmem_limit_bytes=VMEM_LIMIT_BYTES),
        name="mixer_back",
    )(x, bg, bp, ba, bc, *params)


def _cast_body(w_ref, o_ref):
    o_ref[...] = w_ref[...].astype(BF16)


def _cast_bf16(w):
    layers, rows, cols = w.shape
    rb = CAST_ROWS
    assert rows % rb == 0
    spec = pl.BlockSpec((1, rb, cols), lambda l, r: (l, r, 0))
    return pl.pallas_call(
        _cast_body,
        grid=(layers, rows // rb),
        in_specs=[spec],
        out_specs=spec,
        out_shape=jax.ShapeDtypeStruct(w.shape, BF16),
        compiler_params=pltpu.CompilerParams(
            dimension_semantics=("arbitrary", "arbitrary"), vmem_limit_bytes=VMEM_LIMIT_BYTES),
        name="cast_weights",
    )(w)


def _pack_front_weights(w_in):
    d = w_in.shape[0]
    w = BRANCH_W
    o = 0
    uv = w_in[:, o:o + 2 * w]; o += 2 * w
    pz = w_in[:, o:o + w]; o += w
    qz = w_in[:, o:o + w]; o += w
    kz = w_in[:, o:o + w]; o += w
    vz = w_in[:, o:o + w]; o += w
    qi = w_in[:, o:o + IDX_HEADS * IDX_HD]; o += IDX_HEADS * IDX_HD
    ki = w_in[:, o:o + IDX_HD]; o += IDX_HD
    wi = w_in[:, o:o + IDX_HEADS]; o += IDX_HEADS
    cz = w_in[:, o:o + 2 * w]; o += 2 * w
    gz = w_in[:, o:]
    w1 = jnp.concatenate([uv, pz, kz, cz] + [ki] * (LANES // IDX_HD), axis=1).astype(BF16)
    wi_pad = jnp.pad(wi, ((0, 0), (0, _W2_ROWS - _WIT0 - IDX_HEADS)))
    w2t = jnp.concatenate([qz, vz, qi, wi_pad], axis=1).T.astype(BF16)
    wgate = gz.reshape(d, N_BRANCH, d).transpose(1, 0, 2).astype(BF16)
    return w1, w2t, wgate


def _block_diag(pw):
    g, c, _ = pw.shape
    out = jnp.zeros((g * c, g * c), pw.dtype)
    for k in range(g):
        out = out.at[k * c:(k + 1) * c, k * c:(k + 1) * c].set(pw[k])
    return out


def kernel(x, ffn1_pre_g, ffn1_post_g, ffn1_w_gu, ffn1_w_down, mix_pre_g, mix_post_g, w_in,
           gm_v_g, gm_ws, gm_b, pool_w, pool_scale, conv_dw, conv_b, conv_ln_g, conv_ln_b,
           w_branch, w_out, ffn2_pre_g, ffn2_post_g, ffn2_w_gu, ffn2_w_down, rel_bias):
    batch, seq, d = x.shape
    depth = w_in.shape[0]
    n = batch * seq
    tm = min(FRONT_BLOCK, seq)
    w = BRANCH_W
    row = lambda a: a.reshape(1, -1)
    tab = rel_bias.reshape(-1)

    ffn1_w = (_cast_bf16(ffn1_w_gu), _cast_bf16(ffn1_w_down))
    ffn2_w = (_cast_bf16(ffn2_w_gu), _cast_bf16(ffn2_w_down))

    xf = x.reshape(n, d)
    for l in range(depth):
        xf = _ffn(xf, row(ffn1_pre_g[l]), row(ffn1_post_g[l]), *ffn1_w, l)

        w1, w2t, wgate = _pack_front_weights(w_in[l])
        ws_cat = gm_ws[l].transpose(1, 0, 2).reshape(GM_CHUNK, GM_GROUPS * GM_CHUNK)
        gmb2d = jnp.repeat(gm_b[l].T, w // GM_GROUPS, axis=1)
        dw = jnp.pad(conv_dw[l], ((0, 1), (0, 0)))
        bg, bp, bc, k, ki, qt, vt, qit, wit = _front(
            xf, seq, row(mix_pre_g[l]), w1, w2t, row(gm_v_g[l]), ws_cat, gmb2d,
            _block_diag(pool_w[l]).astype(BF16), row(pool_scale[l]), dw, row(conv_b[l]),
            row(conv_ln_g[l]), row(conv_ln_b[l]), tm)
        ba = _attention(tab, qt, qit, wit, k.reshape(batch, seq, w),
                        ki.reshape(batch, seq, LANES), vt)
        xf = _back(xf, bg, bp, ba.reshape(n, w), bc, row(mix_pre_g[l]), row(mix_post_g[l]),
                   wgate, w_branch[l].astype(BF16), w_out[l].astype(BF16))

        xf = _ffn(xf, row(ffn2_pre_g[l]), row(ffn2_post_g[l]), *ffn2_w, l)
    return xf.reshape(batch, seq, d)
```

```python
import functools
import math

import jax
import jax.numpy as jnp
from jax import lax
from jax.experimental import pallas as pl
from jax.experimental.pallas import tpu as pltpu

F32 = jnp.float32
BF16 = jnp.bfloat16

EPS = 1e-6
BRANCH_W = 256
N_BRANCH = 4
GM_GROUPS = 4
GM_CHUNK = 128
POOL_WINDOWS = (2, 4, 8, 16)
ATT_HEADS = 4
ATT_HD = 64
IDX_HEADS = 4
IDX_HD = 32
TOPK_MAX = 256
REL_BUCKETS = 32
REL_MAX_DIST = 128
CONV_K = 31

VMEM_LIMIT_BYTES = 56 * 1024 * 1024
LANES = 128
SUBLANES = 8

FFN_CHUNK = 256
FFN_BLOCK = 1024
FFN_ROWS = 512
CAST_ROWS = 128
BACK_COLS = 256
FRONT_BLOCK = 1024
FRONT_ROWS = 512
CONV_HALO = 32
POOL_HALO = 16
CONV_ROWS = 128
ATT_BLOCK = 256
ATTEND_UNROLL = 4
COARSE_STEPS = 10
BISECT_STEPS = 8
BF16_ROWS = 16
SORT_GROUP = 8
BF16_INTERVAL = 2.0 ** -6
TINY = 1e-30
LOG2E = math.log2(math.e)
Q_SCALE = ATT_HD ** -0.5 * LOG2E
NEG = -1e30
M_INIT = -1e29


def _rmsnorm(x, g):
    return x * lax.rsqrt(jnp.mean(x * x, axis=-1, keepdims=True) + EPS) * g


def _sigmoid(x):
    return 1.0 / (1.0 + jnp.exp(-x))


def _gelu_tanh(x):
    return x * (0.5 * (1.0 + jnp.tanh(math.sqrt(2.0 / math.pi) * (x + 0.044715 * (x ** 3)))))


def _dot(a, b):
    return jnp.dot(a, b, preferred_element_type=F32)


def _fold8(x, op):
    return _fold(x, op, SUBLANES)


def _fold(x, op, rows):
    parts = [x[j * rows:(j + 1) * rows] for j in range(x.shape[0] // rows)]
    while len(parts) > 1:
        nxt = [op(parts[j], parts[j + 1]) for j in range(0, len(parts) - 1, 2)]
        if len(parts) % 2:
            nxt.append(parts[-1])
        parts = nxt
    return parts[0]


def _sorting_network(n):
    net = []

    def merge(lo, n, r):
        step = r * 2
        if step < n:
            merge(lo, n, step)
            merge(lo + r, n, step)
            net.extend((i, i + r) for i in range(lo + r, lo + n - r, step))
        else:
            net.append((lo, lo + r))

    def sort(lo, n):
        if n > 1:
            sort(lo, n // 2)
            sort(lo + n // 2, n // 2)
            merge(lo, n, 1)

    sort(0, n)
    return tuple(net)


def _sort_desc(vals):
    vals = list(vals)
    for a, b in _sorting_network(len(vals)):
        vals[a], vals[b] = jnp.maximum(vals[a], vals[b]), jnp.minimum(vals[a], vals[b])
    return vals


def _pick(masks, cands):
    for m in reversed(masks):
        cands = [jnp.where(m, cands[2 * i + 1], cands[2 * i]) for i in range(len(cands) // 2)]
    return cands[0]


def _search_sorted(s, thr):
    g = len(s)
    ge, pivot = [], None
    for k in range(g.bit_length() - 1):
        pivot = _pick(ge, [s[v * (g >> k) + (g >> (k + 1)) - 1] for v in range(1 << k)])
        ge.append(pivot >= thr)
    return ge, pivot


def _count_ge_sorted(s, thr, dtype):
    g = len(s)
    c = lambda v: jnp.full((), v, dtype)
    ge, _ = _search_sorted(s, thr)
    terms = [jnp.where(m, c(g >> (l + 1)), c(0)) for l, m in enumerate(ge)]
    terms.append(jnp.where(s[g - 1] >= thr, c(1), c(0)))
    while len(terms) > 1:
        terms = [terms[i] + terms[i + 1] for i in range(0, len(terms) - 1, 2)] + (
            [terms[-1]] if len(terms) % 2 else [])
    return terms[0]


def _max_below_sorted(s, thr):
    g = len(s)
    ge, pivot = _search_sorted(s, thr)
    after = [s[2 * v + 1] for v in range(g // 2)]
    after[-1] = jnp.where(s[g - 1] < thr, s[g - 1], -jnp.inf)
    return jnp.where(ge[-1], _pick(ge[:-1], after), pivot)


def _ffn_body(x_ref, pre_ref, post_ref, wgu_ref, wd_ref, o_ref, xn_ref, acc_ref):
    f = wd_ref.shape[0]
    tm = x_ref.shape[0]
    halves = [slice(r, r + FFN_ROWS) for r in range(0, tm, FFN_ROWS)]
    for rows in halves:
        xn_ref[rows, :] = _rmsnorm(x_ref[rows, :], pre_ref[...]).astype(BF16)
    for c in range(0, f, FFN_CHUNK):
        for rows in halves:
            xn = xn_ref[rows, :]
            a = _dot(xn, wgu_ref[:, c:c + FFN_CHUNK])
            b = _dot(xn, wgu_ref[:, f + c:f + c + FFN_CHUNK])
            hm = (a * _sigmoid(a) * b).astype(BF16)
            down = _dot(hm, wd_ref[c:c + FFN_CHUNK, :])
            acc_ref[rows, :] = down if c == 0 else acc_ref[rows, :] + down
    for rows in halves:
        o_ref[rows, :] = x_ref[rows, :] + 0.5 * _rmsnorm(acc_ref[rows, :], post_ref[...])


def _ffn(x, pre_g, post_g, wgu, wd, layer):
    n, d = x.shape
    f = wd.shape[1]
    tm = min(FFN_BLOCK, n)
    assert f % FFN_CHUNK == 0 and wgu.shape[1:] == (d, 2 * f) and tm % FFN_ROWS == 0
    full = lambda shape: pl.BlockSpec(shape, lambda i: (0,) * len(shape))
    resident = lambda rows, cols: pl.BlockSpec((None, rows, cols), lambda i: (layer, 0, 0),
                                               pipeline_mode=pl.Buffered(1))
    return pl.pallas_call(
        _ffn_body,
        grid=(n // tm,),
        in_specs=[
            pl.BlockSpec((tm, d), lambda i: (i, 0)),
            full((1, d)), full((1, d)), resident(d, 2 * f), resident(f, d),
        ],
        out_specs=pl.BlockSpec((tm, d), lambda i: (i, 0)),
        out_shape=jax.ShapeDtypeStruct((n, d), F32),
        scratch_shapes=[pltpu.VMEM((tm, d), BF16), pltpu.VMEM((tm, d), F32)],
        compiler_params=pltpu.CompilerParams(
            dimension_semantics=("arbitrary",), vmem_limit_bytes=VMEM_LIMIT_BYTES),
        name="ffn",
    )(x, pre_g, post_g, wgu, wd)


_UV0, _PZ0, _K0, _CZ0, _KI0, _W1_COLS = 0, 512, 768, 1024, 1536, 1664
_QT0, _VT0, _QIT0, _WIT0, _W2_ROWS = 0, 256, 512, 640, 656


def _front_body(blocks_per_seq,
                x_ref, pre_ref, w1_ref, w2t_ref, gmvg_ref, ws_ref, gmb_ref, poolw_ref,
                pscale_ref, dw_ref, cb_ref, lng_ref, lnb_ref,
                bg_ref, bp_ref, bc_ref, k_ref, ki_ref, qt_ref, vt_ref, qit_ref, wit_ref,
                hbuf, pbuf):
    tm_step = x_ref.shape[0]
    tm = min(FRONT_ROWS, tm_step)
    w = BRANCH_W
    j = pl.program_id(0) % blocks_per_seq

    @pl.when(j == 0)
    def _():
        hbuf[...] = jnp.zeros_like(hbuf)
        pbuf[0:POOL_HALO, :] = jnp.zeros((POOL_HALO, w), F32)

    row = lax.broadcasted_iota(jnp.int32, (GM_CHUNK, GM_GROUPS * GM_CHUNK), 0)
    col = lax.broadcasted_iota(jnp.int32, (GM_CHUNK, GM_GROUPS * GM_CHUNK), 1)
    wsm = jnp.where((col % GM_CHUNK) <= row, ws_ref[...], 0.0).astype(BF16)
    lane_group = lax.broadcasted_iota(jnp.int32, (GM_CHUNK, w), 1) // (w // GM_GROUPS)

    for r0 in range(0, tm_step, tm):
        rows = slice(r0, r0 + tm)
        xn = _rmsnorm(x_ref[rows, :], pre_ref[...]).astype(BF16)

        k_ref[rows, :] = _dot(xn, w1_ref[:, _K0:_K0 + w]).astype(BF16)
        ki_ref[rows, :] = _dot(xn, w1_ref[:, _KI0:_KI0 + LANES]).astype(BF16)
        zt = lax.dot_general(w2t_ref[...], xn, (((1,), (1,)), ((), ())),
                             preferred_element_type=F32)
        for cc in range(tm // ATT_BLOCK):
            sl = slice(cc * ATT_BLOCK, (cc + 1) * ATT_BLOCK)
            oc = r0 // ATT_BLOCK + cc
            qt_ref[0, oc] = (zt[_QT0:_QT0 + w, sl] * Q_SCALE).astype(BF16)
            vt_ref[0, oc] = zt[_VT0:_VT0 + w, sl].astype(BF16)
            qit_ref[0, oc] = zt[_QIT0:_QIT0 + LANES, sl].astype(BF16)
            wit_ref[0, oc] = zt[_WIT0:_WIT0 + SUBLANES, sl]

        u = _gelu_tanh(_dot(xn, w1_ref[:, _UV0:_UV0 + w]))
        vv = _rmsnorm(_gelu_tanh(_dot(xn, w1_ref[:, _UV0 + w:_UV0 + 2 * w])), gmvg_ref[...])
        for c in range(tm // GM_CHUNK):
            rs = slice(c * GM_CHUNK, (c + 1) * GM_CHUNK)
            vc = vv[rs, :]
            stacked = jnp.concatenate(
                [jnp.where(lane_group == g, vc, 0.0) for g in range(GM_GROUPS)], axis=0).astype(BF16)
            mixed = _dot(wsm, stacked) + gmb_ref[...]
            bg_ref[r0 + c * GM_CHUNK:r0 + (c + 1) * GM_CHUNK, :] = (u[rs, :] * mixed).astype(BF16)

        p = _dot(xn, w1_ref[:, _PZ0:_PZ0 + w])
        base = POOL_HALO + r0
        pbuf[base:base + tm, :] = p
        pos1 = (j * tm_step + r0 + lax.broadcasted_iota(jnp.int32, (tm, LANES), 0) + 1).astype(F32)
        small = lax.broadcasted_iota(jnp.int32, (tm, LANES), 1) < w // len(POOL_WINDOWS)
        w2, w4, w8, w16 = (float(v) for v in POOL_WINDOWS)
        lo_half = slice(0, LANES)
        s2 = p[:, lo_half] + pbuf[base - 1:base - 1 + tm, lo_half]
        s4 = s2 + pbuf[base - 2:base - 2 + tm, lo_half] + pbuf[base - 3:base - 3 + tm, lo_half]
        d_lo = (jnp.where(small, s2, s4) / jnp.minimum(pos1, jnp.where(small, w2, w4))
                - p[:, lo_half])
        hi_half = slice(LANES, 2 * LANES)
        ext = None
        for shift in range(SUBLANES):
            term = pbuf[base - SUBLANES - shift:base - shift + tm, hi_half]
            ext = term if ext is None else ext + term
        s8 = ext[SUBLANES:, :]
        s16 = s8 + ext[0:tm, :]
        d_hi = (jnp.where(small, s8, s16) / jnp.minimum(pos1, jnp.where(small, w8, w16))
                - p[:, hi_half])
        dpool = jnp.concatenate([d_lo, d_hi], axis=1).astype(BF16)
        bp_ref[rows, :] = (_dot(dpool, poolw_ref[...]) * pscale_ref[...]).astype(BF16)

        hbuf[CONV_HALO + r0:CONV_HALO + r0 + tm, :] = (
            _dot(xn, w1_ref[:, _CZ0:_CZ0 + w]) * _sigmoid(_dot(xn, w1_ref[:, _CZ0 + w:_CZ0 + 2 * w])))
        lead = CONV_HALO - (CONV_K - 1)
        for r in range(r0, r0 + tm, CONV_ROWS):
            acc = None
            for shift in range(SUBLANES):
                part = None
                for t in range(CONV_K):
                    if (lead + t) % SUBLANES == shift:
                        start = r + lead + t - shift
                        term = hbuf[start:start + CONV_ROWS + SUBLANES, :] * dw_ref[t:t + 1, :]
                        part = term if part is None else part + term
                part = part[shift:shift + CONV_ROWS, :]
                acc = part if acc is None else acc + part
            hc = acc + cb_ref[...]
            mu = jnp.mean(hc, axis=-1, keepdims=True)
            xc = hc - mu
            yn = xc * lax.rsqrt(jnp.mean(xc * xc, axis=-1, keepdims=True) + EPS)
            yn = yn * lng_ref[...] + lnb_ref[...]
            bc_ref[r:r + CONV_ROWS, :] = (yn * _sigmoid(yn)).astype(BF16)

    pbuf[0:POOL_HALO, :] = pbuf[tm_step:tm_step + POOL_HALO, :]
    hbuf[0:CONV_HALO, :] = hbuf[tm_step:tm_step + CONV_HALO, :]


def _front(x, seq, pre_g, w1, w2t, gmvg, ws_cat, gmb2d, poolw, pscale, dw, cb, lng, lnb, tm):
    n, d = x.shape
    batch = n // seq
    bps = seq // tm
    w = BRANCH_W
    assert POOL_WINDOWS == (2, 4, 8, 16) and w == 2 * LANES and POOL_HALO >= 2 * SUBLANES - 1
    nch = seq // ATT_BLOCK
    cpb = tm // ATT_BLOCK
    full = lambda a: pl.BlockSpec(a.shape, lambda i: (0,) * a.ndim)
    tok = lambda width: pl.BlockSpec((tm, width), lambda i: (i, 0))
    chunked = lambda rows: pl.BlockSpec((1, cpb, rows, ATT_BLOCK),
                                        lambda i: (i // bps, i % bps, 0, 0))
    params = (pre_g, w1, w2t, gmvg, ws_cat, gmb2d, poolw, pscale, dw, cb, lng, lnb)
    out_shape = (
        jax.ShapeDtypeStruct((n, w), BF16),
        jax.ShapeDtypeStruct((n, w), BF16),
        jax.ShapeDtypeStruct((n, w), BF16),
        jax.ShapeDtypeStruct((n, w), BF16),
        jax.ShapeDtypeStruct((n, LANES), BF16),
        jax.ShapeDtypeStruct((batch, nch, w, ATT_BLOCK), BF16),
        jax.ShapeDtypeStruct((batch, nch, w, ATT_BLOCK), BF16),
        jax.ShapeDtypeStruct((batch, nch, LANES, ATT_BLOCK), BF16),
        jax.ShapeDtypeStruct((batch, nch, SUBLANES, ATT_BLOCK), F32),
    )
    out_specs = (tok(w), tok(w), tok(w), tok(w), tok(LANES),
                 chunked(w), chunked(w), chunked(LANES), chunked(SUBLANES))
    return pl.pallas_call(
        functools.partial(_front_body, bps),
        grid=(n // tm,),
        in_specs=[tok(d)] + [full(a) for a in params],
        out_specs=out_specs,
        out_shape=out_shape,
        scratch_shapes=[pltpu.VMEM((tm + CONV_HALO + SUBLANES, w), F32),
                        pltpu.VMEM((tm + POOL_HALO, w), F32)],
        compiler_params=pltpu.CompilerParams(
            dimension_semantics=("arbitrary",), vmem_limit_bytes=VMEM_LIMIT_BYTES),
        name="mixer_front",
    )(x, *params)


def _attn_body(topk,
               tab_ref, qt_ref, qit_ref, wit_ref, k_ref, ki_ref, vt_ref, o_ref,
               s_ref, ssort_ref, s16sort_ref, bd_ref, bp_ref, tri_ref, qm_ref, qim_ref, acc_ref,
               lg0_ref, lg1_ref):
    qb = ATT_BLOCK
    nch = vt_ref.shape[1]
    b = pl.program_id(0)
    i = pl.program_id(1)
    key = lax.broadcasted_iota(jnp.int32, (qb, qb), 0)
    qry = lax.broadcasted_iota(jnp.int32, (qb, qb), 1)

    @pl.when((b == 0) & (i == 0))
    def _():
        tri_ref[...] = jnp.where(qry <= key, 1.0, 0.0).astype(BF16)
        max_exact = REL_BUCKETS // 2
        for ref, off in ((bd_ref, 0), (bp_ref, qb)):
            n = jnp.maximum(qry - key + off, 0)
            large = max_exact + (
                jnp.log(jnp.maximum(n, 1).astype(F32) / max_exact)
                / math.log(REL_MAX_DIST / max_exact) * (REL_BUCKETS - max_exact)).astype(jnp.int32)
            bucket = jnp.where(n < max_exact, n, jnp.minimum(large, REL_BUCKETS - 1))
            for h in range(ATT_HEADS):
                bias = jnp.zeros((qb, qb), F32)
                for k in range(REL_BUCKETS):
                    bias = jnp.where(bucket == k, tab_ref[k * ATT_HEADS + h], bias)
                ref[h] = (bias - tab_ref[(REL_BUCKETS - 1) * ATT_HEADS + h]) * LOG2E

    qt = qt_ref[0, 0]
    qit = qit_ref[0, 0]
    q_head = lax.broadcasted_iota(jnp.int32, qt.shape, 0) // ATT_HD
    qi_head = lax.broadcasted_iota(jnp.int32, qit.shape, 0) // IDX_HD
    for h in range(ATT_HEADS):
        qm_ref[h] = jnp.where(q_head == h, qt, jnp.zeros_like(qt))
    for h in range(IDX_HEADS):
        qim_ref[h] = jnp.where(qi_head == h, qit, jnp.zeros_like(qit))
    wv = wit_ref[0, 0] * ((IDX_HEADS ** -0.5) * (IDX_HD ** -0.5))

    t_pos = i * qb + qry
    n_pairs = (i + 2) // 2

    def keys_of(c):
        return pl.ds(pl.multiple_of(c * qb, qb), qb)

    def score_chunk(c, masked):
        kic = ki_ref[0, keys_of(jnp.minimum(c, nch - 1)), :]
        sc = jnp.zeros((qb, qb), F32)
        for h in range(IDX_HEADS):
            sc = sc + wv[h:h + 1, :] * jnp.maximum(_dot(kic, qim_ref[h]), 0.0)
        if masked:
            valid = (c * qb + key) <= t_pos
            bot = _fold8(jnp.where(valid, sc, jnp.inf), jnp.minimum)
            sc = jnp.where(valid, sc, -jnp.inf)
        s_ref[c] = sc
        slabs = [sc[r:r + SUBLANES] for r in range(0, qb, SUBLANES)]
        groups = [_sort_desc(slabs[g:g + SORT_GROUP]) for g in range(0, len(slabs), SORT_GROUP)]
        ssort_ref[c] = jnp.concatenate([slab for grp in groups for slab in grp], axis=0)
        s16sort_ref[c] = jnp.concatenate(
            [slab for g in range(0, len(groups), 2) for pair in zip(groups[g], groups[g + 1])
             for slab in pair], axis=0).astype(BF16)
        top = functools.reduce(jnp.maximum, [grp[0] for grp in groups])
        if not masked:
            bot = functools.reduce(jnp.minimum, [grp[-1] for grp in groups])
        return top, bot

    def score_chunks(first, count, carry, masked):
        top, bot = carry
        for j in range(count):
            top_j, bot_j = score_chunk(first + j, masked)
            top, bot = jnp.maximum(top, top_j), jnp.minimum(bot, bot_j)
        return top, bot

    quads = (n_pairs - 1) // 2
    carry = lax.fori_loop(
        0, quads, lambda g, cr: score_chunks(4 * g, 4, cr, False),
        (jnp.full((SUBLANES, qb), -jnp.inf, F32), jnp.full((SUBLANES, qb), jnp.inf, F32)))
    carry = lax.fori_loop(
        2 * quads, n_pairs - 1, lambda p, cr: score_chunks(2 * p, 2, cr, False), carry)
    last = 2 * (n_pairs - 1)
    carry = score_chunks(last, 1, carry, True)

    def after_diagonal(cr):
        ssort_ref[last + 1] = jnp.full((qb, qb), -jnp.inf, F32)
        s16sort_ref[last + 1] = jnp.full((qb, qb), -jnp.inf, BF16)
        return cr

    top, bot = lax.cond(i % 2 == 1, lambda cr: score_chunks(last + 1, 1, cr, True),
                        after_diagonal, carry)
    rmin = jnp.min(bot, axis=0, keepdims=True)
    rmax = jnp.max(top, axis=0, keepdims=True)
    cap16 = rmax.astype(BF16).astype(F32)

    n_valid = (i * qb + lax.broadcasted_iota(jnp.int32, (1, qb), 1) + 1).astype(F32)
    kp = jnp.minimum(float(topk), n_valid)

    def count_sorted(x, thr, rows, dtype):
        counts = [_count_ge_sorted([x[r + j * rows:r + (j + 1) * rows] for j in range(SORT_GROUP)],
                                   thr, dtype)
                  for r in range(0, qb, SORT_GROUP * rows)]
        return functools.reduce(jnp.add, counts)

    def count_ge(thr):
        def body(p, acc):
            return acc + (count_sorted(ssort_ref[2 * p], thr, SUBLANES, F32)
                          + count_sorted(ssort_ref[2 * p + 1], thr, SUBLANES, F32))
        acc = lax.fori_loop(0, n_pairs, body, jnp.zeros((SUBLANES, qb), F32))
        return jnp.sum(acc, axis=0, keepdims=True)

    def count_ge16(thr16):
        def body(p, acc):
            both = (count_sorted(s16sort_ref[2 * p], thr16, BF16_ROWS, BF16)
                    + count_sorted(s16sort_ref[2 * p + 1], thr16, BF16_ROWS, BF16))
            return acc + both.astype(F32)
        acc = lax.fori_loop(0, n_pairs, body, jnp.zeros((BF16_ROWS, qb), F32))
        return jnp.sum(acc, axis=0, keepdims=True)

    def coarse(_, carry):
        lo, hi = carry
        mid16 = (0.5 * lo + 0.5 * jnp.minimum(hi, cap16)).astype(BF16)
        feas = count_ge16(mid16) >= kp
        mid = mid16.astype(F32)
        return jnp.where(feas, mid, lo), jnp.where(feas, hi, mid)

    lo16, hi = lax.fori_loop(0, COARSE_STEPS, coarse,
                             (rmin.astype(BF16).astype(F32), jnp.full((1, qb), jnp.inf, F32)))
    lo = lo16 - (jnp.abs(lo16) * BF16_INTERVAL + TINY)
    chi = count_ge(hi)

    def bisect(_, carry):
        lo, hi, chi = carry
        mid = 0.5 * lo + 0.5 * jnp.minimum(hi, rmax)
        cnt = count_ge(mid)
        feas = cnt >= kp
        return (jnp.where(feas, mid, lo), jnp.where(feas, hi, mid), jnp.where(feas, chi, cnt))

    lo, hi, chi = lax.fori_loop(0, BISECT_STEPS, bisect, (lo, hi, chi))

    def max_below(thr):
        def body(p, acc):
            for c in (2 * p, 2 * p + 1):
                x = ssort_ref[c]
                for r in range(0, qb, SORT_GROUP * SUBLANES):
                    group = [x[r + j * SUBLANES:r + (j + 1) * SUBLANES] for j in range(SORT_GROUP)]
                    acc = jnp.maximum(acc, _max_below_sorted(group, thr))
            return acc
        acc = lax.fori_loop(0, n_pairs, body, jnp.full((SUBLANES, qb), -jnp.inf, F32))
        return jnp.max(acc, axis=0, keepdims=True)

    def finish_cond(state):
        return state[4] > 0.0

    def finish_body(state):
        hi, chi, tau, done, _ = state
        m = max_below(hi)
        cnt = count_ge(m)
        feas = cnt >= kp
        active = done < 0.5
        tau = jnp.where(active & feas, m, tau)
        hi = jnp.where(active & (~feas), m, hi)
        chi = jnp.where(active & (~feas), cnt, chi)
        done = jnp.where(active & feas, 1.0, done)
        return hi, chi, tau, done, jnp.max(1.0 - done)

    hi, chi, tau, _, _ = lax.while_loop(
        finish_cond, finish_body,
        (hi, chi, lo, jnp.zeros((1, qb), F32), jnp.float32(1.0)))
    need = kp - chi

    acc_ref[...] = jnp.zeros_like(acc_ref)
    lg1_ref[...] = jnp.full(lg1_ref.shape, NEG, F32)
    heads = range(ATT_HEADS)

    def stage_a(c, bias, live, buf, need_left):
        x = s_ref[c]
        eq = jnp.where(x == tau, 1.0, 0.0)
        if live is not None:
            eq = jnp.where(live, eq, 0.0)
        prefix = _dot(tri_ref[...], eq.astype(BF16))
        maskadd = jnp.where(x >= jnp.where(prefix <= need_left, tau, hi), 0.0, NEG)
        if live is not None:
            maskadd = jnp.where(live, maskadd, NEG)
        kc = k_ref[0, keys_of(c), :]
        mxs = []
        for h in heads:
            lg = _dot(kc, qm_ref[h]) + maskadd
            if bias is not None:
                lg = lg + bias[h]
            buf[h] = lg
            mxs.append(jnp.max(_fold8(lg, jnp.maximum), axis=0, keepdims=True))
        return tuple(mxs), need_left - prefix[qb - 1:qb, :]

    ones_rows = jnp.ones((2 * SUBLANES, qb), BF16)

    def stage_b(c, buf, mxs, ms, ls):
        vtc = vt_ref[0, c]
        new_ms = [jnp.maximum(ms[h], mxs[h]) for h in heads]
        alphas = [jnp.exp2(ms[h] - new_ms[h]) for h in heads]
        pvs = [_dot(jnp.concatenate([vtc[h * ATT_HD:(h + 1) * ATT_HD, :], ones_rows], axis=0),
                    jnp.exp2(buf[h] - new_ms[h]).astype(BF16)) for h in heads]
        new_ls = [alphas[h] * ls[h] + pvs[h][ATT_HD:ATT_HD + 1, :] for h in heads]
        pv = jnp.concatenate([pvs[h][0:ATT_HD, :] for h in heads], axis=0)
        alpha_rows = jnp.concatenate(
            [jnp.broadcast_to(alphas[h], (ATT_HD, qb)) for h in heads], axis=0)
        acc_ref[...] = acc_ref[...] * alpha_rows + pv
        return tuple(new_ms), tuple(new_ls)

    lg_refs = (lg0_ref, lg1_ref)

    def run_stages(stages, carry):
        ms, ls, need_left, pend_mx, pend_c = carry
        for k, (c, bias, live) in enumerate(stages):
            mx, need_left = stage_a(c, bias, live, lg_refs[k % 2], need_left)
            ms, ls = stage_b(pend_c, lg_refs[1 - k % 2], pend_mx, ms, ls)
            pend_mx, pend_c = mx, c
        return ms, ls, need_left, pend_mx, pend_c

    n_far = jnp.maximum(i - 1, 0)
    m_init = tuple(jnp.full((1, qb), M_INIT, F32) for _ in heads)
    carry = (m_init, tuple(jnp.zeros((1, qb), F32) for _ in heads), need, m_init, jnp.int32(0))
    carry = lax.fori_loop(
        0, n_far // ATTEND_UNROLL,
        lambda g, cr: run_stages([(ATTEND_UNROLL * g + j, None, None)
                                  for j in range(ATTEND_UNROLL)], cr),
        carry)
    far_done = (n_far // ATTEND_UNROLL) * ATTEND_UNROLL

    def tail(n_left):
        def run(cr):
            stages = [(far_done + j, None, None) for j in range(n_left)]
            stages += [(jnp.maximum(i - 1, 0), bp_ref, i >= 1), (i, bd_ref, None)]
            ms, ls, _, pend_mx, pend_c = run_stages(stages, cr)
            return stage_b(pend_c, lg_refs[1 - len(stages) % 2], pend_mx, ms, ls)[1]
        return run

    ls = lax.switch(n_far - far_done, [tail(r) for r in range(ATTEND_UNROLL)], carry)

    for h in range(ATT_HEADS):
        rows = slice(h * ATT_HD, (h + 1) * ATT_HD)
        acc_ref[rows, :] = acc_ref[rows, :] / ls[h]
    o_ref[0] = acc_ref[...].T.astype(BF16)


def _attention(tab, qt, qit, wit, k, ki, vt):
    batch, nch, w, qb = qt.shape
    seq = nch * qb
    topk = min(TOPK_MAX, seq // 4)
    per_block = lambda rows: pl.BlockSpec((1, 1, rows, qb), lambda b, i: (b, i, 0, 0))
    return pl.pallas_call(
        functools.partial(_attn_body, topk),
        grid=(batch, nch),
        in_specs=[
            pl.BlockSpec(memory_space=pltpu.SMEM),
            per_block(w), per_block(LANES), per_block(SUBLANES),
            pl.BlockSpec((1, seq, w), lambda b, i: (b, 0, 0)),
            pl.BlockSpec((1, seq, LANES), lambda b, i: (b, 0, 0)),
            pl.BlockSpec((1, nch, w, qb), lambda b, i: (b, 0, 0, 0)),
        ],
        out_specs=pl.BlockSpec((1, qb, w), lambda b, i: (b, i, 0)),
        out_shape=jax.ShapeDtypeStruct((batch, seq, w), BF16),
        scratch_shapes=[
            pltpu.VMEM((nch + 1, qb, qb), F32),
            pltpu.VMEM((nch + 1, qb, qb), F32),
            pltpu.VMEM((nch + 1, qb, qb), BF16),
            pltpu.VMEM((ATT_HEADS, qb, qb), F32),
            pltpu.VMEM((ATT_HEADS, qb, qb), F32),
            pltpu.VMEM((qb, qb), BF16),
            pltpu.VMEM((ATT_HEADS, w, qb), BF16),
            pltpu.VMEM((IDX_HEADS, LANES, qb), BF16),
            pltpu.VMEM((w, qb), F32),
            pltpu.VMEM((ATT_HEADS, qb, qb), F32),
            pltpu.VMEM((ATT_HEADS, qb, qb), F32),
        ],
        compiler_params=pltpu.CompilerParams(
            dimension_semantics=("arbitrary", "arbitrary"), vmem_limit_bytes=VMEM_LIMIT_BYTES),
        name="sparse_attention",
    )(tab, qt, qit, wit, k, ki, vt)


def _back_body(x_ref, bg_ref, bp_ref, ba_ref, bc_ref, pre_ref, post_ref, wgate_ref, wbr_ref,
               wout_ref, o_ref):
    for r in range(0, x_ref.shape[0], FFN_ROWS):
        rows = slice(r, r + FFN_ROWS)
        x = x_ref[rows, :]
        xn = _rmsnorm(x, pre_ref[...]).astype(BF16)
        h = None
        for c in range(0, x_ref.shape[1], BACK_COLS):
            cols = slice(c, c + BACK_COLS)
            y = None
            for n, br in enumerate((bg_ref, bp_ref, ba_ref, bc_ref)):
                term = (_sigmoid(_dot(xn, wgate_ref[n, :, cols]))
                        * _dot(br[rows, :], wbr_ref[n, :, cols]))
                y = term if y is None else y + term
            part = _dot(y.astype(BF16), wout_ref[cols, :])
            h = part if h is None else h + part
        o_ref[rows, :] = x + _rmsnorm(h, post_ref[...])


def _back(x, bg, bp, ba, bc, pre_g, post_g, wgate, wbr, wout):
    n, d = x.shape
    w = BRANCH_W
    tm = min(FFN_BLOCK, n)
    assert tm % FFN_ROWS == 0
    full = lambda a: pl.BlockSpec(a.shape, lambda i: (0,) * a.ndim)
    resident = lambda a: pl.BlockSpec(a.shape, lambda i: (0,) * a.ndim,
                                      pipeline_mode=pl.Buffered(1))
    tok = lambda width: pl.BlockSpec((tm, width), lambda i: (i, 0))
    params = (pre_g, post_g, wgate, wbr, wout)
    return pl.pallas_call(
        _back_body,
        grid=(n // tm,),
        in_specs=([tok(d), tok(w), tok(w), tok(w), tok(w), full(pre_g), full(post_g)]
                  + [resident(a) for a in (wgate, wbr, wout)]),
        out_specs=tok(d),
        out_shape=jax.ShapeDtypeStruct((n, d), F32),
        compiler_params=pltpu.CompilerParams(
            dimension_semantics=("arbitrary",), vmem_limit_bytes=VMEM_LIMIT_BYTES),
        name="mixer_back",
    )(x, bg, bp, ba, bc, *params)


def _cast_body(w_ref, o_ref):
    o_ref[...] = w_ref[...].astype(BF16)


def _cast_bf16(w):
    layers, rows, cols = w.shape
    rb = CAST_ROWS
    assert rows % rb == 0
    spec = pl.BlockSpec((1, rb, cols), lambda l, r: (l, r, 0))
    return pl.pallas_call(
        _cast_body,
        grid=(layers, rows // rb),
        in_specs=[spec],
        out_specs=spec,
        out_shape=jax.ShapeDtypeStruct(w.shape, BF16),
        compiler_params=pltpu.CompilerParams(
            dimension_semantics=("arbitrary", "arbitrary"), vmem_limit_bytes=VMEM_LIMIT_BYTES),
        name="cast_weights",
    )(w)


def _pack_front_weights(w_in):
    d = w_in.shape[0]
    w = BRANCH_W
    o = 0
    uv = w_in[:, o:o + 2 * w]; o += 2 * w
    pz = w_in[:, o:o + w]; o += w
    qz = w_in[:, o:o + w]; o += w
    kz = w_in[:, o:o + w]; o += w
    vz = w_in[:, o:o + w]; o += w
    qi = w_in[:, o:o + IDX_HEADS * IDX_HD]; o += IDX_HEADS * IDX_HD
    ki = w_in[:, o:o + IDX_HD]; o += IDX_HD
    wi = w_in[:, o:o + IDX_HEADS]; o += IDX_HEADS
    cz = w_in[:, o:o + 2 * w]; o += 2 * w
    gz = w_in[:, o:]
    w1 = jnp.concatenate([uv, pz, kz, cz] + [ki] * (LANES // IDX_HD), axis=1).astype(BF16)
    wi_pad = jnp.pad(wi, ((0, 0), (0, _W2_ROWS - _WIT0 - IDX_HEADS)))
    w2t = jnp.concatenate([qz, vz, qi, wi_pad], axis=1).T.astype(BF16)
    wgate = gz.reshape(d, N_BRANCH, d).transpose(1, 0, 2).astype(BF16)
    return w1, w2t, wgate


def _block_diag(pw):
    g, c, _ = pw.shape
    out = jnp.zeros((g * c, g * c), pw.dtype)
    for k in range(g):
        out = out.at[k * c:(k + 1) * c, k * c:(k + 1) * c].set(pw[k])
    return out


def kernel(x, ffn1_pre_g, ffn1_post_g, ffn1_w_gu, ffn1_w_down, mix_pre_g, mix_post_g, w_in,
           gm_v_g, gm_ws, gm_b, pool_w, pool_scale, conv_dw, conv_b, conv_ln_g, conv_ln_b,
           w_branch, w_out, ffn2_pre_g, ffn2_post_g, ffn2_w_gu, ffn2_w_down, rel_bias):
    batch, seq, d = x.shape
    depth = w_in.shape[0]
    n = batch * seq
    tm = min(FRONT_BLOCK, seq)
    w = BRANCH_W
    row = lambda a: a.reshape(1, -1)
    tab = rel_bias.reshape(-1)

    ffn1_w = (_cast_bf16(ffn1_w_gu), _cast_bf16(ffn1_w_down))
    ffn2_w = (_cast_bf16(ffn2_w_gu), _cast_bf16(ffn2_w_down))

    xf = x.reshape(n, d)
    for l in range(depth):
        xf = _ffn(xf, row(ffn1_pre_g[l]), row(ffn1_post_g[l]), *ffn1_w, l)

        w1, w2t, wgate = _pack_front_weights(w_in[l])
        ws_cat = gm_ws[l].transpose(1, 0, 2).reshape(GM_CHUNK, GM_GROUPS * GM_CHUNK)
        gmb2d = jnp.repeat(gm_b[l].T, w // GM_GROUPS, axis=1)
        dw = jnp.pad(conv_dw[l], ((0, 1), (0, 0)))
        bg, bp, bc, k, ki, qt, vt, qit, wit = _front(
            xf, seq, row(mix_pre_g[l]), w1, w2t, row(gm_v_g[l]), ws_cat, gmb2d,
            _block_diag(pool_w[l]).astype(BF16), row(pool_scale[l]), dw, row(conv_b[l]),
            row(conv_ln_g[l]), row(conv_ln_b[l]), tm)
        ba = _attention(tab, qt, qit, wit, k.reshape(batch, seq, w),
                        ki.reshape(batch, seq, LANES), vt)
        xf = _back(xf, bg, bp, ba.reshape(n, w), bc, row(mix_pre_g[l]), row(mix_post_g[l]),
                   wgate, w_branch[l].astype(BF16), w_out[l].astype(BF16))

        xf = _ffn(xf, row(ffn2_pre_g[l]), row(ffn2_post_g[l]), *ffn2_w, l)
    return xf.reshape(batch, seq, d)
```

```python
import functools
import math

import jax
import jax.numpy as jnp
from jax import lax
from jax.experimental import pallas as pl
from jax.experimental.pallas import tpu as pltpu

F32 = jnp.float32
BF16 = jnp.bfloat16

EPS = 1e-6
BRANCH_W = 256
N_BRANCH = 4
GM_GROUPS = 4
GM_CHUNK = 128
POOL_WINDOWS = (2, 4, 8, 16)
ATT_HEADS = 4
ATT_HD = 64
IDX_HEADS = 4
IDX_HD = 32
TOPK_MAX = 256
REL_BUCKETS = 32
REL_MAX_DIST = 128
CONV_K = 31

VMEM_LIMIT_BYTES = 56 * 1024 * 1024
LANES = 128
SUBLANES = 8

FFN_CHUNK = 256
FFN_BLOCK = 1024
FFN_ROWS = 512
CAST_ROWS = 128
BACK_COLS = 256
FRONT_BLOCK = 1024
FRONT_ROWS = 512
CONV_HALO = 32
POOL_HALO = 16
CONV_ROWS = 128
ATT_BLOCK = 256
ATTEND_UNROLL = 4
COARSE_STEPS = 10
BISECT_STEPS = 8
BF16_ROWS = 16
SORT_GROUP = 8
BF16_INTERVAL = 2.0 ** -6
TINY = 1e-30
LOG2E = math.log2(math.e)
Q_SCALE = ATT_HD ** -0.5 * LOG2E
NEG = -1e30
M_INIT = -1e29


def _rmsnorm(x, g):
    return x * lax.rsqrt(jnp.mean(x * x, axis=-1, keepdims=True) + EPS) * g


def _sigmoid(x):
    return 1.0 / (1.0 + jnp.exp(-x))


def _gelu_tanh(x):
    return x * (0.5 * (1.0 + jnp.tanh(math.sqrt(2.0 / math.pi) * (x + 0.044715 * (x ** 3)))))


def _dot(a, b):
    return jnp.dot(a, b, preferred_element_type=F32)


def _fold8(x, op):
    return _fold(x, op, SUBLANES)


def _fold(x, op, rows):
    parts = [x[j * rows:(j + 1) * rows] for j in range(x.shape[0] // rows)]
    while len(parts) > 1:
        nxt = [op(parts[j], parts[j + 1]) for j in range(0, len(parts) - 1, 2)]
        if len(parts) % 2:
            nxt.append(parts[-1])
        parts = nxt
    return parts[0]


def _sorting_network(n):
    net = []

    def merge(lo, n, r):
        step = r * 2
        if step < n:
            merge(lo, n, step)
            merge(lo + r, n, step)
            net.extend((i, i + r) for i in range(lo + r, lo + n - r, step))
        else:
            net.append((lo, lo + r))

    def sort(lo, n):
        if n > 1:
            sort(lo, n // 2)
            sort(lo + n // 2, n // 2)
            merge(lo, n, 1)

    sort(0, n)
    return tuple(net)


def _sort_desc(vals):
    vals = list(vals)
    for a, b in _sorting_network(len(vals)):
        vals[a], vals[b] = jnp.maximum(vals[a], vals[b]), jnp.minimum(vals[a], vals[b])
    return vals


def _pick(masks, cands):
    for m in reversed(masks):
        cands = [jnp.where(m, cands[2 * i + 1], cands[2 * i]) for i in range(len(cands) // 2)]
    return cands[0]


def _search_sorted(s, thr):
    g = len(s)
    ge, pivot = [], None
    for k in range(g.bit_length() - 1):
        pivot = _pick(ge, [s[v * (g >> k) + (g >> (k + 1)) - 1] for v in range(1 << k)])
        ge.append(pivot >= thr)
    return ge, pivot


def _count_ge_sorted(s, thr, dtype):
    g = len(s)
    c = lambda v: jnp.full((), v, dtype)
    ge, _ = _search_sorted(s, thr)
    terms = [jnp.where(m, c(g >> (l + 1)), c(0)) for l, m in enumerate(ge)]
    terms.append(jnp.where(s[g - 1] >= thr, c(1), c(0)))
    while len(terms) > 1:
        terms = [terms[i] + terms[i + 1] for i in range(0, len(terms) - 1, 2)] + (
            [terms[-1]] if len(terms) % 2 else [])
    return terms[0]


def _max_below_sorted(s, thr):
    g = len(s)
    ge, pivot = _search_sorted(s, thr)
    after = [s[2 * v + 1] for v in range(g // 2)]
    after[-1] = jnp.where(s[g - 1] < thr, s[g - 1], -jnp.inf)
    return jnp.where(ge[-1], _pick(ge[:-1], after), pivot)


def _ffn_body(x_ref, pre_ref, post_ref, wgu_ref, wd_ref, o_ref, xn_ref, acc_ref):
    f = wd_ref.shape[0]
    tm = x_ref.shape[0]
    halves = [slice(r, r + FFN_ROWS) for r in range(0, tm, FFN_ROWS)]
    for rows in halves:
        xn_ref[rows, :] = _rmsnorm(x_ref[rows, :], pre_ref[...]).astype(BF16)
    for c in range(0, f, FFN_CHUNK):
        for rows in halves:
            xn = xn_ref[rows, :]
            a = _dot(xn, wgu_ref[:, c:c + FFN_CHUNK])
            b = _dot(xn, wgu_ref[:, f + c:f + c + FFN_CHUNK])
            hm = (a * _sigmoid(a) * b).astype(BF16)
            down = _dot(hm, wd_ref[c:c + FFN_CHUNK, :])
            acc_ref[rows, :] = down if c == 0 else acc_ref[rows, :] + down
    for rows in halves:
        o_ref[rows, :] = x_ref[rows, :] + 0.5 * _rmsnorm(acc_ref[rows, :], post_ref[...])


def _ffn(x, pre_g, post_g, wgu, wd, layer):
    n, d = x.shape
    f = wd.shape[1]
    tm = min(FFN_BLOCK, n)
    assert f % FFN_CHUNK == 0 and wgu.shape[1:] == (d, 2 * f) and tm % FFN_ROWS == 0
    full = lambda shape: pl.BlockSpec(shape, lambda i: (0,) * len(shape))
    resident = lambda rows, cols: pl.BlockSpec((None, rows, cols), lambda i: (layer, 0, 0),
                                               pipeline_mode=pl.Buffered(1))
    return pl.pallas_call(
        _ffn_body,
        grid=(n // tm,),
        in_specs=[
            pl.BlockSpec((tm, d), lambda i: (i, 0)),
            full((1, d)), full((1, d)), resident(d, 2 * f), resident(f, d),
        ],
        out_specs=pl.BlockSpec((tm, d), lambda i: (i, 0)),
        out_shape=jax.ShapeDtypeStruct((n, d), F32),
        scratch_shapes=[pltpu.VMEM((tm, d), BF16), pltpu.VMEM((tm, d), F32)],
        compiler_params=pltpu.CompilerParams(
            dimension_semantics=("arbitrary",), vmem_limit_bytes=VMEM_LIMIT_BYTES),
        name="ffn",
    )(x, pre_g, post_g, wgu, wd)


_UV0, _PZ0, _K0, _CZ0, _KI0, _W1_COLS = 0, 512, 768, 1024, 1536, 1664
_QT0, _VT0, _QIT0, _WIT0, _W2_ROWS = 0, 256, 512, 640, 656


def _front_body(blocks_per_seq,
                x_ref, pre_ref, w1_ref, w2t_ref, gmvg_ref, ws_ref, gmb_ref, poolw_ref,
                pscale_ref, dw_ref, cb_ref, lng_ref, lnb_ref,
                bg_ref, bp_ref, bc_ref, k_ref, ki_ref, qt_ref, vt_ref, qit_ref, wit_ref,
                hbuf, pbuf):
    tm_step = x_ref.shape[0]
    tm = min(FRONT_ROWS, tm_step)
    w = BRANCH_W
    j = pl.program_id(0) % blocks_per_seq

    @pl.when(j == 0)
    def _():
        hbuf[...] = jnp.zeros_like(hbuf)
        pbuf[0:POOL_HALO, :] = jnp.zeros((POOL_HALO, w), F32)

    row = lax.broadcasted_iota(jnp.int32, (GM_CHUNK, GM_GROUPS * GM_CHUNK), 0)
    col = lax.broadcasted_iota(jnp.int32, (GM_CHUNK, GM_GROUPS * GM_CHUNK), 1)
    wsm = jnp.where((col % GM_CHUNK) <= row, ws_ref[...], 0.0).astype(BF16)
    lane_group = lax.broadcasted_iota(jnp.int32, (GM_CHUNK, w), 1) // (w // GM_GROUPS)

    for r0 in range(0, tm_step, tm):
        rows = slice(r0, r0 + tm)
        xn = _rmsnorm(x_ref[rows, :], pre_ref[...]).astype(BF16)

        k_ref[rows, :] = _dot(xn, w1_ref[:, _K0:_K0 + w]).astype(BF16)
        ki_ref[rows, :] = _dot(xn, w1_ref[:, _KI0:_KI0 + LANES]).astype(BF16)
        zt = lax.dot_general(w2t_ref[...], xn, (((1,), (1,)), ((), ())),
                             preferred_element_type=F32)
        for cc in range(tm // ATT_BLOCK):
            sl = slice(cc * ATT_BLOCK, (cc + 1) * ATT_BLOCK)
            oc = r0 // ATT_BLOCK + cc
            qt_ref[0, oc] = (zt[_QT0:_QT0 + w, sl] * Q_SCALE).astype(BF16)
            vt_ref[0, oc] = zt[_VT0:_VT0 + w, sl].astype(BF16)
            qit_ref[0, oc] = zt[_QIT0:_QIT0 + LANES, sl].astype(BF16)
            wit_ref[0, oc] = zt[_WIT0:_WIT0 + SUBLANES, sl]

        u = _gelu_tanh(_dot(xn, w1_ref[:, _UV0:_UV0 + w]))
        vv = _rmsnorm(_gelu_tanh(_dot(xn, w1_ref[:, _UV0 + w:_UV0 + 2 * w])), gmvg_ref[...])
        for c in range(tm // GM_CHUNK):
            rs = slice(c * GM_CHUNK, (c + 1) * GM_CHUNK)
            vc = vv[rs, :]
            stacked = jnp.concatenate(
                [jnp.where(lane_group == g, vc, 0.0) for g in range(GM_GROUPS)], axis=0).astype(BF16)
            mixed = _dot(wsm, stacked) + gmb_ref[...]
            bg_ref[r0 + c * GM_CHUNK:r0 + (c + 1) * GM_CHUNK, :] = (u[rs, :] * mixed).astype(BF16)

        p = _dot(xn, w1_ref[:, _PZ0:_PZ0 + w])
        base = POOL_HALO + r0
        pbuf[base:base + tm, :] = p
        pos1 = (j * tm_step + r0 + lax.broadcasted_iota(jnp.int32, (tm, LANES), 0) + 1).astype(F32)
        small = lax.broadcasted_iota(jnp.int32, (tm, LANES), 1) < w // len(POOL_WINDOWS)
        w2, w4, w8, w16 = (float(v) for v in POOL_WINDOWS)
        lo_half = slice(0, LANES)
        s2 = p[:, lo_half] + pbuf[base - 1:base - 1 + tm, lo_half]
        s4 = s2 + pbuf[base - 2:base - 2 + tm, lo_half] + pbuf[base - 3:base - 3 + tm, lo_half]
        d_lo = (jnp.where(small, s2, s4) / jnp.minimum(pos1, jnp.where(small, w2, w4))
                - p[:, lo_half])
        hi_half = slice(LANES, 2 * LANES)
        ext = None
        for shift in range(SUBLANES):
            term = pbuf[base - SUBLANES - shift:base - shift + tm, hi_half]
            ext = term if ext is None else ext + term
        s8 = ext[SUBLANES:, :]
        s16 = s8 + ext[0:tm, :]
        d_hi = (jnp.where(small, s8, s16) / jnp.minimum(pos1, jnp.where(small, w8, w16))
                - p[:, hi_half])
        dpool = jnp.concatenate([d_lo, d_hi], axis=1).astype(BF16)
        bp_ref[rows, :] = (_dot(dpool, poolw_ref[...]) * pscale_ref[...]).astype(BF16)

        hbuf[CONV_HALO + r0:CONV_HALO + r0 + tm, :] = (
            _dot(xn, w1_ref[:, _CZ0:_CZ0 + w]) * _sigmoid(_dot(xn, w1_ref[:, _CZ0 + w:_CZ0 + 2 * w])))
        lead = CONV_HALO - (CONV_K - 1)
        for r in range(r0, r0 + tm, CONV_ROWS):
            acc = None
            for shift in range(SUBLANES):
                part = None
                for t in range(CONV_K):
                    if (lead + t) % SUBLANES == shift:
                        start = r + lead + t - shift
                        term = hbuf[start:start + CONV_ROWS + SUBLANES, :] * dw_ref[t:t + 1, :]
                        part = term if part is None else part + term
                part = part[shift:shift + CONV_ROWS, :]
                acc = part if acc is None else acc + part
            hc = acc + cb_ref[...]
            mu = jnp.mean(hc, axis=-1, keepdims=True)
            xc = hc - mu
            yn = xc * lax.rsqrt(jnp.mean(xc * xc, axis=-1, keepdims=True) + EPS)
            yn = yn * lng_ref[...] + lnb_ref[...]
            bc_ref[r:r + CONV_ROWS, :] = (yn * _sigmoid(yn)).astype(BF16)

    pbuf[0:POOL_HALO, :] = pbuf[tm_step:tm_step + POOL_HALO, :]
    hbuf[0:CONV_HALO, :] = hbuf[tm_step:tm_step + CONV_HALO, :]


def _front(x, seq, pre_g, w1, w2t, gmvg, ws_cat, gmb2d, poolw, pscale, dw, cb, lng, lnb, tm):
    n, d = x.shape
    batch = n // seq
    bps = seq // tm
    w = BRANCH_W
    assert POOL_WINDOWS == (2, 4, 8, 16) and w == 2 * LANES and POOL_HALO >= 2 * SUBLANES - 1
    nch = seq // ATT_BLOCK
    cpb = tm // ATT_BLOCK
    full = lambda a: pl.BlockSpec(a.shape, lambda i: (0,) * a.ndim)
    tok = lambda width: pl.BlockSpec((tm, width), lambda i: (i, 0))
    chunked = lambda rows: pl.BlockSpec((1, cpb, rows, ATT_BLOCK),
                                        lambda i: (i // bps, i % bps, 0, 0))
    params = (pre_g, w1, w2t, gmvg, ws_cat, gmb2d, poolw, pscale, dw, cb, lng, lnb)
    out_shape = (
        jax.ShapeDtypeStruct((n, w), BF16),
        jax.ShapeDtypeStruct((n, w), BF16),
        jax.ShapeDtypeStruct((n, w), BF16),
        jax.ShapeDtypeStruct((n, w), BF16),
        jax.ShapeDtypeStruct((n, LANES), BF16),
        jax.ShapeDtypeStruct((batch, nch, w, ATT_BLOCK), BF16),
        jax.ShapeDtypeStruct((batch, nch, w, ATT_BLOCK), BF16),
        jax.ShapeDtypeStruct((batch, nch, LANES, ATT_BLOCK), BF16),
        jax.ShapeDtypeStruct((batch, nch, SUBLANES, ATT_BLOCK), F32),
    )
    out_specs = (tok(w), tok(w), tok(w), tok(w), tok(LANES),
                 chunked(w), chunked(w), chunked(LANES), chunked(SUBLANES))
    return pl.pallas_call(
        functools.partial(_front_body, bps),
        grid=(n // tm,),
        in_specs=[tok(d)] + [full(a) for a in params],
        out_specs=out_specs,
        out_shape=out_shape,
        scratch_shapes=[pltpu.VMEM((tm + CONV_HALO + SUBLANES, w), F32),
                        pltpu.VMEM((tm + POOL_HALO, w), F32)],
        compiler_params=pltpu.CompilerParams(
            dimension_semantics=("arbitrary",), vmem_limit_bytes=VMEM_LIMIT_BYTES),
        name="mixer_front",
    )(x, *params)


def _attn_body(topk,
               tab_ref, qt_ref, qit_ref, wit_ref, k_ref, ki_ref, vt_ref, o_ref,
               s_ref, ssort_ref, s16sort_ref, bd_ref, bp_ref, tri_ref, qm_ref, qim_ref, acc_ref,
               lg0_ref, lg1_ref):
    qb = ATT_BLOCK
    nch = vt_ref.shape[1]
    b = pl.program_id(0)
    i = pl.program_id(1)
    key = lax.broadcasted_iota(jnp.int32, (qb, qb), 0)
    qry = lax.broadcasted_iota(jnp.int32, (qb, qb), 1)

    @pl.when((b == 0) & (i == 0))
    def _():
        tri_ref[...] = jnp.where(qry <= key, 1.0, 0.0).astype(BF16)
        max_exact = REL_BUCKETS // 2
        for ref, off in ((bd_ref, 0), (bp_ref, qb)):
            n = jnp.maximum(qry - key + off, 0)
            large = max_exact + (
                jnp.log(jnp.maximum(n, 1).astype(F32) / max_exact)
                / math.log(REL_MAX_DIST / max_exact) * (REL_BUCKETS - max_exact)).astype(jnp.int32)
            bucket = jnp.where(n < max_exact, n, jnp.minimum(large, REL_BUCKETS - 1))
            for h in range(ATT_HEADS):
                bias = jnp.zeros((qb, qb), F32)
                for k in range(REL_BUCKETS):
                    bias = jnp.where(bucket == k, tab_ref[k * ATT_HEADS + h], bias)
                ref[h] = (bias - tab_ref[(REL_BUCKETS - 1) * ATT_HEADS + h]) * LOG2E

    qt = qt_ref[0, 0]
    qit = qit_ref[0, 0]
    q_head = lax.broadcasted_iota(jnp.int32, qt.shape, 0) // ATT_HD
    qi_head = lax.broadcasted_iota(jnp.int32, qit.shape, 0) // IDX_HD
    for h in range(ATT_HEADS):
        qm_ref[h] = jnp.where(q_head == h, qt, jnp.zeros_like(qt))
    for h in range(IDX_HEADS):
        qim_ref[h] = jnp.where(qi_head == h, qit, jnp.zeros_like(qit))
    wv = wit_ref[0, 0] * ((IDX_HEADS ** -0.5) * (IDX_HD ** -0.5))

    t_pos = i * qb + qry
    n_pairs = (i + 2) // 2

    def keys_of(c):
        return pl.ds(pl.multiple_of(c * qb, qb), qb)

    def score_chunk(c, masked):
        kic = ki_ref[0, keys_of(jnp.minimum(c, nch - 1)), :]
        sc = jnp.zeros((qb, qb), F32)
        for h in range(IDX_HEADS):
            sc = sc + wv[h:h + 1, :] * jnp.maximum(_dot(kic, qim_ref[h]), 0.0)
        if masked:
            valid = (c * qb + key) <= t_pos
            bot = _fold8(jnp.where(valid, sc, jnp.inf), jnp.minimum)
            sc = jnp.where(valid, sc, -jnp.inf)
        s_ref[c] = sc
        slabs = [sc[r:r + SUBLANES] for r in range(0, qb, SUBLANES)]
        groups = [_sort_desc(slabs[g:g + SORT_GROUP]) for g in range(0, len(slabs), SORT_GROUP)]
        ssort_ref[c] = jnp.concatenate([slab for grp in groups for slab in grp], axis=0)
        s16sort_ref[c] = jnp.concatenate(
            [slab for g in range(0, len(groups), 2) for pair in zip(groups[g], groups[g + 1])
             for slab in pair], axis=0).astype(BF16)
        top = functools.reduce(jnp.maximum, [grp[0] for grp in groups])
        if not masked:
            bot = functools.reduce(jnp.minimum, [grp[-1] for grp in groups])
        return top, bot

    def score_chunks(first, count, carry, masked):
        top, bot = carry
        for j in range(count):
            top_j, bot_j = score_chunk(first + j, masked)
            top, bot = jnp.maximum(top, top_j), jnp.minimum(bot, bot_j)
        return top, bot

    quads = (n_pairs - 1) // 2
    carry = lax.fori_loop(
        0, quads, lambda g, cr: score_chunks(4 * g, 4, cr, False),
        (jnp.full((SUBLANES, qb), -jnp.inf, F32), jnp.full((SUBLANES, qb), jnp.inf, F32)))
    carry = lax.fori_loop(
        2 * quads, n_pairs - 1, lambda p, cr: score_chunks(2 * p, 2, cr, False), carry)
    last = 2 * (n_pairs - 1)
    carry = score_chunks(last, 1, carry, True)

    def after_diagonal(cr):
        ssort_ref[last + 1] = jnp.full((qb, qb), -jnp.inf, F32)
        s16sort_ref[last + 1] = jnp.full((qb, qb), -jnp.inf, BF16)
        return cr

    top, bot = lax.cond(i % 2 == 1, lambda cr: score_chunks(last + 1, 1, cr, True),
                        after_diagonal, carry)
    rmin = jnp.min(bot, axis=0, keepdims=True)
    rmax = jnp.max(top, axis=0, keepdims=True)
    cap16 = rmax.astype(BF16).astype(F32)

    n_valid = (i * qb + lax.broadcasted_iota(jnp.int32, (1, qb), 1) + 1).astype(F32)
    kp = jnp.minimum(float(topk), n_valid)

    def count_sorted(x, thr, rows, dtype):
        counts = [_count_ge_sorted([x[r + j * rows:r + (j + 1) * rows] for j in range(SORT_GROUP)],
                                   thr, dtype)
                  for r in range(0, qb, SORT_GROUP * rows)]
        return functools.reduce(jnp.add, counts)

    def count_ge(thr):
        def body(p, acc):
            return acc + (count_sorted(ssort_ref[2 * p], thr, SUBLANES, F32)
                          + count_sorted(ssort_ref[2 * p + 1], thr, SUBLANES, F32))
        acc = lax.fori_loop(0, n_pairs, body, jnp.zeros((SUBLANES, qb), F32))
        return jnp.sum(acc, axis=0, keepdims=True)

    def count_ge16(thr16):
        def body(p, acc):
            both = (count_sorted(s16sort_ref[2 * p], thr16, BF16_ROWS, BF16)
                    + count_sorted(s16sort_ref[2 * p + 1], thr16, BF16_ROWS, BF16))
            return acc + both.astype(F32)
        acc = lax.fori_loop(0, n_pairs, body, jnp.zeros((BF16_ROWS, qb), F32))
        return jnp.sum(acc, axis=0, keepdims=True)

    def coarse(_, carry):
        lo, hi = carry
        mid16 = (0.5 * lo + 0.5 * jnp.minimum(hi, cap16)).astype(BF16)
        feas = count_ge16(mid16) >= kp
        mid = mid16.astype(F32)
        return jnp.where(feas, mid, lo), jnp.where(feas, hi, mid)

    lo16, hi = lax.fori_loop(0, COARSE_STEPS, coarse,
                             (rmin.astype(BF16).astype(F32), jnp.full((1, qb), jnp.inf, F32)))
    lo = lo16 - (jnp.abs(lo16) * BF16_INTERVAL + TINY)
    chi = count_ge(hi)

    def bisect(_, carry):
        lo, hi, chi = carry
        mid = 0.5 * lo + 0.5 * jnp.minimum(hi, rmax)
        cnt = count_ge(mid)
        feas = cnt >= kp
        return (jnp.where(feas, mid, lo), jnp.where(feas, hi, mid), jnp.where(feas, chi, cnt))

    lo, hi, chi = lax.fori_loop(0, BISECT_STEPS, bisect, (lo, hi, chi))

    def max_below(thr):
        def body(p, acc):
            for c in (2 * p, 2 * p + 1):
                x = ssort_ref[c]
                for r in range(0, qb, SORT_GROUP * SUBLANES):
                    group = [x[r + j * SUBLANES:r + (j + 1) * SUBLANES] for j in range(SORT_GROUP)]
                    acc = jnp.maximum(acc, _max_below_sorted(group, thr))
            return acc
        acc = lax.fori_loop(0, n_pairs, body, jnp.full((SUBLANES, qb), -jnp.inf, F32))
        return jnp.max(acc, axis=0, keepdims=True)

    def finish_cond(state):
        return state[4] > 0.0

    def finish_body(state):
        hi, chi, tau, done, _ = state
        m = max_below(hi)
        cnt = count_ge(m)
        feas = cnt >= kp
        active = done < 0.5
        tau = jnp.where(active & feas, m, tau)
        hi = jnp.where(active & (~feas), m, hi)
        chi = jnp.where(active & (~feas), cnt, chi)
        done = jnp.where(active & feas, 1.0, done)
        return hi, chi, tau, done, jnp.max(1.0 - done)

    hi, chi, tau, _, _ = lax.while_loop(
        finish_cond, finish_body,
        (hi, chi, lo, jnp.zeros((1, qb), F32), jnp.float32(1.0)))
    need = kp - chi

    acc_ref[...] = jnp.zeros_like(acc_ref)
    heads = range(ATT_HEADS)

    def stage_a(c, bias, live, buf, need_left):
        x = s_ref[c]
        eq = jnp.where(x == tau, 1.0, 0.0)
        if live is not None:
            eq = jnp.where(live, eq, 0.0)
        prefix = _dot(tri_ref[...], eq.astype(BF16))
        maskadd = jnp.where(x >= jnp.where(prefix <= need_left, tau, hi), 0.0, NEG)
        if live is not None:
            maskadd = jnp.where(live, maskadd, NEG)
        kc = k_ref[0, keys_of(c), :]
        mxs = []
        for h in heads:
            lg = _dot(kc, qm_ref[h]) + maskadd
            if bias is not None:
                lg = lg + bias[h]
            buf[h] = lg
            mxs.append(jnp.max(_fold8(lg, jnp.maximum), axis=0, keepdims=True))
        return tuple(mxs), need_left - prefix[qb - 1:qb, :]

    ones_rows = jnp.ones((2 * SUBLANES, qb), BF16)

    def stage_b(c, buf, mxs, ms, ls):
        vtc = vt_ref[0, c]
        new_ms = [jnp.maximum(ms[h], mxs[h]) for h in heads]
        alphas = [jnp.exp2(ms[h] - new_ms[h]) for h in heads]
        pvs = [_dot(jnp.concatenate([vtc[h * ATT_HD:(h + 1) * ATT_HD, :], ones_rows], axis=0),
                    jnp.exp2(buf[h] - new_ms[h]).astype(BF16)) for h in heads]
        new_ls = [alphas[h] * ls[h] + pvs[h][ATT_HD:ATT_HD + 1, :] for h in heads]
        pv = jnp.concatenate([pvs[h][0:ATT_HD, :] for h in heads], axis=0)
        alpha_rows = jnp.concatenate(
            [jnp.broadcast_to(alphas[h], (ATT_HD, qb)) for h in heads], axis=0)
        acc_ref[...] = acc_ref[...] * alpha_rows + pv
        return tuple(new_ms), tuple(new_ls)

    lg_refs = (lg0_ref, lg1_ref)

    def run_stages(stages, carry, pending=True):
        ms, ls, need_left, pend_mx, pend_c = carry
        for k, (c, bias, live) in enumerate(stages):
            mx, need_left = stage_a(c, bias, live, lg_refs[k % 2], need_left)
            if k > 0 or pending:
                ms, ls = stage_b(pend_c, lg_refs[1 - k % 2], pend_mx, ms, ls)
            pend_mx, pend_c = mx, c
        return ms, ls, need_left, pend_mx, pend_c

    def far_group(g):
        first = jnp.asarray(ATTEND_UNROLL * g, jnp.int32)
        return [(first + j, None, None) for j in range(ATTEND_UNROLL)]

    n_far = jnp.maximum(i - 1, 0)
    n_groups = n_far // ATTEND_UNROLL
    m_init = tuple(jnp.full((1, qb), M_INIT, F32) for _ in heads)
    carry = (m_init, tuple(jnp.zeros((1, qb), F32) for _ in heads), need, m_init, jnp.int32(0))

    def empty_pipeline(cr):
        lg1_ref[...] = jnp.full(lg1_ref.shape, NEG, F32)
        return cr

    carry = lax.cond(n_groups >= 1, lambda cr: run_stages(far_group(0), cr, pending=False),
                     empty_pipeline, carry)
    carry = lax.fori_loop(1, n_groups, lambda g, cr: run_stages(far_group(g), cr), carry)
    far_done = n_groups * ATTEND_UNROLL

    def tail(n_left):
        def run(cr):
            stages = [(far_done + j, None, None) for j in range(n_left)]
            stages += [(jnp.maximum(i - 1, 0), bp_ref, i >= 1), (i, bd_ref, None)]
            ms, ls, _, pend_mx, pend_c = run_stages(stages, cr)
            return stage_b(pend_c, lg_refs[1 - len(stages) % 2], pend_mx, ms, ls)[1]
        return run

    ls = lax.switch(n_far - far_done, [tail(r) for r in range(ATTEND_UNROLL)], carry)

    for h in range(ATT_HEADS):
        rows = slice(h * ATT_HD, (h + 1) * ATT_HD)
        acc_ref[rows, :] = acc_ref[rows, :] / ls[h]
    o_ref[0] = acc_ref[...].T.astype(BF16)


def _attention(tab, qt, qit, wit, k, ki, vt):
    batch, nch, w, qb = qt.shape
    seq = nch * qb
    topk = min(TOPK_MAX, seq // 4)
    per_block = lambda rows: pl.BlockSpec((1, 1, rows, qb), lambda b, i: (b, i, 0, 0))
    return pl.pallas_call(
        functools.partial(_attn_body, topk),
        grid=(batch, nch),
        in_specs=[
            pl.BlockSpec(memory_space=pltpu.SMEM),
            per_block(w), per_block(LANES), per_block(SUBLANES),
            pl.BlockSpec((1, seq, w), lambda b, i: (b, 0, 0)),
            pl.BlockSpec((1, seq, LANES), lambda b, i: (b, 0, 0)),
            pl.BlockSpec((1, nch, w, qb), lambda b, i: (b, 0, 0, 0)),
        ],
        out_specs=pl.BlockSpec((1, qb, w), lambda b, i: (b, i, 0)),
        out_shape=jax.ShapeDtypeStruct((batch, seq, w), BF16),
        scratch_shapes=[
            pltpu.VMEM((nch + 1, qb, qb), F32),
            pltpu.VMEM((nch + 1, qb, qb), F32),
            pltpu.VMEM((nch + 1, qb, qb), BF16),
            pltpu.VMEM((ATT_HEADS, qb, qb), F32),
            pltpu.VMEM((ATT_HEADS, qb, qb), F32),
            pltpu.VMEM((qb, qb), BF16),
            pltpu.VMEM((ATT_HEADS, w, qb), BF16),
            pltpu.VMEM((IDX_HEADS, LANES, qb), BF16),
            pltpu.VMEM((w, qb), F32),
            pltpu.VMEM((ATT_HEADS, qb, qb), F32),
            pltpu.VMEM((ATT_HEADS, qb, qb), F32),
        ],
        compiler_params=pltpu.CompilerParams(
            dimension_semantics=("arbitrary", "arbitrary"), vmem_limit_bytes=VMEM_LIMIT_BYTES),
        name="sparse_attention",
    )(tab, qt, qit, wit, k, ki, vt)


def _back_body(x_ref, bg_ref, bp_ref, ba_ref, bc_ref, pre_ref, post_ref, wgate_ref, wbr_ref,
               wout_ref, o_ref):
    for r in range(0, x_ref.shape[0], FFN_ROWS):
        rows = slice(r, r + FFN_ROWS)
        x = x_ref[rows, :]
        xn = _rmsnorm(x, pre_ref[...]).astype(BF16)
        h = None
        for c in range(0, x_ref.shape[1], BACK_COLS):
            cols = slice(c, c + BACK_COLS)
            y = None
            for n, br in enumerate((bg_ref, bp_ref, ba_ref, bc_ref)):
                term = (_sigmoid(_dot(xn, wgate_ref[n, :, cols]))
                        * _dot(br[rows, :], wbr_ref[n, :, cols]))
                y = term if y is None else y + term
            part = _dot(y.astype(BF16), wout_ref[cols, :])
            h = part if h is None else h + part
        o_ref[rows, :] = x + _rmsnorm(h, post_ref[...])


def _back(x, bg, bp, ba, bc, pre_g, post_g, wgate, wbr, wout):
    n, d = x.shape
    w = BRANCH_W
    tm = min(FFN_BLOCK, n)
    assert tm % FFN_ROWS == 0
    full = lambda a: pl.BlockSpec(a.shape, lambda i: (0,) * a.ndim)
    resident = lambda a: pl.BlockSpec(a.shape, lambda i: (0,) * a.ndim,
                                      pipeline_mode=pl.Buffered(1))
    tok = lambda width: pl.BlockSpec((tm, width), lambda i: (i, 0))
    params = (pre_g, post_g, wgate, wbr, wout)
    return pl.pallas_call(
        _back_body,
        grid=(n // tm,),
        in_specs=([tok(d), tok(w), tok(w), tok(w), tok(w), full(pre_g), full(post_g)]
                  + [resident(a) for a in (wgate, wbr, wout)]),
        out_specs=tok(d),
        out_shape=jax.ShapeDtypeStruct((n, d), F32),
        compiler_params=pltpu.CompilerParams(
            dimension_semantics=("arbitrary",), vmem_limit_bytes=VMEM_LIMIT_BYTES),
        name="mixer_back",
    )(x, bg, bp, ba, bc, *params)


def _cast_body(w_ref, o_ref):
    o_ref[...] = w_ref[...].astype(BF16)


def _cast_bf16(w):
    layers, rows, cols = w.shape
    rb = CAST_ROWS
    assert rows % rb == 0
    spec = pl.BlockSpec((1, rb, cols), lambda l, r: (l, r, 0))
    return pl.pallas_call(
        _cast_body,
        grid=(layers, rows // rb),
        in_specs=[spec],
        out_specs=spec,
        out_shape=jax.ShapeDtypeStruct(w.shape, BF16),
        compiler_params=pltpu.CompilerParams(
            dimension_semantics=("arbitrary", "arbitrary"), vmem_limit_bytes=VMEM_LIMIT_BYTES),
        name="cast_weights",
    )(w)


def _pack_front_weights(w_in):
    d = w_in.shape[0]
    w = BRANCH_W
    o = 0
    uv = w_in[:, o:o + 2 * w]; o += 2 * w
    pz = w_in[:, o:o + w]; o += w
    qz = w_in[:, o:o + w]; o += w
    kz = w_in[:, o:o + w]; o += w
    vz = w_in[:, o:o + w]; o += w
    qi = w_in[:, o:o + IDX_HEADS * IDX_HD]; o += IDX_HEADS * IDX_HD
    ki = w_in[:, o:o + IDX_HD]; o += IDX_HD
    wi = w_in[:, o:o + IDX_HEADS]; o += IDX_HEADS
    cz = w_in[:, o:o + 2 * w]; o += 2 * w
    gz = w_in[:, o:]
    w1 = jnp.concatenate([uv, pz, kz, cz] + [ki] * (LANES // IDX_HD), axis=1).astype(BF16)
    wi_pad = jnp.pad(wi, ((0, 0), (0, _W2_ROWS - _WIT0 - IDX_HEADS)))
    w2t = jnp.concatenate([qz, vz, qi, wi_pad], axis=1).T.astype(BF16)
    wgate = gz.reshape(d, N_BRANCH, d).transpose(1, 0, 2).astype(BF16)
    return w1, w2t, wgate


def _block_diag(pw):
    g, c, _ = pw.shape
    out = jnp.zeros((g * c, g * c), pw.dtype)
    for k in range(g):
        out = out.at[k * c:(k + 1) * c, k * c:(k + 1) * c].set(pw[k])
    return out


def kernel(x, ffn1_pre_g, ffn1_post_g, ffn1_w_gu, ffn1_w_down, mix_pre_g, mix_post_g, w_in,
           gm_v_g, gm_ws, gm_b, pool_w, pool_scale, conv_dw, conv_b, conv_ln_g, conv_ln_b,
           w_branch, w_out, ffn2_pre_g, ffn2_post_g, ffn2_w_gu, ffn2_w_down, rel_bias):
    batch, seq, d = x.shape
    depth = w_in.shape[0]
    n = batch * seq
    tm = min(FRONT_BLOCK, seq)
    w = BRANCH_W
    row = lambda a: a.reshape(1, -1)
    tab = rel_bias.reshape(-1)

    ffn1_w = (_cast_bf16(ffn1_w_gu), _cast_bf16(ffn1_w_down))
    ffn2_w = (_cast_bf16(ffn2_w_gu), _cast_bf16(ffn2_w_down))

    xf = x.reshape(n, d)
    for l in range(depth):
        xf = _ffn(xf, row(ffn1_pre_g[l]), row(ffn1_post_g[l]), *ffn1_w, l)

        w1, w2t, wgate = _pack_front_weights(w_in[l])
        ws_cat = gm_ws[l].transpose(1, 0, 2).reshape(GM_CHUNK, GM_GROUPS * GM_CHUNK)
        gmb2d = jnp.repeat(gm_b[l].T, w // GM_GROUPS, axis=1)
        dw = jnp.pad(conv_dw[l], ((0, 1), (0, 0)))
        bg, bp, bc, k, ki, qt, vt, qit, wit = _front(
            xf, seq, row(mix_pre_g[l]), w1, w2t, row(gm_v_g[l]), ws_cat, gmb2d,
            _block_diag(pool_w[l]).astype(BF16), row(pool_scale[l]), dw, row(conv_b[l]),
            row(conv_ln_g[l]), row(conv_ln_b[l]), tm)
        ba = _attention(tab, qt, qit, wit, k.reshape(batch, seq, w),
                        ki.reshape(batch, seq, LANES), vt)
        xf = _back(xf, bg, bp, ba.reshape(n, w), bc, row(mix_pre_g[l]), row(mix_post_g[l]),
                   wgate, w_branch[l].astype(BF16), w_out[l].astype(BF16))

        xf = _ffn(xf, row(ffn2_pre_g[l]), row(ffn2_post_g[l]), *ffn2_w, l)
    return xf.reshape(batch, seq, d)
```

```python
import functools
import math

import jax
import jax.numpy as jnp
from jax import lax
from jax.experimental import pallas as pl
from jax.experimental.pallas import tpu as pltpu

F32 = jnp.float32
BF16 = jnp.bfloat16

EPS = 1e-6
BRANCH_W = 256
N_BRANCH = 4
GM_GROUPS = 4
GM_CHUNK = 128
POOL_WINDOWS = (2, 4, 8, 16)
ATT_HEADS = 4
ATT_HD = 64
IDX_HEADS = 4
IDX_HD = 32
TOPK_MAX = 256
REL_BUCKETS = 32
REL_MAX_DIST = 128
CONV_K = 31

VMEM_LIMIT_BYTES = 56 * 1024 * 1024
LANES = 128
SUBLANES = 8

FFN_CHUNK = 256
FFN_BLOCK = 1024
FFN_ROWS = 512
CAST_ROWS = 128
BACK_COLS = 256
FRONT_BLOCK = 1024
FRONT_ROWS = 512
CONV_HALO = 32
POOL_HALO = 16
CONV_ROWS = 128
ATT_BLOCK = 256
ATTEND_UNROLL = 6
COARSE_STEPS = 10
BISECT_STEPS = 8
BF16_ROWS = 16
SORT_GROUP = 8
BF16_INTERVAL = 2.0 ** -6
TINY = 1e-30
LOG2E = math.log2(math.e)
Q_SCALE = ATT_HD ** -0.5 * LOG2E
NEG = -1e30
M_INIT = -1e29


def _rmsnorm(x, g):
    return x * lax.rsqrt(jnp.mean(x * x, axis=-1, keepdims=True) + EPS) * g


def _sigmoid(x):
    return 1.0 / (1.0 + jnp.exp(-x))


def _gelu_tanh(x):
    return x * (0.5 * (1.0 + jnp.tanh(math.sqrt(2.0 / math.pi) * (x + 0.044715 * (x ** 3)))))


def _dot(a, b):
    return jnp.dot(a, b, preferred_element_type=F32)


def _fold8(x, op):
    return _fold(x, op, SUBLANES)


def _fold(x, op, rows):
    parts = [x[j * rows:(j + 1) * rows] for j in range(x.shape[0] // rows)]
    while len(parts) > 1:
        nxt = [op(parts[j], parts[j + 1]) for j in range(0, len(parts) - 1, 2)]
        if len(parts) % 2:
            nxt.append(parts[-1])
        parts = nxt
    return parts[0]


def _sorting_network(n):
    net = []

    def merge(lo, n, r):
        step = r * 2
        if step < n:
            merge(lo, n, step)
            merge(lo + r, n, step)
            net.extend((i, i + r) for i in range(lo + r, lo + n - r, step))
        else:
            net.append((lo, lo + r))

    def sort(lo, n):
        if n > 1:
            sort(lo, n // 2)
            sort(lo + n // 2, n // 2)
            merge(lo, n, 1)

    sort(0, n)
    return tuple(net)


def _sort_desc(vals):
    vals = list(vals)
    for a, b in _sorting_network(len(vals)):
        vals[a], vals[b] = jnp.maximum(vals[a], vals[b]), jnp.minimum(vals[a], vals[b])
    return vals


def _pick(masks, cands):
    for m in reversed(masks):
        cands = [jnp.where(m, cands[2 * i + 1], cands[2 * i]) for i in range(len(cands) // 2)]
    return cands[0]


def _search_sorted(s, thr):
    g = len(s)
    ge, pivot = [], None
    for k in range(g.bit_length() - 1):
        pivot = _pick(ge, [s[v * (g >> k) + (g >> (k + 1)) - 1] for v in range(1 << k)])
        ge.append(pivot >= thr)
    return ge, pivot


def _count_ge_sorted(s, thr, dtype):
    g = len(s)
    c = lambda v: jnp.full((), v, dtype)
    ge, _ = _search_sorted(s, thr)
    terms = [jnp.where(m, c(g >> (l + 1)), c(0)) for l, m in enumerate(ge)]
    terms.append(jnp.where(s[g - 1] >= thr, c(1), c(0)))
    while len(terms) > 1:
        terms = [terms[i] + terms[i + 1] for i in range(0, len(terms) - 1, 2)] + (
            [terms[-1]] if len(terms) % 2 else [])
    return terms[0]


def _max_below_sorted(s, thr):
    g = len(s)
    ge, pivot = _search_sorted(s, thr)
    after = [s[2 * v + 1] for v in range(g // 2)]
    after[-1] = jnp.where(s[g - 1] < thr, s[g - 1], -jnp.inf)
    return jnp.where(ge[-1], _pick(ge[:-1], after), pivot)


def _ffn_body(x_ref, pre_ref, post_ref, wgu_ref, wd_ref, o_ref, xn_ref, acc_ref):
    f = wd_ref.shape[0]
    tm = x_ref.shape[0]
    halves = [slice(r, r + FFN_ROWS) for r in range(0, tm, FFN_ROWS)]
    for rows in halves:
        xn_ref[rows, :] = _rmsnorm(x_ref[rows, :], pre_ref[...]).astype(BF16)
    for c in range(0, f, FFN_CHUNK):
        for rows in halves:
            xn = xn_ref[rows, :]
            a = _dot(xn, wgu_ref[:, c:c + FFN_CHUNK])
            b = _dot(xn, wgu_ref[:, f + c:f + c + FFN_CHUNK])
            hm = (a * _sigmoid(a) * b).astype(BF16)
            down = _dot(hm, wd_ref[c:c + FFN_CHUNK, :])
            acc_ref[rows, :] = down if c == 0 else acc_ref[rows, :] + down
    for rows in halves:
        o_ref[rows, :] = x_ref[rows, :] + 0.5 * _rmsnorm(acc_ref[rows, :], post_ref[...])


def _ffn(x, pre_g, post_g, wgu, wd, layer):
    n, d = x.shape
    f = wd.shape[1]
    tm = min(FFN_BLOCK, n)
    assert f % FFN_CHUNK == 0 and wgu.shape[1:] == (d, 2 * f) and tm % FFN_ROWS == 0
    full = lambda shape: pl.BlockSpec(shape, lambda i: (0,) * len(shape))
    resident = lambda rows, cols: pl.BlockSpec((None, rows, cols), lambda i: (layer, 0, 0),
                                               pipeline_mode=pl.Buffered(1))
    return pl.pallas_call(
        _ffn_body,
        grid=(n // tm,),
        in_specs=[
            pl.BlockSpec((tm, d), lambda i: (i, 0)),
            full((1, d)), full((1, d)), resident(d, 2 * f), resident(f, d),
        ],
        out_specs=pl.BlockSpec((tm, d), lambda i: (i, 0)),
        out_shape=jax.ShapeDtypeStruct((n, d), F32),
        scratch_shapes=[pltpu.VMEM((tm, d), BF16), pltpu.VMEM((tm, d), F32)],
        compiler_params=pltpu.CompilerParams(
            dimension_semantics=("arbitrary",), vmem_limit_bytes=VMEM_LIMIT_BYTES),
        name="ffn",
    )(x, pre_g, post_g, wgu, wd)


_UV0, _PZ0, _K0, _CZ0, _KI0, _W1_COLS = 0, 512, 768, 1024, 1536, 1664
_QT0, _VT0, _QIT0, _WIT0, _W2_ROWS = 0, 256, 512, 640, 656


def _front_body(blocks_per_seq,
                x_ref, pre_ref, w1_ref, w2t_ref, gmvg_ref, ws_ref, gmb_ref, poolw_ref,
                pscale_ref, dw_ref, cb_ref, lng_ref, lnb_ref,
                bg_ref, bp_ref, bc_ref, k_ref, ki_ref, qt_ref, vt_ref, qit_ref, wit_ref,
                hbuf, pbuf):
    tm_step = x_ref.shape[0]
    tm = min(FRONT_ROWS, tm_step)
    w = BRANCH_W
    j = pl.program_id(0) % blocks_per_seq

    @pl.when(j == 0)
    def _():
        hbuf[...] = jnp.zeros_like(hbuf)
        pbuf[0:POOL_HALO, :] = jnp.zeros((POOL_HALO, w), F32)

    row = lax.broadcasted_iota(jnp.int32, (GM_CHUNK, GM_GROUPS * GM_CHUNK), 0)
    col = lax.broadcasted_iota(jnp.int32, (GM_CHUNK, GM_GROUPS * GM_CHUNK), 1)
    wsm = jnp.where((col % GM_CHUNK) <= row, ws_ref[...], 0.0).astype(BF16)
    lane_group = lax.broadcasted_iota(jnp.int32, (GM_CHUNK, w), 1) // (w // GM_GROUPS)

    for r0 in range(0, tm_step, tm):
        rows = slice(r0, r0 + tm)
        xn = _rmsnorm(x_ref[rows, :], pre_ref[...]).astype(BF16)

        k_ref[rows, :] = _dot(xn, w1_ref[:, _K0:_K0 + w]).astype(BF16)
        ki_ref[rows, :] = _dot(xn, w1_ref[:, _KI0:_KI0 + LANES]).astype(BF16)
        zt = lax.dot_general(w2t_ref[...], xn, (((1,), (1,)), ((), ())),
                             preferred_element_type=F32)
        for cc in range(tm // ATT_BLOCK):
            sl = slice(cc * ATT_BLOCK, (cc + 1) * ATT_BLOCK)
            oc = r0 // ATT_BLOCK + cc
            qt_ref[0, oc] = (zt[_QT0:_QT0 + w, sl] * Q_SCALE).astype(BF16)
            vt_ref[0, oc] = zt[_VT0:_VT0 + w, sl].astype(BF16)
            qit_ref[0, oc] = zt[_QIT0:_QIT0 + LANES, sl].astype(BF16)
            wit_ref[0, oc] = zt[_WIT0:_WIT0 + SUBLANES, sl]

        u = _gelu_tanh(_dot(xn, w1_ref[:, _UV0:_UV0 + w]))
        vv = _rmsnorm(_gelu_tanh(_dot(xn, w1_ref[:, _UV0 + w:_UV0 + 2 * w])), gmvg_ref[...])
        for c in range(tm // GM_CHUNK):
            rs = slice(c * GM_CHUNK, (c + 1) * GM_CHUNK)
            vc = vv[rs, :]
            stacked = jnp.concatenate(
                [jnp.where(lane_group == g, vc, 0.0) for g in range(GM_GROUPS)], axis=0).astype(BF16)
            mixed = _dot(wsm, stacked) + gmb_ref[...]
            bg_ref[r0 + c * GM_CHUNK:r0 + (c + 1) * GM_CHUNK, :] = (u[rs, :] * mixed).astype(BF16)

        p = _dot(xn, w1_ref[:, _PZ0:_PZ0 + w])
        base = POOL_HALO + r0
        pbuf[base:base + tm, :] = p
        pos1 = (j * tm_step + r0 + lax.broadcasted_iota(jnp.int32, (tm, LANES), 0) + 1).astype(F32)
        small = lax.broadcasted_iota(jnp.int32, (tm, LANES), 1) < w // len(POOL_WINDOWS)
        w2, w4, w8, w16 = (float(v) for v in POOL_WINDOWS)
        lo_half = slice(0, LANES)
        s2 = p[:, lo_half] + pbuf[base - 1:base - 1 + tm, lo_half]
        s4 = s2 + pbuf[base - 2:base - 2 + tm, lo_half] + pbuf[base - 3:base - 3 + tm, lo_half]
        d_lo = (jnp.where(small, s2, s4) / jnp.minimum(pos1, jnp.where(small, w2, w4))
                - p[:, lo_half])
        hi_half = slice(LANES, 2 * LANES)
        ext = None
        for shift in range(SUBLANES):
            term = pbuf[base - SUBLANES - shift:base - shift + tm, hi_half]
            ext = term if ext is None else ext + term
        s8 = ext[SUBLANES:, :]
        s16 = s8 + ext[0:tm, :]
        d_hi = (jnp.where(small, s8, s16) / jnp.minimum(pos1, jnp.where(small, w8, w16))
                - p[:, hi_half])
        dpool = jnp.concatenate([d_lo, d_hi], axis=1).astype(BF16)
        bp_ref[rows, :] = (_dot(dpool, poolw_ref[...]) * pscale_ref[...]).astype(BF16)

        hbuf[CONV_HALO + r0:CONV_HALO + r0 + tm, :] = (
            _dot(xn, w1_ref[:, _CZ0:_CZ0 + w]) * _sigmoid(_dot(xn, w1_ref[:, _CZ0 + w:_CZ0 + 2 * w])))
        lead = CONV_HALO - (CONV_K - 1)
        for r in range(r0, r0 + tm, CONV_ROWS):
            acc = None
            for shift in range(SUBLANES):
                part = None
                for t in range(CONV_K):
                    if (lead + t) % SUBLANES == shift:
                        start = r + lead + t - shift
                        term = hbuf[start:start + CONV_ROWS + SUBLANES, :] * dw_ref[t:t + 1, :]
                        part = term if part is None else part + term
                part = part[shift:shift + CONV_ROWS, :]
                acc = part if acc is None else acc + part
            hc = acc + cb_ref[...]
            mu = jnp.mean(hc, axis=-1, keepdims=True)
            xc = hc - mu
            yn = xc * lax.rsqrt(jnp.mean(xc * xc, axis=-1, keepdims=True) + EPS)
            yn = yn * lng_ref[...] + lnb_ref[...]
            bc_ref[r:r + CONV_ROWS, :] = (yn * _sigmoid(yn)).astype(BF16)

    pbuf[0:POOL_HALO, :] = pbuf[tm_step:tm_step + POOL_HALO, :]
    hbuf[0:CONV_HALO, :] = hbuf[tm_step:tm_step + CONV_HALO, :]


def _front(x, seq, pre_g, w1, w2t, gmvg, ws_cat, gmb2d, poolw, pscale, dw, cb, lng, lnb, tm):
    n, d = x.shape
    batch = n // seq
    bps = seq // tm
    w = BRANCH_W
    assert POOL_WINDOWS == (2, 4, 8, 16) and w == 2 * LANES and POOL_HALO >= 2 * SUBLANES - 1
    nch = seq // ATT_BLOCK
    cpb = tm // ATT_BLOCK
    full = lambda a: pl.BlockSpec(a.shape, lambda i: (0,) * a.ndim)
    tok = lambda width: pl.BlockSpec((tm, width), lambda i: (i, 0))
    chunked = lambda rows: pl.BlockSpec((1, cpb, rows, ATT_BLOCK),
                                        lambda i: (i // bps, i % bps, 0, 0))
    params = (pre_g, w1, w2t, gmvg, ws_cat, gmb2d, poolw, pscale, dw, cb, lng, lnb)
    out_shape = (
        jax.ShapeDtypeStruct((n, w), BF16),
        jax.ShapeDtypeStruct((n, w), BF16),
        jax.ShapeDtypeStruct((n, w), BF16),
        jax.ShapeDtypeStruct((n, w), BF16),
        jax.ShapeDtypeStruct((n, LANES), BF16),
        jax.ShapeDtypeStruct((batch, nch, w, ATT_BLOCK), BF16),
        jax.ShapeDtypeStruct((batch, nch, w, ATT_BLOCK), BF16),
        jax.ShapeDtypeStruct((batch, nch, LANES, ATT_BLOCK), BF16),
        jax.ShapeDtypeStruct((batch, nch, SUBLANES, ATT_BLOCK), F32),
    )
    out_specs = (tok(w), tok(w), tok(w), tok(w), tok(LANES),
                 chunked(w), chunked(w), chunked(LANES), chunked(SUBLANES))
    return pl.pallas_call(
        functools.partial(_front_body, bps),
        grid=(n // tm,),
        in_specs=[tok(d)] + [full(a) for a in params],
        out_specs=out_specs,
        out_shape=out_shape,
        scratch_shapes=[pltpu.VMEM((tm + CONV_HALO + SUBLANES, w), F32),
                        pltpu.VMEM((tm + POOL_HALO, w), F32)],
        compiler_params=pltpu.CompilerParams(
            dimension_semantics=("arbitrary",), vmem_limit_bytes=VMEM_LIMIT_BYTES),
        name="mixer_front",
    )(x, *params)


def _attn_body(topk,
               tab_ref, qt_ref, qit_ref, wit_ref, k_ref, ki_ref, vt_ref, o_ref,
               s_ref, ssort_ref, s16sort_ref, bd_ref, bp_ref, tri_ref, qm_ref, qim_ref, acc_ref,
               lg0_ref, lg1_ref):
    qb = ATT_BLOCK
    nch = vt_ref.shape[1]
    b = pl.program_id(0)
    i = pl.program_id(1)
    key = lax.broadcasted_iota(jnp.int32, (qb, qb), 0)
    qry = lax.broadcasted_iota(jnp.int32, (qb, qb), 1)

    @pl.when((b == 0) & (i == 0))
    def _():
        tri_ref[...] = jnp.where(qry <= key, 1.0, 0.0).astype(BF16)
        max_exact = REL_BUCKETS // 2
        for ref, off in ((bd_ref, 0), (bp_ref, qb)):
            n = jnp.maximum(qry - key + off, 0)
            large = max_exact + (
                jnp.log(jnp.maximum(n, 1).astype(F32) / max_exact)
                / math.log(REL_MAX_DIST / max_exact) * (REL_BUCKETS - max_exact)).astype(jnp.int32)
            bucket = jnp.where(n < max_exact, n, jnp.minimum(large, REL_BUCKETS - 1))
            for h in range(ATT_HEADS):
                bias = jnp.zeros((qb, qb), F32)
                for k in range(REL_BUCKETS):
                    bias = jnp.where(bucket == k, tab_ref[k * ATT_HEADS + h], bias)
                ref[h] = (bias - tab_ref[(REL_BUCKETS - 1) * ATT_HEADS + h]) * LOG2E

    qt = qt_ref[0, 0]
    qit = qit_ref[0, 0]
    q_head = lax.broadcasted_iota(jnp.int32, qt.shape, 0) // ATT_HD
    qi_head = lax.broadcasted_iota(jnp.int32, qit.shape, 0) // IDX_HD
    for h in range(ATT_HEADS):
        qm_ref[h] = jnp.where(q_head == h, qt, jnp.zeros_like(qt))
    for h in range(IDX_HEADS):
        qim_ref[h] = jnp.where(qi_head == h, qit, jnp.zeros_like(qit))
    wv = wit_ref[0, 0] * ((IDX_HEADS ** -0.5) * (IDX_HD ** -0.5))

    t_pos = i * qb + qry
    n_pairs = (i + 2) // 2

    def keys_of(c):
        return pl.ds(pl.multiple_of(c * qb, qb), qb)

    def score_chunk(c, masked):
        kic = ki_ref[0, keys_of(jnp.minimum(c, nch - 1)), :]
        sc = jnp.zeros((qb, qb), F32)
        for h in range(IDX_HEADS):
            sc = sc + wv[h:h + 1, :] * jnp.maximum(_dot(kic, qim_ref[h]), 0.0)
        if masked:
            valid = (c * qb + key) <= t_pos
            bot = _fold8(jnp.where(valid, sc, jnp.inf), jnp.minimum)
            sc = jnp.where(valid, sc, -jnp.inf)
        s_ref[c] = sc
        slabs = [sc[r:r + SUBLANES] for r in range(0, qb, SUBLANES)]
        groups = [_sort_desc(slabs[g:g + SORT_GROUP]) for g in range(0, len(slabs), SORT_GROUP)]
        ssort_ref[c] = jnp.concatenate([slab for grp in groups for slab in grp], axis=0)
        s16sort_ref[c] = jnp.concatenate(
            [slab for g in range(0, len(groups), 2) for pair in zip(groups[g], groups[g + 1])
             for slab in pair], axis=0).astype(BF16)
        top = functools.reduce(jnp.maximum, [grp[0] for grp in groups])
        if not masked:
            bot = functools.reduce(jnp.minimum, [grp[-1] for grp in groups])
        return top, bot

    def score_chunks(first, count, carry, masked):
        top, bot = carry
        for j in range(count):
            top_j, bot_j = score_chunk(first + j, masked)
            top, bot = jnp.maximum(top, top_j), jnp.minimum(bot, bot_j)
        return top, bot

    quads = (n_pairs - 1) // 2
    carry = lax.fori_loop(
        0, quads, lambda g, cr: score_chunks(4 * g, 4, cr, False),
        (jnp.full((SUBLANES, qb), -jnp.inf, F32), jnp.full((SUBLANES, qb), jnp.inf, F32)))
    carry = lax.fori_loop(
        2 * quads, n_pairs - 1, lambda p, cr: score_chunks(2 * p, 2, cr, False), carry)
    last = 2 * (n_pairs - 1)
    carry = score_chunks(last, 1, carry, True)

    def after_diagonal(cr):
        ssort_ref[last + 1] = jnp.full((qb, qb), -jnp.inf, F32)
        s16sort_ref[last + 1] = jnp.full((qb, qb), -jnp.inf, BF16)
        return cr

    top, bot = lax.cond(i % 2 == 1, lambda cr: score_chunks(last + 1, 1, cr, True),
                        after_diagonal, carry)
    rmin = jnp.min(bot, axis=0, keepdims=True)
    rmax = jnp.max(top, axis=0, keepdims=True)
    cap16 = rmax.astype(BF16).astype(F32)

    n_valid = (i * qb + lax.broadcasted_iota(jnp.int32, (1, qb), 1) + 1).astype(F32)
    kp = jnp.minimum(float(topk), n_valid)

    def count_sorted(x, thr, rows, dtype):
        counts = [_count_ge_sorted([x[r + j * rows:r + (j + 1) * rows] for j in range(SORT_GROUP)],
                                   thr, dtype)
                  for r in range(0, qb, SORT_GROUP * rows)]
        return functools.reduce(jnp.add, counts)

    def count_ge(thr):
        def body(p, acc):
            return acc + (count_sorted(ssort_ref[2 * p], thr, SUBLANES, F32)
                          + count_sorted(ssort_ref[2 * p + 1], thr, SUBLANES, F32))
        acc = lax.fori_loop(0, n_pairs, body, jnp.zeros((SUBLANES, qb), F32))
        return jnp.sum(acc, axis=0, keepdims=True)

    def count_ge16(thr16):
        def body(p, acc):
            both = (count_sorted(s16sort_ref[2 * p], thr16, BF16_ROWS, BF16)
                    + count_sorted(s16sort_ref[2 * p + 1], thr16, BF16_ROWS, BF16))
            return acc + both.astype(F32)
        acc = lax.fori_loop(0, n_pairs, body, jnp.zeros((BF16_ROWS, qb), F32))
        return jnp.sum(acc, axis=0, keepdims=True)

    def coarse(_, carry):
        lo, hi = carry
        mid16 = (0.5 * lo + 0.5 * jnp.minimum(hi, cap16)).astype(BF16)
        feas = count_ge16(mid16) >= kp
        mid = mid16.astype(F32)
        return jnp.where(feas, mid, lo), jnp.where(feas, hi, mid)

    lo16, hi = lax.fori_loop(0, COARSE_STEPS, coarse,
                             (rmin.astype(BF16).astype(F32), jnp.full((1, qb), jnp.inf, F32)))
    lo = lo16 - (jnp.abs(lo16) * BF16_INTERVAL + TINY)
    chi = count_ge(hi)

    def bisect(_, carry):
        lo, hi, chi = carry
        mid = 0.5 * lo + 0.5 * jnp.minimum(hi, rmax)
        cnt = count_ge(mid)
        feas = cnt >= kp
        return (jnp.where(feas, mid, lo), jnp.where(feas, hi, mid), jnp.where(feas, chi, cnt))

    lo, hi, chi = lax.fori_loop(0, BISECT_STEPS, bisect, (lo, hi, chi))

    def max_below(thr):
        def body(p, acc):
            for c in (2 * p, 2 * p + 1):
                x = ssort_ref[c]
                for r in range(0, qb, SORT_GROUP * SUBLANES):
                    group = [x[r + j * SUBLANES:r + (j + 1) * SUBLANES] for j in range(SORT_GROUP)]
                    acc = jnp.maximum(acc, _max_below_sorted(group, thr))
            return acc
        acc = lax.fori_loop(0, n_pairs, body, jnp.full((SUBLANES, qb), -jnp.inf, F32))
        return jnp.max(acc, axis=0, keepdims=True)

    def finish_cond(state):
        return state[4] > 0.0

    def finish_body(state):
        hi, chi, tau, done, _ = state
        m = max_below(hi)
        cnt = count_ge(m)
        feas = cnt >= kp
        active = done < 0.5
        tau = jnp.where(active & feas, m, tau)
        hi = jnp.where(active & (~feas), m, hi)
        chi = jnp.where(active & (~feas), cnt, chi)
        done = jnp.where(active & feas, 1.0, done)
        return hi, chi, tau, done, jnp.max(1.0 - done)

    hi, chi, tau, _, _ = lax.while_loop(
        finish_cond, finish_body,
        (hi, chi, lo, jnp.zeros((1, qb), F32), jnp.float32(1.0)))
    need = kp - chi

    acc_ref[...] = jnp.zeros_like(acc_ref)
    heads = range(ATT_HEADS)

    def stage_a(c, bias, live, buf, need_left):
        x = s_ref[c]
        eq = jnp.where(x == tau, 1.0, 0.0)
        if live is not None:
            eq = jnp.where(live, eq, 0.0)
        prefix = _dot(tri_ref[...], eq.astype(BF16))
        maskadd = jnp.where(x >= jnp.where(prefix <= need_left, tau, hi), 0.0, NEG)
        if live is not None:
            maskadd = jnp.where(live, maskadd, NEG)
        kc = k_ref[0, keys_of(c), :]
        mxs = []
        for h in heads:
            lg = _dot(kc, qm_ref[h]) + maskadd
            if bias is not None:
                lg = lg + bias[h]
            buf[h] = lg
            mxs.append(jnp.max(_fold8(lg, jnp.maximum), axis=0, keepdims=True))
        return tuple(mxs), need_left - prefix[qb - 1:qb, :]

    ones_rows = jnp.ones((2 * SUBLANES, qb), BF16)

    def stage_b(c, buf, mxs, ms, ls):
        vtc = vt_ref[0, c]
        new_ms = [jnp.maximum(ms[h], mxs[h]) for h in heads]
        alphas = [jnp.exp2(ms[h] - new_ms[h]) for h in heads]
        pvs = [_dot(jnp.concatenate([vtc[h * ATT_HD:(h + 1) * ATT_HD, :], ones_rows], axis=0),
                    jnp.exp2(buf[h] - new_ms[h]).astype(BF16)) for h in heads]
        new_ls = [alphas[h] * ls[h] + pvs[h][ATT_HD:ATT_HD + 1, :] for h in heads]
        pv = jnp.concatenate([pvs[h][0:ATT_HD, :] for h in heads], axis=0)
        alpha_rows = jnp.concatenate(
            [jnp.broadcast_to(alphas[h], (ATT_HD, qb)) for h in heads], axis=0)
        acc_ref[...] = acc_ref[...] * alpha_rows + pv
        return tuple(new_ms), tuple(new_ls)

    lg_refs = (lg0_ref, lg1_ref)

    def run_stages(stages, carry, pending=True):
        ms, ls, need_left, pend_mx, pend_c = carry
        for k, (c, bias, live) in enumerate(stages):
            mx, need_left = stage_a(c, bias, live, lg_refs[k % 2], need_left)
            if k > 0 or pending:
                ms, ls = stage_b(pend_c, lg_refs[1 - k % 2], pend_mx, ms, ls)
            pend_mx, pend_c = mx, c
        return ms, ls, need_left, pend_mx, pend_c

    def far_group(g):
        first = jnp.asarray(ATTEND_UNROLL * g, jnp.int32)
        return [(first + j, None, None) for j in range(ATTEND_UNROLL)]

    n_far = jnp.maximum(i - 1, 0)
    n_groups = n_far // ATTEND_UNROLL
    m_init = tuple(jnp.full((1, qb), M_INIT, F32) for _ in heads)
    carry = (m_init, tuple(jnp.zeros((1, qb), F32) for _ in heads), need, m_init, jnp.int32(0))

    def empty_pipeline(cr):
        lg1_ref[...] = jnp.full(lg1_ref.shape, NEG, F32)
        return cr

    carry = lax.cond(n_groups >= 1, lambda cr: run_stages(far_group(0), cr, pending=False),
                     empty_pipeline, carry)
    carry = lax.fori_loop(1, n_groups, lambda g, cr: run_stages(far_group(g), cr), carry)
    far_done = n_groups * ATTEND_UNROLL

    def tail(n_left):
        def run(cr):
            stages = [(far_done + j, None, None) for j in range(n_left)]
            stages += [(jnp.maximum(i - 1, 0), bp_ref, i >= 1), (i, bd_ref, None)]
            ms, ls, _, pend_mx, pend_c = run_stages(stages, cr)
            return stage_b(pend_c, lg_refs[1 - len(stages) % 2], pend_mx, ms, ls)[1]
        return run

    ls = lax.switch(n_far - far_done, [tail(r) for r in range(ATTEND_UNROLL)], carry)

    for h in range(ATT_HEADS):
        rows = slice(h * ATT_HD, (h + 1) * ATT_HD)
        acc_ref[rows, :] = acc_ref[rows, :] / ls[h]
    o_ref[0] = acc_ref[...].T.astype(BF16)


def _attention(tab, qt, qit, wit, k, ki, vt):
    batch, nch, w, qb = qt.shape
    seq = nch * qb
    topk = min(TOPK_MAX, seq // 4)
    per_block = lambda rows: pl.BlockSpec((1, 1, rows, qb), lambda b, i: (b, i, 0, 0))
    return pl.pallas_call(
        functools.partial(_attn_body, topk),
        grid=(batch, nch),
        in_specs=[
            pl.BlockSpec(memory_space=pltpu.SMEM),
            per_block(w), per_block(LANES), per_block(SUBLANES),
            pl.BlockSpec((1, seq, w), lambda b, i: (b, 0, 0)),
            pl.BlockSpec((1, seq, LANES), lambda b, i: (b, 0, 0)),
            pl.BlockSpec((1, nch, w, qb), lambda b, i: (b, 0, 0, 0)),
        ],
        out_specs=pl.BlockSpec((1, qb, w), lambda b, i: (b, i, 0)),
        out_shape=jax.ShapeDtypeStruct((batch, seq, w), BF16),
        scratch_shapes=[
            pltpu.VMEM((nch + 1, qb, qb), F32),
            pltpu.VMEM((nch + 1, qb, qb), F32),
            pltpu.VMEM((nch + 1, qb, qb), BF16),
            pltpu.VMEM((ATT_HEADS, qb, qb), F32),
            pltpu.VMEM((ATT_HEADS, qb, qb), F32),
            pltpu.VMEM((qb, qb), BF16),
            pltpu.VMEM((ATT_HEADS, w, qb), BF16),
            pltpu.VMEM((IDX_HEADS, LANES, qb), BF16),
            pltpu.VMEM((w, qb), F32),
            pltpu.VMEM((ATT_HEADS, qb, qb), F32),
            pltpu.VMEM((ATT_HEADS, qb, qb), F32),
        ],
        compiler_params=pltpu.CompilerParams(
            dimension_semantics=("arbitrary", "arbitrary"), vmem_limit_bytes=VMEM_LIMIT_BYTES),
        name="sparse_attention",
    )(tab, qt, qit, wit, k, ki, vt)


def _back_body(x_ref, bg_ref, bp_ref, ba_ref, bc_ref, pre_ref, post_ref, wgate_ref, wbr_ref,
               wout_ref, o_ref):
    for r in range(0, x_ref.shape[0], FFN_ROWS):
        rows = slice(r, r + FFN_ROWS)
        x = x_ref[rows, :]
        xn = _rmsnorm(x, pre_ref[...]).astype(BF16)
        h = None
        for c in range(0, x_ref.shape[1], BACK_COLS):
            cols = slice(c, c + BACK_COLS)
            y = None
            for n, br in enumerate((bg_ref, bp_ref, ba_ref, bc_ref)):
                term = (_sigmoid(_dot(xn, wgate_ref[n, :, cols]))
                        * _dot(br[rows, :], wbr_ref[n, :, cols]))
                y = term if y is None else y + term
            part = _dot(y.astype(BF16), wout_ref[cols, :])
            h = part if h is None else h + part
        o_ref[rows, :] = x + _rmsnorm(h, post_ref[...])


def _back(x, bg, bp, ba, bc, pre_g, post_g, wgate, wbr, wout):
    n, d = x.shape
    w = BRANCH_W
    tm = min(FFN_BLOCK, n)
    assert tm % FFN_ROWS == 0
    full = lambda a: pl.BlockSpec(a.shape, lambda i: (0,) * a.ndim)
    resident = lambda a: pl.BlockSpec(a.shape, lambda i: (0,) * a.ndim,
                                      pipeline_mode=pl.Buffered(1))
    tok = lambda width: pl.BlockSpec((tm, width), lambda i: (i, 0))
    params = (pre_g, post_g, wgate, wbr, wout)
    return pl.pallas_call(
        _back_body,
        grid=(n // tm,),
        in_specs=([tok(d), tok(w), tok(w), tok(w), tok(w), full(pre_g), full(post_g)]
                  + [resident(a) for a in (wgate, wbr, wout)]),
        out_specs=tok(d),
        out_shape=jax.ShapeDtypeStruct((n, d), F32),
        compiler_params=pltpu.CompilerParams(
            dimension_semantics=("arbitrary",), vmem_limit_bytes=VMEM_LIMIT_BYTES),
        name="mixer_back",
    )(x, bg, bp, ba, bc, *params)


def _cast_body(w_ref, o_ref):
    o_ref[...] = w_ref[...].astype(BF16)


def _cast_bf16(w):
    layers, rows, cols = w.shape
    rb = CAST_ROWS
    assert rows % rb == 0
    spec = pl.BlockSpec((1, rb, cols), lambda l, r: (l, r, 0))
    return pl.pallas_call(
        _cast_body,
        grid=(layers, rows // rb),
        in_specs=[spec],
        out_specs=spec,
        out_shape=jax.ShapeDtypeStruct(w.shape, BF16),
        compiler_params=pltpu.CompilerParams(
            dimension_semantics=("arbitrary", "arbitrary"), vmem_limit_bytes=VMEM_LIMIT_BYTES),
        name="cast_weights",
    )(w)


def _pack_front_weights(w_in):
    d = w_in.shape[0]
    w = BRANCH_W
    o = 0
    uv = w_in[:, o:o + 2 * w]; o += 2 * w
    pz = w_in[:, o:o + w]; o += w
    qz = w_in[:, o:o + w]; o += w
    kz = w_in[:, o:o + w]; o += w
    vz = w_in[:, o:o + w]; o += w
    qi = w_in[:, o:o + IDX_HEADS * IDX_HD]; o += IDX_HEADS * IDX_HD
    ki = w_in[:, o:o + IDX_HD]; o += IDX_HD
    wi = w_in[:, o:o + IDX_HEADS]; o += IDX_HEADS
    cz = w_in[:, o:o + 2 * w]; o += 2 * w
    gz = w_in[:, o:]
    w1 = jnp.concatenate([uv, pz, kz, cz] + [ki] * (LANES // IDX_HD), axis=1).astype(BF16)
    wi_pad = jnp.pad(wi, ((0, 0), (0, _W2_ROWS - _WIT0 - IDX_HEADS)))
    w2t = jnp.concatenate([qz, vz, qi, wi_pad], axis=1).T.astype(BF16)
    wgate = gz.reshape(d, N_BRANCH, d).transpose(1, 0, 2).astype(BF16)
    return w1, w2t, wgate


def _block_diag(pw):
    g, c, _ = pw.shape
    out = jnp.zeros((g * c, g * c), pw.dtype)
    for k in range(g):
        out = out.at[k * c:(k + 1) * c, k * c:(k + 1) * c].set(pw[k])
    return out


def kernel(x, ffn1_pre_g, ffn1_post_g, ffn1_w_gu, ffn1_w_down, mix_pre_g, mix_post_g, w_in,
           gm_v_g, gm_ws, gm_b, pool_w, pool_scale, conv_dw, conv_b, conv_ln_g, conv_ln_b,
           w_branch, w_out, ffn2_pre_g, ffn2_post_g, ffn2_w_gu, ffn2_w_down, rel_bias):
    batch, seq, d = x.shape
    depth = w_in.shape[0]
    n = batch * seq
    tm = min(FRONT_BLOCK, seq)
    w = BRANCH_W
    row = lambda a: a.reshape(1, -1)
    tab = rel_bias.reshape(-1)

    ffn1_w = (_cast_bf16(ffn1_w_gu), _cast_bf16(ffn1_w_down))
    ffn2_w = (_cast_bf16(ffn2_w_gu), _cast_bf16(ffn2_w_down))

    xf = x.reshape(n, d)
    for l in range(depth):
        xf = _ffn(xf, row(ffn1_pre_g[l]), row(ffn1_post_g[l]), *ffn1_w, l)

        w1, w2t, wgate = _pack_front_weights(w_in[l])
        ws_cat = gm_ws[l].transpose(1, 0, 2).reshape(GM_CHUNK, GM_GROUPS * GM_CHUNK)
        gmb2d = jnp.repeat(gm_b[l].T, w // GM_GROUPS, axis=1)
        dw = jnp.pad(conv_dw[l], ((0, 1), (0, 0)))
        bg, bp, bc, k, ki, qt, vt, qit, wit = _front(
            xf, seq, row(mix_pre_g[l]), w1, w2t, row(gm_v_g[l]), ws_cat, gmb2d,
            _block_diag(pool_w[l]).astype(BF16), row(pool_scale[l]), dw, row(conv_b[l]),
            row(conv_ln_g[l]), row(conv_ln_b[l]), tm)
        ba = _attention(tab, qt, qit, wit, k.reshape(batch, seq, w),
                        ki.reshape(batch, seq, LANES), vt)
        xf = _back(xf, bg, bp, ba.reshape(n, w), bc, row(mix_pre_g[l]), row(mix_post_g[l]),
                   wgate, w_branch[l].astype(BF16), w_out[l].astype(BF16))

        xf = _ffn(xf, row(ffn2_pre_g[l]), row(ffn2_post_g[l]), *ffn2_w, l)
    return xf.reshape(batch, seq, d)
```

```python
import functools
import math

import jax
import jax.numpy as jnp
from jax import lax
from jax.experimental import pallas as pl
from jax.experimental.pallas import tpu as pltpu

F32 = jnp.float32
BF16 = jnp.bfloat16

EPS = 1e-6
BRANCH_W = 256
N_BRANCH = 4
GM_GROUPS = 4
GM_CHUNK = 128
POOL_WINDOWS = (2, 4, 8, 16)
ATT_HEADS = 4
ATT_HD = 64
IDX_HEADS = 4
IDX_HD = 32
TOPK_MAX = 256
REL_BUCKETS = 32
REL_MAX_DIST = 128
CONV_K = 31

VMEM_LIMIT_BYTES = 56 * 1024 * 1024
LANES = 128
SUBLANES = 8

FFN_CHUNK = 256
FFN_BLOCK = 1024
FFN_ROWS = 512
CAST_ROWS = 128
BACK_COLS = 256
FRONT_BLOCK = 1024
FRONT_ROWS = 512
CONV_HALO = 32
POOL_HALO = 16
CONV_ROWS = 128
ATT_BLOCK = 256
ATTEND_UNROLL = 6
COARSE_STEPS = 10
BISECT_STEPS = 8
BF16_ROWS = 16
SORT_GROUP = 8
BF16_INTERVAL = 2.0 ** -6
TINY = 1e-30
LOG2E = math.log2(math.e)
Q_SCALE = ATT_HD ** -0.5 * LOG2E
NEG = -1e30
M_INIT = -1e29


def _rmsnorm(x, g):
    return x * lax.rsqrt(jnp.mean(x * x, axis=-1, keepdims=True) + EPS) * g


def _sigmoid(x):
    return 1.0 / (1.0 + jnp.exp(-x))


def _gelu_tanh(x):
    return x * (0.5 * (1.0 + jnp.tanh(math.sqrt(2.0 / math.pi) * (x + 0.044715 * (x ** 3)))))


def _dot(a, b):
    return jnp.dot(a, b, preferred_element_type=F32)


def _fold8(x, op):
    return _fold(x, op, SUBLANES)


def _fold(x, op, rows):
    parts = [x[j * rows:(j + 1) * rows] for j in range(x.shape[0] // rows)]
    while len(parts) > 1:
        nxt = [op(parts[j], parts[j + 1]) for j in range(0, len(parts) - 1, 2)]
        if len(parts) % 2:
            nxt.append(parts[-1])
        parts = nxt
    return parts[0]


def _sorting_network(n):
    net = []

    def merge(lo, n, r):
        step = r * 2
        if step < n:
            merge(lo, n, step)
            merge(lo + r, n, step)
            net.extend((i, i + r) for i in range(lo + r, lo + n - r, step))
        else:
            net.append((lo, lo + r))

    def sort(lo, n):
        if n > 1:
            sort(lo, n // 2)
            sort(lo + n // 2, n // 2)
            merge(lo, n, 1)

    sort(0, n)
    return tuple(net)


def _sort_desc(vals):
    vals = list(vals)
    for a, b in _sorting_network(len(vals)):
        vals[a], vals[b] = jnp.maximum(vals[a], vals[b]), jnp.minimum(vals[a], vals[b])
    return vals


def _pick(masks, cands):
    for m in reversed(masks):
        cands = [jnp.where(m, cands[2 * i + 1], cands[2 * i]) for i in range(len(cands) // 2)]
    return cands[0]


def _search_sorted(s, thr):
    g = len(s)
    ge, pivot = [], None
    for k in range(g.bit_length() - 1):
        pivot = _pick(ge, [s[v * (g >> k) + (g >> (k + 1)) - 1] for v in range(1 << k)])
        ge.append(pivot >= thr)
    return ge, pivot


def _count_ge_sorted(s, thr, dtype):
    g = len(s)
    c = lambda v: jnp.full((), v, dtype)
    ge, _ = _search_sorted(s, thr)
    terms = [jnp.where(m, c(g >> (l + 1)), c(0)) for l, m in enumerate(ge)]
    terms.append(jnp.where(s[g - 1] >= thr, c(1), c(0)))
    while len(terms) > 1:
        terms = [terms[i] + terms[i + 1] for i in range(0, len(terms) - 1, 2)] + (
            [terms[-1]] if len(terms) % 2 else [])
    return terms[0]


def _max_below_sorted(s, thr):
    g = len(s)
    ge, pivot = _search_sorted(s, thr)
    after = [s[2 * v + 1] for v in range(g // 2)]
    after[-1] = jnp.where(s[g - 1] < thr, s[g - 1], -jnp.inf)
    return jnp.where(ge[-1], _pick(ge[:-1], after), pivot)


def _ffn_body(x_ref, pre_ref, post_ref, wgu_ref, wd_ref, o_ref, xn_ref, acc_ref):
    f = wd_ref.shape[0]
    tm = x_ref.shape[0]
    halves = [slice(r, r + FFN_ROWS) for r in range(0, tm, FFN_ROWS)]
    for rows in halves:
        xn_ref[rows, :] = _rmsnorm(x_ref[rows, :], pre_ref[...]).astype(BF16)
    for c in range(0, f, FFN_CHUNK):
        for rows in halves:
            xn = xn_ref[rows, :]
            a = _dot(xn, wgu_ref[:, c:c + FFN_CHUNK])
            b = _dot(xn, wgu_ref[:, f + c:f + c + FFN_CHUNK])
            hm = (a * _sigmoid(a) * b).astype(BF16)
            down = _dot(hm, wd_ref[c:c + FFN_CHUNK, :])
            acc_ref[rows, :] = down if c == 0 else acc_ref[rows, :] + down
    for rows in halves:
        o_ref[rows, :] = x_ref[rows, :] + 0.5 * _rmsnorm(acc_ref[rows, :], post_ref[...])


def _ffn(x, pre_g, post_g, wgu, wd, layer):
    n, d = x.shape
    f = wd.shape[1]
    tm = min(FFN_BLOCK, n)
    assert f % FFN_CHUNK == 0 and wgu.shape[1:] == (d, 2 * f) and tm % FFN_ROWS == 0
    full = lambda shape: pl.BlockSpec(shape, lambda i: (0,) * len(shape))
    resident = lambda rows, cols: pl.BlockSpec((None, rows, cols), lambda i: (layer, 0, 0),
                                               pipeline_mode=pl.Buffered(1))
    return pl.pallas_call(
        _ffn_body,
        grid=(n // tm,),
        in_specs=[
            pl.BlockSpec((tm, d), lambda i: (i, 0)),
            full((1, d)), full((1, d)), resident(d, 2 * f), resident(f, d),
        ],
        out_specs=pl.BlockSpec((tm, d), lambda i: (i, 0)),
        out_shape=jax.ShapeDtypeStruct((n, d), F32),
        scratch_shapes=[pltpu.VMEM((tm, d), BF16), pltpu.VMEM((tm, d), F32)],
        compiler_params=pltpu.CompilerParams(
            dimension_semantics=("arbitrary",), vmem_limit_bytes=VMEM_LIMIT_BYTES),
        name="ffn",
    )(x, pre_g, post_g, wgu, wd)


_UV0, _PZ0, _K0, _CZ0, _KI0, _W1_COLS = 0, 512, 768, 1024, 1536, 1664
_QT0, _VT0, _QIT0, _WIT0, _W2_ROWS = 0, 256, 512, 640, 656


def _front_body(blocks_per_seq,
                x_ref, pre_ref, w1_ref, w2t_ref, gmvg_ref, ws_ref, gmb_ref, poolw_ref,
                pscale_ref, dw_ref, cb_ref, lng_ref, lnb_ref,
                bg_ref, bp_ref, bc_ref, k_ref, ki_ref, qt_ref, vt_ref, qit_ref, wit_ref,
                hbuf, pbuf):
    tm_step = x_ref.shape[0]
    tm = min(FRONT_ROWS, tm_step)
    w = BRANCH_W
    j = pl.program_id(0) % blocks_per_seq

    @pl.when(j == 0)
    def _():
        hbuf[...] = jnp.zeros_like(hbuf)
        pbuf[0:POOL_HALO, :] = jnp.zeros((POOL_HALO, w), F32)

    row = lax.broadcasted_iota(jnp.int32, (GM_CHUNK, GM_GROUPS * GM_CHUNK), 0)
    col = lax.broadcasted_iota(jnp.int32, (GM_CHUNK, GM_GROUPS * GM_CHUNK), 1)
    wsm = jnp.where((col % GM_CHUNK) <= row, ws_ref[...], 0.0).astype(BF16)
    lane_group = lax.broadcasted_iota(jnp.int32, (GM_CHUNK, w), 1) // (w // GM_GROUPS)

    for r0 in range(0, tm_step, tm):
        rows = slice(r0, r0 + tm)
        xn = _rmsnorm(x_ref[rows, :], pre_ref[...]).astype(BF16)

        k_ref[rows, :] = _dot(xn, w1_ref[:, _K0:_K0 + w]).astype(BF16)
        ki_ref[rows, :] = _dot(xn, w1_ref[:, _KI0:_KI0 + LANES]).astype(BF16)
        zt = lax.dot_general(w2t_ref[...], xn, (((1,), (1,)), ((), ())),
                             preferred_element_type=F32)
        for cc in range(tm // ATT_BLOCK):
            sl = slice(cc * ATT_BLOCK, (cc + 1) * ATT_BLOCK)
            oc = r0 // ATT_BLOCK + cc
            qt_ref[0, oc] = (zt[_QT0:_QT0 + w, sl] * Q_SCALE).astype(BF16)
            vt_ref[0, oc] = zt[_VT0:_VT0 + w, sl].astype(BF16)
            qit_ref[0, oc] = zt[_QIT0:_QIT0 + LANES, sl].astype(BF16)
            wit_ref[0, oc] = zt[_WIT0:_WIT0 + SUBLANES, sl]

        u = _gelu_tanh(_dot(xn, w1_ref[:, _UV0:_UV0 + w]))
        vv = _rmsnorm(_gelu_tanh(_dot(xn, w1_ref[:, _UV0 + w:_UV0 + 2 * w])), gmvg_ref[...])
        for c in range(tm // GM_CHUNK):
            rs = slice(c * GM_CHUNK, (c + 1) * GM_CHUNK)
            vc = vv[rs, :]
            stacked = jnp.concatenate(
                [jnp.where(lane_group == g, vc, 0.0) for g in range(GM_GROUPS)], axis=0).astype(BF16)
            mixed = _dot(wsm, stacked) + gmb_ref[...]
            bg_ref[r0 + c * GM_CHUNK:r0 + (c + 1) * GM_CHUNK, :] = (u[rs, :] * mixed).astype(BF16)

        p = _dot(xn, w1_ref[:, _PZ0:_PZ0 + w])
        base = POOL_HALO + r0
        pbuf[base:base + tm, :] = p
        pos1 = (j * tm_step + r0 + lax.broadcasted_iota(jnp.int32, (tm, LANES), 0) + 1).astype(F32)
        small = lax.broadcasted_iota(jnp.int32, (tm, LANES), 1) < w // len(POOL_WINDOWS)
        w2, w4, w8, w16 = (float(v) for v in POOL_WINDOWS)
        lo_half = slice(0, LANES)
        s2 = p[:, lo_half] + pbuf[base - 1:base - 1 + tm, lo_half]
        s4 = s2 + pbuf[base - 2:base - 2 + tm, lo_half] + pbuf[base - 3:base - 3 + tm, lo_half]
        d_lo = (jnp.where(small, s2, s4) / jnp.minimum(pos1, jnp.where(small, w2, w4))
                - p[:, lo_half])
        hi_half = slice(LANES, 2 * LANES)
        ext = None
        for shift in range(SUBLANES):
            term = pbuf[base - SUBLANES - shift:base - shift + tm, hi_half]
            ext = term if ext is None else ext + term
        s8 = ext[SUBLANES:, :]
        s16 = s8 + ext[0:tm, :]
        d_hi = (jnp.where(small, s8, s16) / jnp.minimum(pos1, jnp.where(small, w8, w16))
                - p[:, hi_half])
        dpool = jnp.concatenate([d_lo, d_hi], axis=1).astype(BF16)
        bp_ref[rows, :] = (_dot(dpool, poolw_ref[...]) * pscale_ref[...]).astype(BF16)

        hbuf[CONV_HALO + r0:CONV_HALO + r0 + tm, :] = (
            _dot(xn, w1_ref[:, _CZ0:_CZ0 + w]) * _sigmoid(_dot(xn, w1_ref[:, _CZ0 + w:_CZ0 + 2 * w])))
        lead = CONV_HALO - (CONV_K - 1)
        for r in range(r0, r0 + tm, CONV_ROWS):
            acc = None
            for shift in range(SUBLANES):
                part = None
                for t in range(CONV_K):
                    if (lead + t) % SUBLANES == shift:
                        start = r + lead + t - shift
                        term = hbuf[start:start + CONV_ROWS + SUBLANES, :] * dw_ref[t:t + 1, :]
                        part = term if part is None else part + term
                part = part[shift:shift + CONV_ROWS, :]
                acc = part if acc is None else acc + part
            hc = acc + cb_ref[...]
            mu = jnp.mean(hc, axis=-1, keepdims=True)
            xc = hc - mu
            yn = xc * lax.rsqrt(jnp.mean(xc * xc, axis=-1, keepdims=True) + EPS)
            yn = yn * lng_ref[...] + lnb_ref[...]
            bc_ref[r:r + CONV_ROWS, :] = (yn * _sigmoid(yn)).astype(BF16)

    pbuf[0:POOL_HALO, :] = pbuf[tm_step:tm_step + POOL_HALO, :]
    hbuf[0:CONV_HALO, :] = hbuf[tm_step:tm_step + CONV_HALO, :]


def _front(x, seq, pre_g, w1, w2t, gmvg, ws_cat, gmb2d, poolw, pscale, dw, cb, lng, lnb, tm):
    n, d = x.shape
    batch = n // seq
    bps = seq // tm
    w = BRANCH_W
    assert POOL_WINDOWS == (2, 4, 8, 16) and w == 2 * LANES and POOL_HALO >= 2 * SUBLANES - 1
    nch = seq // ATT_BLOCK
    cpb = tm // ATT_BLOCK
    full = lambda a: pl.BlockSpec(a.shape, lambda i: (0,) * a.ndim)
    tok = lambda width: pl.BlockSpec((tm, width), lambda i: (i, 0))
    chunked = lambda rows: pl.BlockSpec((1, cpb, rows, ATT_BLOCK),
                                        lambda i: (i // bps, i % bps, 0, 0))
    params = (pre_g, w1, w2t, gmvg, ws_cat, gmb2d, poolw, pscale, dw, cb, lng, lnb)
    out_shape = (
        jax.ShapeDtypeStruct((n, w), BF16),
        jax.ShapeDtypeStruct((n, w), BF16),
        jax.ShapeDtypeStruct((n, w), BF16),
        jax.ShapeDtypeStruct((n, w), BF16),
        jax.ShapeDtypeStruct((n, LANES), BF16),
        jax.ShapeDtypeStruct((batch, nch, w, ATT_BLOCK), BF16),
        jax.ShapeDtypeStruct((batch, nch, w, ATT_BLOCK), BF16),
        jax.ShapeDtypeStruct((batch, nch, LANES, ATT_BLOCK), BF16),
        jax.ShapeDtypeStruct((batch, nch, SUBLANES, ATT_BLOCK), F32),
    )
    out_specs = (tok(w), tok(w), tok(w), tok(w), tok(LANES),
                 chunked(w), chunked(w), chunked(LANES), chunked(SUBLANES))
    return pl.pallas_call(
        functools.partial(_front_body, bps),
        grid=(n // tm,),
        in_specs=[tok(d)] + [full(a) for a in params],
        out_specs=out_specs,
        out_shape=out_shape,
        scratch_shapes=[pltpu.VMEM((tm + CONV_HALO + SUBLANES, w), F32),
                        pltpu.VMEM((tm + POOL_HALO, w), F32)],
        compiler_params=pltpu.CompilerParams(
            dimension_semantics=("arbitrary",), vmem_limit_bytes=VMEM_LIMIT_BYTES),
        name="mixer_front",
    )(x, *params)


def _attn_body(topk,
               tab_ref, qt_ref, qit_ref, wit_ref, k_ref, ki_ref, vt_ref, o_ref,
               s_ref, ssort_ref, s16sort_ref, bd_ref, bp_ref, tri_ref, qm_ref, qim_ref, acc_ref,
               lg0_ref, lg1_ref):
    qb = ATT_BLOCK
    nch = vt_ref.shape[1]
    b = pl.program_id(0)
    i = pl.program_id(1)
    key = lax.broadcasted_iota(jnp.int32, (qb, qb), 0)
    qry = lax.broadcasted_iota(jnp.int32, (qb, qb), 1)

    @pl.when((b == 0) & (i == 0))
    def _():
        tri_ref[...] = jnp.where(qry <= key, 1.0, 0.0).astype(BF16)
        max_exact = REL_BUCKETS // 2
        for ref, off in ((bd_ref, 0), (bp_ref, qb)):
            n = jnp.maximum(qry - key + off, 0)
            large = max_exact + (
                jnp.log(jnp.maximum(n, 1).astype(F32) / max_exact)
                / math.log(REL_MAX_DIST / max_exact) * (REL_BUCKETS - max_exact)).astype(jnp.int32)
            bucket = jnp.where(n < max_exact, n, jnp.minimum(large, REL_BUCKETS - 1))
            for h in range(ATT_HEADS):
                bias = jnp.zeros((qb, qb), F32)
                for k in range(REL_BUCKETS):
                    bias = jnp.where(bucket == k, tab_ref[k * ATT_HEADS + h], bias)
                ref[h] = (bias - tab_ref[(REL_BUCKETS - 1) * ATT_HEADS + h]) * LOG2E

    qt = qt_ref[0, 0]
    qit = qit_ref[0, 0]
    q_head = lax.broadcasted_iota(jnp.int32, qt.shape, 0) // ATT_HD
    qi_head = lax.broadcasted_iota(jnp.int32, qit.shape, 0) // IDX_HD
    for h in range(ATT_HEADS):
        qm_ref[h] = jnp.where(q_head == h, qt, jnp.zeros_like(qt))
    for h in range(IDX_HEADS):
        qim_ref[h] = jnp.where(qi_head == h, qit, jnp.zeros_like(qit))
    wv = wit_ref[0, 0] * ((IDX_HEADS ** -0.5) * (IDX_HD ** -0.5))

    t_pos = i * qb + qry
    n_pairs = (i + 2) // 2

    def keys_of(c):
        return pl.ds(pl.multiple_of(c * qb, qb), qb)

    def score_chunk(c, masked):
        kic = ki_ref[0, keys_of(jnp.minimum(c, nch - 1)), :]
        sc = jnp.zeros((qb, qb), F32)
        for h in range(IDX_HEADS):
            sc = sc + wv[h:h + 1, :] * jnp.maximum(_dot(kic, qim_ref[h]), 0.0)
        if masked:
            valid = (c * qb + key) <= t_pos
            bot = _fold8(jnp.where(valid, sc, jnp.inf), jnp.minimum)
            sc = jnp.where(valid, sc, -jnp.inf)
        s_ref[c] = sc
        slabs = [sc[r:r + SUBLANES] for r in range(0, qb, SUBLANES)]
        groups = [_sort_desc(slabs[g:g + SORT_GROUP]) for g in range(0, len(slabs), SORT_GROUP)]
        ssort_ref[c] = jnp.concatenate([slab for grp in groups for slab in grp], axis=0)
        s16sort_ref[c] = jnp.concatenate(
            [slab for g in range(0, len(groups), 2) for pair in zip(groups[g], groups[g + 1])
             for slab in pair], axis=0).astype(BF16)
        top = functools.reduce(jnp.maximum, [grp[0] for grp in groups])
        if not masked:
            bot = functools.reduce(jnp.minimum, [grp[-1] for grp in groups])
        return top, bot

    def score_chunks(first, count, carry, masked):
        top, bot = carry
        for j in range(count):
            top_j, bot_j = score_chunk(first + j, masked)
            top, bot = jnp.maximum(top, top_j), jnp.minimum(bot, bot_j)
        return top, bot

    triples = (n_pairs - 1) // 3
    carry = lax.fori_loop(
        0, triples, lambda g, cr: score_chunks(6 * g, 6, cr, False),
        (jnp.full((SUBLANES, qb), -jnp.inf, F32), jnp.full((SUBLANES, qb), jnp.inf, F32)))
    carry = lax.fori_loop(
        3 * triples, n_pairs - 1, lambda p, cr: score_chunks(2 * p, 2, cr, False), carry)
    last = 2 * (n_pairs - 1)
    carry = score_chunks(last, 1, carry, True)

    def after_diagonal(cr):
        ssort_ref[last + 1] = jnp.full((qb, qb), -jnp.inf, F32)
        s16sort_ref[last + 1] = jnp.full((qb, qb), -jnp.inf, BF16)
        return cr

    top, bot = lax.cond(i % 2 == 1, lambda cr: score_chunks(last + 1, 1, cr, True),
                        after_diagonal, carry)
    rmin = jnp.min(bot, axis=0, keepdims=True)
    rmax = jnp.max(top, axis=0, keepdims=True)
    cap16 = rmax.astype(BF16).astype(F32)

    n_valid = (i * qb + lax.broadcasted_iota(jnp.int32, (1, qb), 1) + 1).astype(F32)
    kp = jnp.minimum(float(topk), n_valid)

    def count_sorted(x, thr, rows, dtype):
        counts = [_count_ge_sorted([x[r + j * rows:r + (j + 1) * rows] for j in range(SORT_GROUP)],
                                   thr, dtype)
                  for r in range(0, qb, SORT_GROUP * rows)]
        return functools.reduce(jnp.add, counts)

    def count_ge(thr):
        def body(p, acc):
            return acc + (count_sorted(ssort_ref[2 * p], thr, SUBLANES, F32)
                          + count_sorted(ssort_ref[2 * p + 1], thr, SUBLANES, F32))
        acc = lax.fori_loop(0, n_pairs, body, jnp.zeros((SUBLANES, qb), F32))
        return jnp.sum(acc, axis=0, keepdims=True)

    def count_ge16(thr16):
        def body(p, acc):
            both = (count_sorted(s16sort_ref[2 * p], thr16, BF16_ROWS, BF16)
                    + count_sorted(s16sort_ref[2 * p + 1], thr16, BF16_ROWS, BF16))
            return acc + both.astype(F32)
        acc = lax.fori_loop(0, n_pairs, body, jnp.zeros((BF16_ROWS, qb), F32))
        return jnp.sum(acc, axis=0, keepdims=True)

    def coarse(_, carry):
        lo, hi = carry
        mid16 = (0.5 * lo + 0.5 * jnp.minimum(hi, cap16)).astype(BF16)
        feas = count_ge16(mid16) >= kp
        mid = mid16.astype(F32)
        return jnp.where(feas, mid, lo), jnp.where(feas, hi, mid)

    lo16, hi = lax.fori_loop(0, COARSE_STEPS, coarse,
                             (rmin.astype(BF16).astype(F32), jnp.full((1, qb), jnp.inf, F32)))
    lo = lo16 - (jnp.abs(lo16) * BF16_INTERVAL + TINY)
    chi = count_ge(hi)

    def bisect(_, carry):
        lo, hi, chi = carry
        mid = 0.5 * lo + 0.5 * jnp.minimum(hi, rmax)
        cnt = count_ge(mid)
        feas = cnt >= kp
        return (jnp.where(feas, mid, lo), jnp.where(feas, hi, mid), jnp.where(feas, chi, cnt))

    lo, hi, chi = lax.fori_loop(0, BISECT_STEPS, bisect, (lo, hi, chi))

    def max_below(thr):
        def body(p, acc):
            for c in (2 * p, 2 * p + 1):
                x = ssort_ref[c]
                for r in range(0, qb, SORT_GROUP * SUBLANES):
                    group = [x[r + j * SUBLANES:r + (j + 1) * SUBLANES] for j in range(SORT_GROUP)]
                    acc = jnp.maximum(acc, _max_below_sorted(group, thr))
            return acc
        acc = lax.fori_loop(0, n_pairs, body, jnp.full((SUBLANES, qb), -jnp.inf, F32))
        return jnp.max(acc, axis=0, keepdims=True)

    def finish_cond(state):
        return state[4] > 0.0

    def finish_body(state):
        hi, chi, tau, done, _ = state
        m = max_below(hi)
        cnt = count_ge(m)
        feas = cnt >= kp
        active = done < 0.5
        tau = jnp.where(active & feas, m, tau)
        hi = jnp.where(active & (~feas), m, hi)
        chi = jnp.where(active & (~feas), cnt, chi)
        done = jnp.where(active & feas, 1.0, done)
        return hi, chi, tau, done, jnp.max(1.0 - done)

    hi, chi, tau, _, _ = lax.while_loop(
        finish_cond, finish_body,
        (hi, chi, lo, jnp.zeros((1, qb), F32), jnp.float32(1.0)))
    need = kp - chi

    acc_ref[...] = jnp.zeros_like(acc_ref)
    heads = range(ATT_HEADS)

    def stage_a(c, bias, live, buf, need_left):
        x = s_ref[c]
        eq = jnp.where(x == tau, 1.0, 0.0)
        if live is not None:
            eq = jnp.where(live, eq, 0.0)
        prefix = _dot(tri_ref[...], eq.astype(BF16))
        maskadd = jnp.where(x >= jnp.where(prefix <= need_left, tau, hi), 0.0, NEG)
        if live is not None:
            maskadd = jnp.where(live, maskadd, NEG)
        kc = k_ref[0, keys_of(c), :]
        mxs = []
        for h in heads:
            lg = _dot(kc, qm_ref[h]) + maskadd
            if bias is not None:
                lg = lg + bias[h]
            buf[h] = lg
            mxs.append(jnp.max(_fold8(lg, jnp.maximum), axis=0, keepdims=True))
        return tuple(mxs), need_left - prefix[qb - 1:qb, :]

    ones_rows = jnp.ones((2 * SUBLANES, qb), BF16)

    def stage_b(c, buf, mxs, ms, ls):
        vtc = vt_ref[0, c]
        new_ms = [jnp.maximum(ms[h], mxs[h]) for h in heads]
        alphas = [jnp.exp2(ms[h] - new_ms[h]) for h in heads]
        pvs = [_dot(jnp.concatenate([vtc[h * ATT_HD:(h + 1) * ATT_HD, :], ones_rows], axis=0),
                    jnp.exp2(buf[h] - new_ms[h]).astype(BF16)) for h in heads]
        new_ls = [alphas[h] * ls[h] + pvs[h][ATT_HD:ATT_HD + 1, :] for h in heads]
        pv = jnp.concatenate([pvs[h][0:ATT_HD, :] for h in heads], axis=0)
        alpha_rows = jnp.concatenate(
            [jnp.broadcast_to(alphas[h], (ATT_HD, qb)) for h in heads], axis=0)
        acc_ref[...] = acc_ref[...] * alpha_rows + pv
        return tuple(new_ms), tuple(new_ls)

    lg_refs = (lg0_ref, lg1_ref)

    def run_stages(stages, carry, pending=True):
        ms, ls, need_left, pend_mx, pend_c = carry
        for k, (c, bias, live) in enumerate(stages):
            mx, need_left = stage_a(c, bias, live, lg_refs[k % 2], need_left)
            if k > 0 or pending:
                ms, ls = stage_b(pend_c, lg_refs[1 - k % 2], pend_mx, ms, ls)
            pend_mx, pend_c = mx, c
        return ms, ls, need_left, pend_mx, pend_c

    def far_group(g):
        first = jnp.asarray(ATTEND_UNROLL * g, jnp.int32)
        return [(first + j, None, None) for j in range(ATTEND_UNROLL)]

    n_far = jnp.maximum(i - 1, 0)
    n_groups = n_far // ATTEND_UNROLL
    m_init = tuple(jnp.full((1, qb), M_INIT, F32) for _ in heads)
    carry = (m_init, tuple(jnp.zeros((1, qb), F32) for _ in heads), need, m_init, jnp.int32(0))

    def empty_pipeline(cr):
        lg1_ref[...] = jnp.full(lg1_ref.shape, NEG, F32)
        return cr

    carry = lax.cond(n_groups >= 1, lambda cr: run_stages(far_group(0), cr, pending=False),
                     empty_pipeline, carry)
    carry = lax.fori_loop(1, n_groups, lambda g, cr: run_stages(far_group(g), cr), carry)
    far_done = n_groups * ATTEND_UNROLL

    def tail(n_left):
        def run(cr):
            stages = [(far_done + j, None, None) for j in range(n_left)]
            stages += [(jnp.maximum(i - 1, 0), bp_ref, i >= 1), (i, bd_ref, None)]
            ms, ls, _, pend_mx, pend_c = run_stages(stages, cr)
            return stage_b(pend_c, lg_refs[1 - len(stages) % 2], pend_mx, ms, ls)[1]
        return run

    ls = lax.switch(n_far - far_done, [tail(r) for r in range(ATTEND_UNROLL)], carry)

    for h in range(ATT_HEADS):
        rows = slice(h * ATT_HD, (h + 1) * ATT_HD)
        acc_ref[rows, :] = acc_ref[rows, :] / ls[h]
    o_ref[0] = acc_ref[...].T.astype(BF16)


def _attention(tab, qt, qit, wit, k, ki, vt):
    batch, nch, w, qb = qt.shape
    seq = nch * qb
    topk = min(TOPK_MAX, seq // 4)
    per_block = lambda rows: pl.BlockSpec((1, 1, rows, qb), lambda b, i: (b, i, 0, 0))
    return pl.pallas_call(
        functools.partial(_attn_body, topk),
        grid=(batch, nch),
        in_specs=[
            pl.BlockSpec(memory_space=pltpu.SMEM),
            per_block(w), per_block(LANES), per_block(SUBLANES),
            pl.BlockSpec((1, seq, w), lambda b, i: (b, 0, 0)),
            pl.BlockSpec((1, seq, LANES), lambda b, i: (b, 0, 0)),
            pl.BlockSpec((1, nch, w, qb), lambda b, i: (b, 0, 0, 0)),
        ],
        out_specs=pl.BlockSpec((1, qb, w), lambda b, i: (b, i, 0)),
        out_shape=jax.ShapeDtypeStruct((batch, seq, w), BF16),
        scratch_shapes=[
            pltpu.VMEM((nch + 1, qb, qb), F32),
            pltpu.VMEM((nch + 1, qb, qb), F32),
            pltpu.VMEM((nch + 1, qb, qb), BF16),
            pltpu.VMEM((ATT_HEADS, qb, qb), F32),
            pltpu.VMEM((ATT_HEADS, qb, qb), F32),
            pltpu.VMEM((qb, qb), BF16),
            pltpu.VMEM((ATT_HEADS, w, qb), BF16),
            pltpu.VMEM((IDX_HEADS, LANES, qb), BF16),
            pltpu.VMEM((w, qb), F32),
            pltpu.VMEM((ATT_HEADS, qb, qb), F32),
            pltpu.VMEM((ATT_HEADS, qb, qb), F32),
        ],
        compiler_params=pltpu.CompilerParams(
            dimension_semantics=("arbitrary", "arbitrary"), vmem_limit_bytes=VMEM_LIMIT_BYTES),
        name="sparse_attention",
    )(tab, qt, qit, wit, k, ki, vt)


def _back_body(x_ref, bg_ref, bp_ref, ba_ref, bc_ref, pre_ref, post_ref, wgate_ref, wbr_ref,
               wout_ref, o_ref):
    for r in range(0, x_ref.shape[0], FFN_ROWS):
        rows = slice(r, r + FFN_ROWS)
        x = x_ref[rows, :]
        xn = _rmsnorm(x, pre_ref[...]).astype(BF16)
        h = None
        for c in range(0, x_ref.shape[1], BACK_COLS):
            cols = slice(c, c + BACK_COLS)
            y = None
            for n, br in enumerate((bg_ref, bp_ref, ba_ref, bc_ref)):
                term = (_sigmoid(_dot(xn, wgate_ref[n, :, cols]))
                        * _dot(br[rows, :], wbr_ref[n, :, cols]))
                y = term if y is None else y + term
            part = _dot(y.astype(BF16), wout_ref[cols, :])
            h = part if h is None else h + part
        o_ref[rows, :] = x + _rmsnorm(h, post_ref[...])


def _back(x, bg, bp, ba, bc, pre_g, post_g, wgate, wbr, wout):
    n, d = x.shape
    w = BRANCH_W
    tm = min(FFN_BLOCK, n)
    assert tm % FFN_ROWS == 0
    full = lambda a: pl.BlockSpec(a.shape, lambda i: (0,) * a.ndim)
    resident = lambda a: pl.BlockSpec(a.shape, lambda i: (0,) * a.ndim,
                                      pipeline_mode=pl.Buffered(1))
    tok = lambda width: pl.BlockSpec((tm, width), lambda i: (i, 0))
    params = (pre_g, post_g, wgate, wbr, wout)
    return pl.pallas_call(
        _back_body,
        grid=(n // tm,),
        in_specs=([tok(d), tok(w), tok(w), tok(w), tok(w), full(pre_g), full(post_g)]
                  + [resident(a) for a in (wgate, wbr, wout)]),
        out_specs=tok(d),
        out_shape=jax.ShapeDtypeStruct((n, d), F32),
        compiler_params=pltpu.CompilerParams(
            dimension_semantics=("arbitrary",), vmem_limit_bytes=VMEM_LIMIT_BYTES),
        name="mixer_back",
    )(x, bg, bp, ba, bc, *params)


def _cast_body(w_ref, o_ref):
    o_ref[...] = w_ref[...].astype(BF16)


def _cast_bf16(w):
    layers, rows, cols = w.shape
    rb = CAST_ROWS
    assert rows % rb == 0
    spec = pl.BlockSpec((1, rb, cols), lambda l, r: (l, r, 0))
    return pl.pallas_call(
        _cast_body,
        grid=(layers, rows // rb),
        in_specs=[spec],
        out_specs=spec,
        out_shape=jax.ShapeDtypeStruct(w.shape, BF16),
        compiler_params=pltpu.CompilerParams(
            dimension_semantics=("arbitrary", "arbitrary"), vmem_limit_bytes=VMEM_LIMIT_BYTES),
        name="cast_weights",
    )(w)


def _pack_front_weights(w_in):
    d = w_in.shape[0]
    w = BRANCH_W
    o = 0
    uv = w_in[:, o:o + 2 * w]; o += 2 * w
    pz = w_in[:, o:o + w]; o += w
    qz = w_in[:, o:o + w]; o += w
    kz = w_in[:, o:o + w]; o += w
    vz = w_in[:, o:o + w]; o += w
    qi = w_in[:, o:o + IDX_HEADS * IDX_HD]; o += IDX_HEADS * IDX_HD
    ki = w_in[:, o:o + IDX_HD]; o += IDX_HD
    wi = w_in[:, o:o + IDX_HEADS]; o += IDX_HEADS
    cz = w_in[:, o:o + 2 * w]; o += 2 * w
    gz = w_in[:, o:]
    w1 = jnp.concatenate([uv, pz, kz, cz] + [ki] * (LANES // IDX_HD), axis=1).astype(BF16)
    wi_pad = jnp.pad(wi, ((0, 0), (0, _W2_ROWS - _WIT0 - IDX_HEADS)))
    w2t = jnp.concatenate([qz, vz, qi, wi_pad], axis=1).T.astype(BF16)
    wgate = gz.reshape(d, N_BRANCH, d).transpose(1, 0, 2).astype(BF16)
    return w1, w2t, wgate


def _block_diag(pw):
    g, c, _ = pw.shape
    out = jnp.zeros((g * c, g * c), pw.dtype)
    for k in range(g):
        out = out.at[k * c:(k + 1) * c, k * c:(k + 1) * c].set(pw[k])
    return out


def kernel(x, ffn1_pre_g, ffn1_post_g, ffn1_w_gu, ffn1_w_down, mix_pre_g, mix_post_g, w_in,
           gm_v_g, gm_ws, gm_b, pool_w, pool_scale, conv_dw, conv_b, conv_ln_g, conv_ln_b,
           w_branch, w_out, ffn2_pre_g, ffn2_post_g, ffn2_w_gu, ffn2_w_down, rel_bias):
    batch, seq, d = x.shape
    depth = w_in.shape[0]
    n = batch * seq
    tm = min(FRONT_BLOCK, seq)
    w = BRANCH_W
    row = lambda a: a.reshape(1, -1)
    tab = rel_bias.reshape(-1)

    ffn1_w = (_cast_bf16(ffn1_w_gu), _cast_bf16(ffn1_w_down))
    ffn2_w = (_cast_bf16(ffn2_w_gu), _cast_bf16(ffn2_w_down))

    xf = x.reshape(n, d)
    for l in range(depth):
        xf = _ffn(xf, row(ffn1_pre_g[l]), row(ffn1_post_g[l]), *ffn1_w, l)

        w1, w2t, wgate = _pack_front_weights(w_in[l])
        ws_cat = gm_ws[l].transpose(1, 0, 2).reshape(GM_CHUNK, GM_GROUPS * GM_CHUNK)
        gmb2d = jnp.repeat(gm_b[l].T, w // GM_GROUPS, axis=1)
        dw = jnp.pad(conv_dw[l], ((0, 1), (0, 0)))
        bg, bp, bc, k, ki, qt, vt, qit, wit = _front(
            xf, seq, row(mix_pre_g[l]), w1, w2t, row(gm_v_g[l]), ws_cat, gmb2d,
            _block_diag(pool_w[l]).astype(BF16), row(pool_scale[l]), dw, row(conv_b[l]),
            row(conv_ln_g[l]), row(conv_ln_b[l]), tm)
        ba = _attention(tab, qt, qit, wit, k.reshape(batch, seq, w),
                        ki.reshape(batch, seq, LANES), vt)
        xf = _back(xf, bg, bp, ba.reshape(n, w), bc, row(mix_pre_g[l]), row(mix_post_g[l]),
                   wgate, w_branch[l].astype(BF16), w_out[l].astype(BF16))

        xf = _ffn(xf, row(ffn2_pre_g[l]), row(ffn2_post_g[l]), *ffn2_w, l)
    return xf.reshape(batch, seq, d)
```
